```python
import math
import jax, jax.numpy as jnp
from jax import lax
import numpy as np

D_MODEL = 2048
BATCH = 4
SEQ = 4096
DEPTH = 2

N_MIXERS = 2
N_RWKV_LAYERS = (DEPTH + 1) // 2
N_ATTN_LAYERS = DEPTH // 2

RWKV_HEAD_SIZE = 64
RWKV_HEADS = D_MODEL // RWKV_HEAD_SIZE
DECAY_LORA = 96
AAA_LORA = 96
GATE_LORA = 256
N_SHIFT_MIX = 6
GN_EPS = 64e-5

ATTN_HEAD_DIM = 128
ATTN_HEADS_PER_GROUP = D_MODEL // ATTN_HEAD_DIM
ATTN_GROUP_WIDTH = ATTN_HEADS_PER_GROUP * ATTN_HEAD_DIM
DILATED_GROUPS = ((128, 1), (512, 4), (2048, 16))
N_GROUPS = len(DILATED_GROUPS)
MASK_VALUE = -1e30

D_FF = 5632
CONV_WIDTH = 3

LN_EPS = 1e-5
DEEPNORM_ALPHA = (2.0 * DEPTH) ** 0.25
DEEPNORM_BETA = (8.0 * DEPTH) ** -0.25

kernel_name = "rwkv7_dilated_attn_convglu_deepnorm_hybrid"


def layer_norm(x, g, b):
    xf = x.astype(jnp.float32)
    mu = jnp.mean(xf, axis=-1, keepdims=True)
    var = jnp.mean(jnp.square(xf - mu), axis=-1, keepdims=True)
    return ((xf - mu) * lax.rsqrt(var + LN_EPS) * g + b).astype(x.dtype)


def token_shift(x):
    return jnp.pad(x, ((0, 0), (1, 0), (0, 0)))[:, :-1]


def rwkv7_time_mix(x, mu, w_rkv, w0, w1, w2, a0, a1, a2, g1, g2, k_k, k_a, r_k, gn_g, gn_b, w_out):
    bsz, seq, dm = x.shape
    H, N = RWKV_HEADS, RWKV_HEAD_SIZE
    xx = token_shift(x) - x
    xr, xw, xk, xv, xa, xg = [x + xx * mu[i] for i in range(N_SHIFT_MIX)]
    rkv = jnp.einsum('cbsd,cde->cbse', jnp.stack([xr, xk, xv]), w_rkv)
    r, k, v = rkv[0], rkv[1], rkv[2]
    w = -jax.nn.softplus(-(w0 + jnp.tanh(xw @ w1) @ w2)) - 0.5
    decay = jnp.exp(-jnp.exp(w.astype(jnp.float32)))
    a = jax.nn.sigmoid(a0 + (xa @ a1) @ a2)
    g = jax.nn.sigmoid(xg @ g1) @ g2
    kk = (k * k_k).reshape(bsz, seq, H, N).astype(jnp.float32)
    kk = kk * lax.rsqrt(jnp.maximum(jnp.sum(kk * kk, axis=-1, keepdims=True), 1e-24))
    k = k * (1.0 + (a - 1.0) * k_a)

    def heads(t):
        return t.reshape(bsz, seq, H, N).astype(jnp.float32)

    r_h, k_h, v_h, a_h, w_h = heads(r), heads(k), heads(v), heads(a), heads(decay)

    def time_major(t):
        return jnp.swapaxes(t, 0, 1)

    def step(state, inp):
        r_t, w_t, k_t, v_t, kk_t, b_t = inp
        sa = jnp.einsum('bhvk,bhk->bhv', state, -kk_t)
        state = (state * w_t[:, :, None, :]
                 + sa[..., None] * b_t[:, :, None, :]
                 + v_t[..., None] * k_t[:, :, None, :])
        return state, jnp.einsum('bhvk,bhk->bhv', state, r_t)

    state0 = jnp.zeros((bsz, H, N, N), jnp.float32)
    _, y = lax.scan(step, state0, (time_major(r_h), time_major(w_h), time_major(k_h),
                                   time_major(v_h), time_major(kk), time_major(kk * a_h)))
    y = time_major(y)
    mean = jnp.mean(y, axis=-1, keepdims=True)
    var = jnp.mean(jnp.square(y - mean), axis=-1, keepdims=True)
    y = ((y - mean) * lax.rsqrt(var + GN_EPS)).reshape(bsz, seq, dm) * gn_g + gn_b
    bonus = jnp.sum(r_h * k_h * r_k, axis=-1, keepdims=True) * v_h
    y = y + bonus.reshape(bsz, seq, dm)
    return (y * g).astype(x.dtype) @ w_out


def dilated_group_attention(q, k, v, dilation, span):
    bsz, seq, H, hd = q.shape
    seg = dilation * span
    s_pad = -(-seq // seg) * seg
    nb = s_pad // seg
    pad = ((0, 0), (0, s_pad - seq), (0, 0), (0, 0))

    def to_blocks(t):
        t = jnp.pad(t, pad).reshape(bsz, nb, span, dilation, H, hd)
        return t.transpose(0, 4, 3, 1, 2, 5)

    qb, kb, vb = to_blocks(q), to_blocks(k), to_blocks(v)

    def with_prev(t):
        prev = jnp.pad(t, ((0, 0), (0, 0), (0, 0), (1, 0), (0, 0), (0, 0)))[:, :, :, :-1]
        return jnp.concatenate([prev, t], axis=4)

    kc, vc = with_prev(kb), with_prev(vb)
    scores = jnp.einsum('bhrnqd,bhrnkd->bhrnqk', qb, kc).astype(jnp.float32) * (hd ** -0.5)
    n_idx = jnp.arange(nb)[:, None, None]
    qi = jnp.arange(span)[None, :, None]
    kj = jnp.arange(2 * span)[None, None, :]
    dist = qi + span - kj
    valid = (dist >= 0) & (dist <= span) & ((n_idx > 0) | (kj >= span))
    scores = jnp.where(valid, scores, MASK_VALUE)
    lse = jax.nn.logsumexp(scores, axis=-1)
    p = jnp.exp(scores - lse[..., None]).astype(v.dtype)
    o = jnp.einsum('bhrnqk,bhrnkd->bhrnqd', p, vc)
    o = o.transpose(0, 3, 4, 2, 1, 5).reshape(bsz, s_pad, H, hd)[:, :seq]
    lse = lse.transpose(0, 3, 4, 2, 1).reshape(bsz, s_pad, H)[:, :seq]
    return o, lse


def dilated_attention_mixer(x, w_in, w_out):
    bsz, seq, _ = x.shape
    H, hd = ATTN_HEADS_PER_GROUP, ATTN_HEAD_DIM
    qkv = (x @ w_in).reshape(bsz, seq, N_GROUPS, 3, H, hd)
    outs, lses = [], []
    for gi, (window, dilation) in enumerate(DILATED_GROUPS):
        o, lse = dilated_group_attention(qkv[:, :, gi, 0], qkv[:, :, gi, 1], qkv[:, :, gi, 2],
                                         dilation, window // dilation)
        outs.append(o)
        lses.append(lse)
    weights = jax.nn.softmax(jnp.stack(lses), axis=0).astype(x.dtype)
    o = jnp.einsum('gbsh,gbshd->bshd', weights, jnp.stack(outs))
    return o.reshape(bsz, seq, H * hd) @ w_out


def conv_glu_ffn(x, w_up, conv_w, conv_b, w_down):
    seq = x.shape[1]
    h = x @ w_up
    gate, up = h[..., :D_FF], h[..., D_FF:]
    gp = jnp.pad(gate, ((0, 0), (CONV_WIDTH - 1, 0), (0, 0)))
    acc = conv_b
    for j in range(CONV_WIDTH):
        acc = acc + gp[:, j:j + seq] * conv_w[j]
    return (jax.nn.silu(acc) * up) @ w_down


def setup_inputs(seed: int = 0) -> dict:
    key = jax.random.key(seed)
    ks = jax.random.split(key, 32)
    D, H, N = D_MODEL, RWKV_HEADS, RWKV_HEAD_SIZE
    nr, na = N_RWKV_LAYERS, N_ATTN_LAYERS
    f32 = jnp.float32

    def nrm(k, shape, scale):
        return jax.random.normal(k, shape, f32) * scale

    return {
        'x': nrm(ks[0], (BATCH, SEQ, D), 1.0),
        'rwkv_mu': jax.random.uniform(ks[1], (nr, N_SHIFT_MIX, D), f32),
        'rwkv_w_rkv': nrm(ks[2], (nr, 3, D, D), D ** -0.5),
        'rwkv_w0': jax.random.uniform(ks[3], (nr, D), f32, minval=-6.0, maxval=1.0),
        'rwkv_w1': nrm(ks[4], (nr, D, DECAY_LORA), D ** -0.5),
        'rwkv_w2': nrm(ks[5], (nr, DECAY_LORA, D), 0.5 * DECAY_LORA ** -0.5),
        'rwkv_a0': nrm(ks[6], (nr, D), 0.5),
        'rwkv_a1': nrm(ks[7], (nr, D, AAA_LORA), D ** -0.5),
        'rwkv_a2': nrm(ks[8], (nr, AAA_LORA, D), 0.5 * AAA_LORA ** -0.5),
        'rwkv_g1': nrm(ks[9], (nr, D, GATE_LORA), D ** -0.5),
        'rwkv_g2': nrm(ks[10], (nr, GATE_LORA, D), GATE_LORA ** -0.5),
        'rwkv_k_k': 0.85 + nrm(ks[11], (nr, D), 0.05),
        'rwkv_k_a': 1.0 + nrm(ks[12], (nr, D), 0.05),
        'rwkv_r_k': nrm(ks[13], (nr, H, N), 0.1),
        'rwkv_gn_g': 1.0 + nrm(ks[14], (nr, D), 0.05),
        'rwkv_gn_b': nrm(ks[15], (nr, D), 0.01),
        'rwkv_w_out': nrm(ks[16], (nr, D, D), DEEPNORM_BETA * D ** -0.5),
        'attn_w_in': nrm(ks[17], (na, D, N_GROUPS * 3 * ATTN_GROUP_WIDTH), D ** -0.5),
        'attn_w_out': nrm(ks[18], (na, ATTN_GROUP_WIDTH, D), DEEPNORM_BETA * ATTN_GROUP_WIDTH ** -0.5),
        'ffn_w_up': nrm(ks[19], (DEPTH, D, 2 * D_FF), D ** -0.5),
        'ffn_conv_w': nrm(ks[20], (DEPTH, CONV_WIDTH, D_FF), CONV_WIDTH ** -0.5),
        'ffn_conv_b': nrm(ks[21], (DEPTH, D_FF), 0.01),
        'ffn_w_down': nrm(ks[22], (DEPTH, D_FF, D), DEEPNORM_BETA * D_FF ** -0.5),
        'ln_mix_g': 1.0 + nrm(ks[23], (DEPTH, D), 0.05),
        'ln_mix_b': nrm(ks[24], (DEPTH, D), 0.01),
        'ln_ffn_g': 1.0 + nrm(ks[25], (DEPTH, D), 0.05),
        'ln_ffn_b': nrm(ks[26], (DEPTH, D), 0.01),
    }


def reference(x, rwkv_mu, rwkv_w_rkv, rwkv_w0, rwkv_w1, rwkv_w2, rwkv_a0, rwkv_a1, rwkv_a2,
              rwkv_g1, rwkv_g2, rwkv_k_k, rwkv_k_a, rwkv_r_k, rwkv_gn_g, rwkv_gn_b, rwkv_w_out,
              attn_w_in, attn_w_out, ffn_w_up, ffn_conv_w, ffn_conv_b, ffn_w_down,
              ln_mix_g, ln_mix_b, ln_ffn_g, ln_ffn_b):
    for i in range(DEPTH):
        j = i // N_MIXERS
        if i % N_MIXERS == 0:
            m = rwkv7_time_mix(x, rwkv_mu[j], rwkv_w_rkv[j], rwkv_w0[j], rwkv_w1[j], rwkv_w2[j],
                               rwkv_a0[j], rwkv_a1[j], rwkv_a2[j], rwkv_g1[j], rwkv_g2[j],
                               rwkv_k_k[j], rwkv_k_a[j], rwkv_r_k[j], rwkv_gn_g[j], rwkv_gn_b[j],
                               rwkv_w_out[j])
        else:
            m = dilated_attention_mixer(x, attn_w_in[j], attn_w_out[j])
        x = layer_norm(DEEPNORM_ALPHA * x + m, ln_mix_g[i], ln_mix_b[i])
        f = conv_glu_ffn(x, ffn_w_up[i], ffn_conv_w[i], ffn_conv_b[i], ffn_w_down[i])
        x = layer_norm(DEEPNORM_ALPHA * x + f, ln_ffn_g[i], ln_ffn_b[i])
    return x
```

```python
import functools

import jax
import jax.numpy as jnp
from jax import lax
from jax.experimental import pallas as pl
from jax.experimental.pallas import tpu as pltpu

F32 = jnp.float32
BF16 = jnp.bfloat16

V7X_VMEM_BYTES = 64 * 1024 * 1024
VMEM_LIMIT_BYTES = V7X_VMEM_BYTES - 8 * 1024 * 1024
SUBLANES = 8
LANES = 128

RWKV_HEAD_SIZE = 64
N_SHIFT_MIX = 6
GN_EPS = 64e-5
ATTN_HEAD_DIM = 128
DILATED_GROUPS = ((128, 1), (512, 4), (2048, 16))
MASK_VALUE = -1e30
CONV_WIDTH = 3
LN_EPS = 1e-5
DEPTH = 2
DEEPNORM_ALPHA = (2.0 * DEPTH) ** 0.25
LORA_PAD = 128
LN_CHUNK_ROWS = 32


def _params(*semantics):
    return pltpu.CompilerParams(dimension_semantics=semantics, vmem_limit_bytes=VMEM_LIMIT_BYTES)


def _mix_kernel(x_ref, xp_ref, mu_ref, o_ref):
    s = pl.program_id(1)
    x = x_ref[...]
    before = jnp.where(s > 0, xp_ref[SUBLANES - 1:SUBLANES, :], 0.0)
    prev = pltpu.roll(x, 1, 0)
    row = lax.broadcasted_iota(jnp.int32, x.shape, 0)
    prev = jnp.where(row == 0, before, prev)
    xx = prev - x
    for c in range(N_SHIFT_MIX):
        o_ref[c] = (x + xx * mu_ref[c:c + 1, :]).astype(o_ref.dtype)


def _token_shift_mix(x, mu, ts=256):
    bsz, seq, dm = x.shape
    nst = seq // ts
    return pl.pallas_call(
        _mix_kernel,
        grid=(bsz, nst),
        in_specs=[
            pl.BlockSpec((None, ts, dm), lambda b, s: (b, s, 0)),
            pl.BlockSpec((None, SUBLANES, dm), lambda b, s: (b, jnp.maximum(s * (ts // SUBLANES) - 1, 0), 0)),
            pl.BlockSpec((N_SHIFT_MIX, dm), lambda b, s: (0, 0)),
        ],
        out_specs=pl.BlockSpec((N_SHIFT_MIX, ts, dm), lambda b, s: (0, b * nst + s, 0)),
        out_shape=jax.ShapeDtypeStruct((N_SHIFT_MIX, bsz * seq, dm), BF16),
        compiler_params=_params("parallel", "parallel"),
        name="token_shift_mix",
    )(x, x, mu)


def _mm_kernel(a_ref, w_ref, o_ref):
    o_ref[...] = jnp.dot(a_ref[...], w_ref[...], preferred_element_type=F32).astype(o_ref.dtype)


def _batched_matmul(a, w, n_batch, out_dtype, tm=1024, tn=1024):
    _, m, kdim = a.shape
    n = w.shape[2]
    return pl.pallas_call(
        _mm_kernel,
        grid=(n_batch, n // tn, m // tm),
        in_specs=[
            pl.BlockSpec((None, tm, kdim), lambda c, j, i: (c, i, 0)),
            pl.BlockSpec((None, kdim, tn), lambda c, j, i: (c, 0, j)),
        ],
        out_specs=pl.BlockSpec((None, tm, tn), lambda c, j, i: (c, i, j)),
        out_shape=jax.ShapeDtypeStruct((n_batch, m, n), out_dtype),
        compiler_params=_params("parallel", "parallel", "parallel"),
        name="batched_matmul",
    )(a, w)


def _lora_kernel(xw_ref, xa_ref, xg_ref, w1_ref, w2_ref, w0_ref, a1_ref, a2_ref, a0_ref,
                 g1_ref, g2_ref, decay_ref, a_ref, g_ref):
    def two_stage(x_ref, p_ref, q_ref, act):
        h = act(jnp.dot(x_ref[...], p_ref[...], preferred_element_type=F32))
        return jnp.dot(h.astype(BF16), q_ref[...], preferred_element_type=F32)

    z = w0_ref[...] + two_stage(xw_ref, w1_ref, w2_ref, jnp.tanh)
    decay_ref[...] = jnp.exp(-jnp.exp(-0.5) * jax.nn.sigmoid(z))
    a_ref[...] = jax.nn.sigmoid(a0_ref[...] + two_stage(xa_ref, a1_ref, a2_ref, lambda h: h))
    g_ref[...] = two_stage(xg_ref, g1_ref, g2_ref, jax.nn.sigmoid)


def _rwkv_lora(mixed, w1, w2, w0, a1, a2, a0, g1, g2, tm=256):
    _, t, dm = mixed.shape

    def full(arr):
        return pl.BlockSpec(arr.shape, lambda i: (0,) * arr.ndim)

    def mix_spec(c):
        return pl.BlockSpec((None, tm, dm), lambda i: (c, i, 0))

    out_spec = pl.BlockSpec((tm, dm), lambda i: (i, 0))
    out = jax.ShapeDtypeStruct((t, dm), F32)
    return pl.pallas_call(
        _lora_kernel,
        grid=(t // tm,),
        in_specs=[mix_spec(3), mix_spec(4), mix_spec(5), full(w1), full(w2), full(w0),
                  full(a1), full(a2), full(a0), full(g1), full(g2)],
        out_specs=[out_spec, out_spec, out_spec],
        out_shape=[out, out, out],
        compiler_params=_params("parallel"),
        name="rwkv_lora",
    )(mixed, mixed, mixed, w1, w2, w0, a1, a2, a0, g1, g2)


SCAN_V_GROUPS = 2
SCAN_PARTIALS = 4


def _scan_kernel(r_ref, k_ref, v_ref, w_ref, a_ref, kkp_ref, kap_ref, rkp_ref, gng_ref, gnb_ref,
                 y_ref, st_ref, kk_s, wr_s, b_s, k2_s, *, steps):
    n = RWKV_HEAD_SIZE
    n_vg = n // SUBLANES

    @pl.when(pl.program_id(0) == 0)
    def _():
        st_ref[...] = jnp.zeros_like(st_ref)

    def step(t, carry):
        r = r_ref[t]
        k = k_ref[t]
        v = v_ref[t]
        w = w_ref[t]
        a = a_ref[t]
        kk = k * kkp_ref[...]
        kk = kk * lax.rsqrt(jnp.maximum(jnp.sum(kk * kk, axis=0, keepdims=True), 1e-24))
        k2 = k * (1.0 + (a - 1.0) * kap_ref[...])
        b = kk * a
        kk_s[...] = kk
        wr_s[...] = w * r
        b_s[...] = b
        k2_s[...] = k2
        br = jnp.sum(b * r, axis=0, keepdims=True)
        kr = jnp.sum(k2 * r, axis=0, keepdims=True)
        bonus = jnp.sum(r * k2 * rkp_ref[...], axis=0, keepdims=True)

        y_groups = []
        for g0 in range(0, n_vg, SCAN_V_GROUPS):
            groups = range(g0, g0 + SCAN_V_GROUPS)
            acc_sa = {g: [None] * SCAN_PARTIALS for g in groups}
            acc_y = {g: [None] * SCAN_PARTIALS for g in groups}
            for ki in range(n):
                kk_row = kk_s[ki:ki + 1, :]
                wr_row = wr_s[ki:ki + 1, :]
                p = ki % SCAN_PARTIALS
                for g in groups:
                    s = st_ref[ki, g * SUBLANES:(g + 1) * SUBLANES, :]
                    t_sa = s * kk_row
                    t_y = s * wr_row
                    acc_sa[g][p] = t_sa if acc_sa[g][p] is None else acc_sa[g][p] + t_sa
                    acc_y[g][p] = t_y if acc_y[g][p] is None else acc_y[g][p] + t_y
            sa = {}
            vv = {}
            for g in groups:
                sa[g] = -functools.reduce(lambda x, y: x + y, acc_sa[g])
                vv[g] = v_ref[t, g * SUBLANES:(g + 1) * SUBLANES, :]
                y_groups.append(functools.reduce(lambda x, y: x + y, acc_y[g]) + sa[g] * br + vv[g] * kr)
            for ki in range(n):
                w_row = w_ref[t, ki:ki + 1, :]
                b_row = b_s[ki:ki + 1, :]
                k_row = k2_s[ki:ki + 1, :]
                for g in groups:
                    s = st_ref[ki, g * SUBLANES:(g + 1) * SUBLANES, :]
                    st_ref[ki, g * SUBLANES:(g + 1) * SUBLANES, :] = s * w_row + sa[g] * b_row + vv[g] * k_row

        y = jnp.concatenate(y_groups, axis=0)
        mean = jnp.mean(y, axis=0, keepdims=True)
        yc = y - mean
        var = jnp.mean(yc * yc, axis=0, keepdims=True)
        y_ref[t] = yc * lax.rsqrt(var + GN_EPS) * gng_ref[...] + gnb_ref[...] + bonus * v
        return carry

    lax.fori_loop(0, steps, step, 0)


def _rwkv_scan(r, k, v, w, a, kkp, kap, rkp, gng, gnb, steps=32):
    seq, n, lanes = r.shape
    seq_spec = pl.BlockSpec((steps, n, lanes), lambda i: (i, 0, 0))
    par_spec = pl.BlockSpec((n, lanes), lambda i: (0, 0))
    return pl.pallas_call(
        functools.partial(_scan_kernel, steps=steps),
        grid=(seq // steps,),
        in_specs=[seq_spec] * 5 + [par_spec] * 5,
        out_specs=seq_spec,
        out_shape=jax.ShapeDtypeStruct((seq, n, lanes), F32),
        scratch_shapes=[pltpu.VMEM((n, n, lanes), F32)] + [pltpu.VMEM((n, lanes), F32)] * 4,
        compiler_params=_params("arbitrary"),
        name="rwkv7_scan",
    )(r, k, v, w, a, kkp, kap, rkp, gng, gnb)


def _mm_ln_kernel(*refs, nk, gated, emit_bf16):
    refs = list(refs)
    a_ref = refs.pop(0)
    gate_ref = refs.pop(0) if gated else None
    w_ref, x_ref, g_ref, b_ref, o_ref = refs[:5]
    ob_ref = refs[5] if emit_bf16 else None
    acc_ref = refs[-1]

    a = a_ref[...]
    if gated:
        a = (a * gate_ref[...]).astype(BF16)
    part = jnp.dot(a, w_ref[...], preferred_element_type=F32)

    def finalize():
        def chunk(c, carry):
            rows = pl.ds(pl.multiple_of(c * LN_CHUNK_ROWS, LN_CHUNK_ROWS), LN_CHUNK_ROWS)
            z = DEEPNORM_ALPHA * x_ref[rows, :] + acc_ref[rows, :]
            mu = jnp.mean(z, axis=-1, keepdims=True)
            zc = z - mu
            var = jnp.mean(zc * zc, axis=-1, keepdims=True)
            out = zc * lax.rsqrt(var + LN_EPS) * g_ref[...] + b_ref[...]
            o_ref[rows, :] = out
            if emit_bf16:
                ob_ref[rows, :] = out.astype(BF16)
            return carry

        lax.fori_loop(0, acc_ref.shape[0] // LN_CHUNK_ROWS, chunk, 0)

    if nk == 1:
        acc_ref[...] = part
        finalize()
    else:
        kk = pl.program_id(1)

        @pl.when(kk == 0)
        def _():
            acc_ref[...] = part

        @pl.when(kk > 0)
        def _():
            acc_ref[...] += part

        @pl.when(kk == nk - 1)
        def _():
            finalize()


def _matmul_residual_ln(a, w, x, g, b, *, gate=None, tm, tk, emit_bf16=True):
    m, kdim = a.shape
    n = w.shape[1]
    nk = kdim // tk
    row = lambda i, kk: (i, 0)
    in_specs = [pl.BlockSpec((tm, tk), lambda i, kk: (i, kk))]
    args = [a]
    if gate is not None:
        in_specs.append(pl.BlockSpec((tm, tk), lambda i, kk: (i, kk)))
        args.append(gate)
    in_specs += [pl.BlockSpec((tk, n), lambda i, kk: (kk, 0)),
                 pl.BlockSpec((tm, n), row),
                 pl.BlockSpec((1, n), lambda i, kk: (0, 0)),
                 pl.BlockSpec((1, n), lambda i, kk: (0, 0))]
    args += [w, x, g.reshape(1, n), b.reshape(1, n)]
    out_specs = [pl.BlockSpec((tm, n), row)]
    out_shape = [jax.ShapeDtypeStruct((m, n), F32)]
    if emit_bf16:
        out_specs.append(pl.BlockSpec((tm, n), row))
        out_shape.append(jax.ShapeDtypeStruct((m, n), BF16))
    outs = pl.pallas_call(
        functools.partial(_mm_ln_kernel, nk=nk, gated=gate is not None, emit_bf16=emit_bf16),
        grid=(m // tm, nk),
        in_specs=in_specs,
        out_specs=out_specs,
        out_shape=out_shape,
        scratch_shapes=[pltpu.VMEM((tm, n), F32)],
        compiler_params=_params("parallel", "arbitrary"),
        name="matmul_residual_ln",
    )(*args)
    return outs if emit_bf16 else (outs[0], None)


def _ffn_up_kernel(x_ref, wg_ref, wu_ref, cw_ref, cb_ref, o_ref, gbuf, *, tm, tiles_per_seq):
    i = pl.program_id(1)
    x = x_ref[...]
    gate = jnp.dot(x, wg_ref[...], preferred_element_type=F32)
    up = jnp.dot(x, wu_ref[...], preferred_element_type=F32)

    @pl.when(i % tiles_per_seq == 0)
    def _():
        gbuf[0:SUBLANES, :] = jnp.zeros((SUBLANES, gbuf.shape[1]), F32)

    @pl.when(i % tiles_per_seq != 0)
    def _():
        gbuf[0:SUBLANES, :] = gbuf[tm:tm + SUBLANES, :]

    gbuf[SUBLANES:SUBLANES + tm, :] = gate
    g1 = gbuf[SUBLANES - 1:SUBLANES - 1 + tm, :]
    g2 = gbuf[SUBLANES - 2:SUBLANES - 2 + tm, :]
    acc = cb_ref[...] + g2 * cw_ref[0:1, :] + g1 * cw_ref[1:2, :] + gate * cw_ref[2:3, :]
    o_ref[...] = (acc * jax.nn.sigmoid(acc) * up).astype(o_ref.dtype)


def _ffn_up(xb, w_up, conv_w, conv_b, seq, tm=1024, tn=512):
    t, dm = xb.shape
    dff = w_up.shape[1] // 2
    n_col = dff // tn
    return pl.pallas_call(
        functools.partial(_ffn_up_kernel, tm=tm, tiles_per_seq=seq // tm),
        grid=(n_col, t // tm),
        in_specs=[
            pl.BlockSpec((tm, dm), lambda j, i: (i, 0)),
            pl.BlockSpec((dm, tn), lambda j, i: (0, j)),
            pl.BlockSpec((dm, tn), lambda j, i: (0, j + n_col)),
            pl.BlockSpec((CONV_WIDTH, tn), lambda j, i: (0, j)),
            pl.BlockSpec((1, tn), lambda j, i: (0, j)),
        ],
        out_specs=pl.BlockSpec((tm, tn), lambda j, i: (i, j)),
        out_shape=jax.ShapeDtypeStruct((t, dff), BF16),
        scratch_shapes=[pltpu.VMEM((tm + 2 * SUBLANES, tn), F32)],
        compiler_params=_params("parallel", "arbitrary"),
        name="ffn_up_convglu",
    )(xb, w_up, w_up, conv_w, conv_b.reshape(1, dff))


def _attn_kernel(q_ref, kp_ref, kc_ref, vp_ref, vc_ref, o_ref, l_ref, *, span, heads):
    hd = ATTN_HEAD_DIM
    nblk = pl.program_id(2)
    qi = lax.broadcasted_iota(jnp.int32, (span, 2 * span), 0)
    kj = lax.broadcasted_iota(jnp.int32, (span, 2 * span), 1)
    kj_min = jnp.where(nblk > 0, 0, span)
    valid = (kj <= qi + span) & (kj >= qi) & (kj >= kj_min)
    scale = hd ** -0.5
    for h in range(heads):
        sl = slice(h * hd, (h + 1) * hd)
        q = q_ref[:, sl]
        kcat = jnp.concatenate([kp_ref[:, sl], kc_ref[:, sl]], axis=0)
        vcat = jnp.concatenate([vp_ref[:, sl], vc_ref[:, sl]], axis=0)
        s = lax.dot_general(q, kcat, (((1,), (1,)), ((), ())), preferred_element_type=F32) * scale
        s = jnp.where(valid, s, MASK_VALUE)
        m = jnp.max(s, axis=-1, keepdims=True)
        p = jnp.exp(s - m)
        l = jnp.sum(p, axis=-1, keepdims=True)
        o = jnp.dot(p.astype(BF16), vcat, preferred_element_type=F32) * (1.0 / l)
        o_ref[:, sl] = o.astype(o_ref.dtype)
        l_ref[:, sl] = jnp.broadcast_to(m + jnp.log(l), (span, hd))


def _dilated_group(qkv, group, window, dilation):
    bsz, seq, width = qkv.shape
    span = window // dilation
    gw = width // (3 * len(DILATED_GROUPS))
    heads = gw // ATTN_HEAD_DIM
    sub = seq // dilation
    n_blk = sub // span
    cols_per_row = width // gw
    view = qkv.reshape(bsz, sub, dilation * width)

    def col(which):
        return lambda b, r, nb: r * cols_per_row + group * 3 + which

    def cur(which):
        return pl.BlockSpec((None, span, gw), lambda b, r, nb: (b, nb, col(which)(b, r, nb)))

    def prev(which):
        return pl.BlockSpec((None, span, gw), lambda b, r, nb: (b, jnp.maximum(nb - 1, 0), col(which)(b, r, nb)))

    out_spec = pl.BlockSpec((None, span, gw), lambda b, r, nb: (b, nb, r))
    o, lse = pl.pallas_call(
        functools.partial(_attn_kernel, span=span, heads=heads),
        grid=(bsz, dilation, n_blk),
        in_specs=[cur(0), prev(1), cur(1), prev(2), cur(2)],
        out_specs=[out_spec, out_spec],
        out_shape=[jax.ShapeDtypeStruct((bsz, sub, dilation * gw), BF16),
                   jax.ShapeDtypeStruct((bsz, sub, dilation * gw), F32)],
        compiler_params=_params("parallel", "parallel", "parallel"),
        name="dilated_attention",
    )(view, view, view, view, view)
    return o.reshape(bsz * seq, gw), lse.reshape(bsz * seq, gw)


def _merge_kernel(o0_ref, o1_ref, o2_ref, l0_ref, l1_ref, l2_ref, out_ref):
    l0, l1, l2 = l0_ref[...], l1_ref[...], l2_ref[...]
    m = jnp.maximum(jnp.maximum(l0, l1), l2)
    e0, e1, e2 = jnp.exp(l0 - m), jnp.exp(l1 - m), jnp.exp(l2 - m)
    num = e0 * o0_ref[...].astype(F32) + e1 * o1_ref[...].astype(F32) + e2 * o2_ref[...].astype(F32)
    out_ref[...] = (num / (e0 + e1 + e2)).astype(out_ref.dtype)


def _merge_groups(outs, lses, tm=256):
    t, gw = outs[0].shape
    spec = pl.BlockSpec((tm, gw), lambda i: (i, 0))
    return pl.pallas_call(
        _merge_kernel,
        grid=(t // tm,),
        in_specs=[spec] * 6,
        out_specs=spec,
        out_shape=jax.ShapeDtypeStruct((t, gw), BF16),
        compiler_params=_params("parallel"),
        name="merge_groups",
    )(*outs, *lses)


def _pad_rank(p, q):
    r = p.shape[1]
    if r % LORA_PAD:
        extra = LORA_PAD - r % LORA_PAD
        p = jnp.pad(p, ((0, 0), (0, extra)))
        q = jnp.pad(q, ((0, extra), (0, 0)))
    return p.astype(BF16), q.astype(BF16)


def _rwkv_layer(x, mu, w_rkv, w0, w1, w2, a0, a1, a2, g1, g2, k_k, k_a, r_k, gn_g, gn_b, w_out, ln_g, ln_b):
    bsz, seq, dm = x.shape
    n = RWKV_HEAD_SIZE
    heads = dm // n
    t = bsz * seq
    mixed = _token_shift_mix(x, mu[jnp.array([0, 2, 3, 1, 4, 5])])
    rkv = _batched_matmul(mixed, w_rkv.astype(BF16), 3, F32)
    w1b, w2b = _pad_rank(w1, w2)
    a1b, a2b = _pad_rank(a1, a2)
    g1b, g2b = _pad_rank(g1, g2)
    decay, a, g = _rwkv_lora(mixed, w1b, w2b, w0.reshape(1, dm), a1b, a2b, a0.reshape(1, dm), g1b, g2b)

    def lanes_major(z):
        return z.reshape(bsz, seq, heads, n).transpose(1, 3, 0, 2).reshape(seq, n, bsz * heads)

    def lanes_param(p):
        return jnp.tile(p.reshape(heads, n).T, (1, bsz))

    y = _rwkv_scan(lanes_major(rkv[0]), lanes_major(rkv[1]), lanes_major(rkv[2]), lanes_major(decay),
                   lanes_major(a), lanes_param(k_k), lanes_param(k_a), lanes_param(r_k),
                   lanes_param(gn_g), lanes_param(gn_b))
    y = y.reshape(seq, n, bsz, heads).transpose(2, 0, 3, 1).reshape(t, dm)
    return _matmul_residual_ln(y, w_out.astype(BF16), x.reshape(t, dm), ln_g, ln_b, gate=g, tm=256, tk=dm)


def _attn_layer(x32, xb, w_in, w_out, ln_g, ln_b, bsz, seq):
    t, dm = x32.shape
    qkv = _batched_matmul(xb[None], w_in.astype(BF16)[None], 1, BF16)[0].reshape(bsz, seq, -1)
    outs, lses = [], []
    for gi, (window, dilation) in enumerate(DILATED_GROUPS):
        o, lse = _dilated_group(qkv, gi, window, dilation)
        outs.append(o)
        lses.append(lse)
    merged = _merge_groups(outs, lses)
    return _matmul_residual_ln(merged, w_out.astype(BF16), x32, ln_g, ln_b, tm=512, tk=dm)


def _ffn_layer(x32, xb, w_up, conv_w, conv_b, w_down, ln_g, ln_b, seq, emit_bf16):
    act = _ffn_up(xb, w_up.astype(BF16), conv_w, conv_b, seq)
    return _matmul_residual_ln(act, w_down.astype(BF16), x32, ln_g, ln_b, tm=512, tk=512, emit_bf16=emit_bf16)


def kernel(x, rwkv_mu, rwkv_w_rkv, rwkv_w0, rwkv_w1, rwkv_w2, rwkv_a0, rwkv_a1, rwkv_a2, rwkv_g1, rwkv_g2, rwkv_k_k, rwkv_k_a, rwkv_r_k, rwkv_gn_g, rwkv_gn_b, rwkv_w_out, attn_w_in, attn_w_out, ffn_w_up, ffn_conv_w, ffn_conv_b, ffn_w_down, ln_mix_g, ln_mix_b, ln_ffn_g, ln_ffn_b):
    bsz, seq, dm = x.shape
    x32, xb = _rwkv_layer(x, rwkv_mu[0], rwkv_w_rkv[0], rwkv_w0[0], rwkv_w1[0], rwkv_w2[0], rwkv_a0[0],
                          rwkv_a1[0], rwkv_a2[0], rwkv_g1[0], rwkv_g2[0], rwkv_k_k[0], rwkv_k_a[0],
                          rwkv_r_k[0], rwkv_gn_g[0], rwkv_gn_b[0], rwkv_w_out[0], ln_mix_g[0], ln_mix_b[0])
    x32, xb = _ffn_layer(x32, xb, ffn_w_up[0], ffn_conv_w[0], ffn_conv_b[0], ffn_w_down[0],
                         ln_ffn_g[0], ln_ffn_b[0], seq, True)
    x32, xb = _attn_layer(x32, xb, attn_w_in[0], attn_w_out[0], ln_mix_g[1], ln_mix_b[1], bsz, seq)
    x32, _ = _ffn_layer(x32, xb, ffn_w_up[1], ffn_conv_w[1], ffn_conv_b[1], ffn_w_down[1],
                        ln_ffn_g[1], ln_ffn_b[1], seq, False)
    return x32.reshape(bsz, seq, dm)
```

```python
import functools

import jax
import jax.numpy as jnp
from jax import lax
from jax.experimental import pallas as pl
from jax.experimental.pallas import tpu as pltpu

F32 = jnp.float32
BF16 = jnp.bfloat16

V7X_VMEM_BYTES = 64 * 1024 * 1024
VMEM_LIMIT_BYTES = V7X_VMEM_BYTES - 8 * 1024 * 1024
SUBLANES = 8
LANES = 128

RWKV_HEAD_SIZE = 64
N_SHIFT_MIX = 6
GN_EPS = 64e-5
ATTN_HEAD_DIM = 128
DILATED_GROUPS = ((128, 1), (512, 4), (2048, 16))
MASK_VALUE = -1e30
CONV_WIDTH = 3
LN_EPS = 1e-5
DEPTH = 2
DEEPNORM_ALPHA = (2.0 * DEPTH) ** 0.25
LORA_PAD = 128
LN_CHUNK_ROWS = 32


def _params(*semantics):
    return pltpu.CompilerParams(dimension_semantics=semantics, vmem_limit_bytes=VMEM_LIMIT_BYTES)


def _mix_kernel(x_ref, xp_ref, mu_ref, o_ref):
    s = pl.program_id(1)
    x = x_ref[...]
    before = jnp.where(s > 0, xp_ref[SUBLANES - 1:SUBLANES, :], 0.0)
    prev = pltpu.roll(x, 1, 0)
    row = lax.broadcasted_iota(jnp.int32, x.shape, 0)
    prev = jnp.where(row == 0, before, prev)
    xx = prev - x
    for c in range(N_SHIFT_MIX):
        o_ref[c] = (x + xx * mu_ref[c:c + 1, :]).astype(o_ref.dtype)


def _token_shift_mix(x, mu, ts=256):
    bsz, seq, dm = x.shape
    nst = seq // ts
    return pl.pallas_call(
        _mix_kernel,
        grid=(bsz, nst),
        in_specs=[
            pl.BlockSpec((None, ts, dm), lambda b, s: (b, s, 0)),
            pl.BlockSpec((None, SUBLANES, dm), lambda b, s: (b, jnp.maximum(s * (ts // SUBLANES) - 1, 0), 0)),
            pl.BlockSpec((N_SHIFT_MIX, dm), lambda b, s: (0, 0)),
        ],
        out_specs=pl.BlockSpec((N_SHIFT_MIX, ts, dm), lambda b, s: (0, b * nst + s, 0)),
        out_shape=jax.ShapeDtypeStruct((N_SHIFT_MIX, bsz * seq, dm), BF16),
        compiler_params=_params("parallel", "parallel"),
        name="token_shift_mix",
    )(x, x, mu)


def _mm_kernel(a_ref, w_ref, o_ref):
    o_ref[...] = jnp.dot(a_ref[...], w_ref[...], preferred_element_type=F32).astype(o_ref.dtype)


def _batched_matmul(a, w, n_batch, out_dtype, tm=1024, tn=1024):
    _, m, kdim = a.shape
    n = w.shape[2]
    return pl.pallas_call(
        _mm_kernel,
        grid=(n_batch, n // tn, m // tm),
        in_specs=[
            pl.BlockSpec((None, tm, kdim), lambda c, j, i: (c, i, 0)),
            pl.BlockSpec((None, kdim, tn), lambda c, j, i: (c, 0, j)),
        ],
        out_specs=pl.BlockSpec((None, tm, tn), lambda c, j, i: (c, i, j)),
        out_shape=jax.ShapeDtypeStruct((n_batch, m, n), out_dtype),
        compiler_params=_params("parallel", "parallel", "parallel"),
        name="batched_matmul",
    )(a, w)


def _lora_kernel(xw_ref, xa_ref, xg_ref, w1_ref, w2_ref, w0_ref, a1_ref, a2_ref, a0_ref,
                 g1_ref, g2_ref, decay_ref, a_ref, g_ref):
    def two_stage(x_ref, p_ref, q_ref, act):
        h = act(jnp.dot(x_ref[...], p_ref[...], preferred_element_type=F32))
        return jnp.dot(h.astype(BF16), q_ref[...], preferred_element_type=F32)

    z = w0_ref[...] + two_stage(xw_ref, w1_ref, w2_ref, jnp.tanh)
    decay_ref[...] = jnp.exp(-jnp.exp(-0.5) * jax.nn.sigmoid(z))
    a_ref[...] = jax.nn.sigmoid(a0_ref[...] + two_stage(xa_ref, a1_ref, a2_ref, lambda h: h))
    g_ref[...] = two_stage(xg_ref, g1_ref, g2_ref, jax.nn.sigmoid)


def _rwkv_lora(mixed, w1, w2, w0, a1, a2, a0, g1, g2, tm=256):
    _, t, dm = mixed.shape

    def full(arr):
        return pl.BlockSpec(arr.shape, lambda i: (0,) * arr.ndim)

    def mix_spec(c):
        return pl.BlockSpec((None, tm, dm), lambda i: (c, i, 0))

    out_spec = pl.BlockSpec((tm, dm), lambda i: (i, 0))
    out = jax.ShapeDtypeStruct((t, dm), F32)
    return pl.pallas_call(
        _lora_kernel,
        grid=(t // tm,),
        in_specs=[mix_spec(3), mix_spec(4), mix_spec(5), full(w1), full(w2), full(w0),
                  full(a1), full(a2), full(a0), full(g1), full(g2)],
        out_specs=[out_spec, out_spec, out_spec],
        out_shape=[out, out, out],
        compiler_params=_params("parallel"),
        name="rwkv_lora",
    )(mixed, mixed, mixed, w1, w2, w0, a1, a2, a0, g1, g2)


SCAN_V_GROUPS = 2
SCAN_PARTIALS = 4


def _scan_kernel(r_ref, k_ref, v_ref, w_ref, a_ref, kkp_ref, kap_ref, rkp_ref, gng_ref, gnb_ref,
                 y_ref, st_ref, kk_s, wr_s, b_s, k2_s, *, steps):
    n = RWKV_HEAD_SIZE
    n_vg = n // SUBLANES

    @pl.when(pl.program_id(0) == 0)
    def _():
        st_ref[...] = jnp.zeros_like(st_ref)

    def step(t, carry):
        r = r_ref[t]
        k = k_ref[t]
        v = v_ref[t]
        w = w_ref[t]
        a = a_ref[t]
        kk = k * kkp_ref[...]
        kk = kk * lax.rsqrt(jnp.maximum(jnp.sum(kk * kk, axis=0, keepdims=True), 1e-24))
        k2 = k * (1.0 + (a - 1.0) * kap_ref[...])
        b = kk * a
        kk_s[...] = kk
        wr_s[...] = w * r
        b_s[...] = b
        k2_s[...] = k2
        br = jnp.sum(b * r, axis=0, keepdims=True)
        kr = jnp.sum(k2 * r, axis=0, keepdims=True)
        bonus = jnp.sum(r * k2 * rkp_ref[...], axis=0, keepdims=True)

        y_groups = []
        for g0 in range(0, n_vg, SCAN_V_GROUPS):
            groups = range(g0, g0 + SCAN_V_GROUPS)
            acc_sa = {g: [None] * SCAN_PARTIALS for g in groups}
            acc_y = {g: [None] * SCAN_PARTIALS for g in groups}
            for ki in range(n):
                kk_row = kk_s[ki:ki + 1, :]
                wr_row = wr_s[ki:ki + 1, :]
                p = ki % SCAN_PARTIALS
                for g in groups:
                    s = st_ref[ki, g * SUBLANES:(g + 1) * SUBLANES, :]
                    t_sa = s * kk_row
                    t_y = s * wr_row
                    acc_sa[g][p] = t_sa if acc_sa[g][p] is None else acc_sa[g][p] + t_sa
                    acc_y[g][p] = t_y if acc_y[g][p] is None else acc_y[g][p] + t_y
            sa = {}
            vv = {}
            for g in groups:
                sa[g] = -functools.reduce(lambda x, y: x + y, acc_sa[g])
                vv[g] = v_ref[t, g * SUBLANES:(g + 1) * SUBLANES, :]
                y_groups.append(functools.reduce(lambda x, y: x + y, acc_y[g]) + sa[g] * br + vv[g] * kr)
            for ki in range(n):
                w_row = w_ref[t, ki:ki + 1, :]
                b_row = b_s[ki:ki + 1, :]
                k_row = k2_s[ki:ki + 1, :]
                for g in groups:
                    s = st_ref[ki, g * SUBLANES:(g + 1) * SUBLANES, :]
                    st_ref[ki, g * SUBLANES:(g + 1) * SUBLANES, :] = s * w_row + sa[g] * b_row + vv[g] * k_row

        y = jnp.concatenate(y_groups, axis=0)
        mean = jnp.mean(y, axis=0, keepdims=True)
        yc = y - mean
        var = jnp.mean(yc * yc, axis=0, keepdims=True)
        y_ref[t] = yc * lax.rsqrt(var + GN_EPS) * gng_ref[...] + gnb_ref[...] + bonus * v
        return carry

    lax.fori_loop(0, steps, step, 0)


def _rwkv_scan(r, k, v, w, a, kkp, kap, rkp, gng, gnb, steps=32):
    seq, n, lanes = r.shape
    seq_spec = pl.BlockSpec((steps, n, lanes), lambda i: (i, 0, 0))
    par_spec = pl.BlockSpec((n, lanes), lambda i: (0, 0))
    return pl.pallas_call(
        functools.partial(_scan_kernel, steps=steps),
        grid=(seq // steps,),
        in_specs=[seq_spec] * 5 + [par_spec] * 5,
        out_specs=seq_spec,
        out_shape=jax.ShapeDtypeStruct((seq, n, lanes), F32),
        scratch_shapes=[pltpu.VMEM((n, n, lanes), F32)] + [pltpu.VMEM((n, lanes), F32)] * 4,
        compiler_params=_params("arbitrary"),
        name="rwkv7_scan",
    )(r, k, v, w, a, kkp, kap, rkp, gng, gnb)


def _mm_ln_kernel(*refs, nk, gated, emit_bf16):
    refs = list(refs)
    a_ref = refs.pop(0)
    gate_ref = refs.pop(0) if gated else None
    w_ref, x_ref, g_ref, b_ref, o_ref = refs[:5]
    ob_ref = refs[5] if emit_bf16 else None
    acc_ref = refs[-1]

    a = a_ref[...]
    if gated:
        a = (a * gate_ref[...]).astype(BF16)
    part = jnp.dot(a, w_ref[...], preferred_element_type=F32)

    def finalize():
        def chunk(c, carry):
            rows = pl.ds(pl.multiple_of(c * LN_CHUNK_ROWS, LN_CHUNK_ROWS), LN_CHUNK_ROWS)
            z = DEEPNORM_ALPHA * x_ref[rows, :] + acc_ref[rows, :]
            mu = jnp.mean(z, axis=-1, keepdims=True)
            zc = z - mu
            var = jnp.mean(zc * zc, axis=-1, keepdims=True)
            out = zc * lax.rsqrt(var + LN_EPS) * g_ref[...] + b_ref[...]
            o_ref[rows, :] = out
            if emit_bf16:
                ob_ref[rows, :] = out.astype(BF16)
            return carry

        lax.fori_loop(0, acc_ref.shape[0] // LN_CHUNK_ROWS, chunk, 0)

    if nk == 1:
        acc_ref[...] = part
        finalize()
    else:
        kk = pl.program_id(1)

        @pl.when(kk == 0)
        def _():
            acc_ref[...] = part

        @pl.when(kk > 0)
        def _():
            acc_ref[...] += part

        @pl.when(kk == nk - 1)
        def _():
            finalize()


def _matmul_residual_ln(a, w, x, g, b, *, gate=None, tm, tk, emit_bf16=True):
    m, kdim = a.shape
    n = w.shape[1]
    nk = kdim // tk
    row = lambda i, kk: (i, 0)
    in_specs = [pl.BlockSpec((tm, tk), lambda i, kk: (i, kk))]
    args = [a]
    if gate is not None:
        in_specs.append(pl.BlockSpec((tm, tk), lambda i, kk: (i, kk)))
        args.append(gate)
    in_specs += [pl.BlockSpec((tk, n), lambda i, kk: (kk, 0)),
                 pl.BlockSpec((tm, n), row),
                 pl.BlockSpec((1, n), lambda i, kk: (0, 0)),
                 pl.BlockSpec((1, n), lambda i, kk: (0, 0))]
    args += [w, x, g.reshape(1, n), b.reshape(1, n)]
    out_specs = [pl.BlockSpec((tm, n), row)]
    out_shape = [jax.ShapeDtypeStruct((m, n), F32)]
    if emit_bf16:
        out_specs.append(pl.BlockSpec((tm, n), row))
        out_shape.append(jax.ShapeDtypeStruct((m, n), BF16))
    outs = pl.pallas_call(
        functools.partial(_mm_ln_kernel, nk=nk, gated=gate is not None, emit_bf16=emit_bf16),
        grid=(m // tm, nk),
        in_specs=in_specs,
        out_specs=out_specs,
        out_shape=out_shape,
        scratch_shapes=[pltpu.VMEM((tm, n), F32)],
        compiler_params=_params("parallel", "arbitrary"),
        name="matmul_residual_ln",
    )(*args)
    return outs if emit_bf16 else (outs[0], None)


def _ffn_up_kernel(x_ref, wg_ref, wu_ref, cw_ref, cb_ref, o_ref, gbuf, *, tm, tiles_per_seq):
    i = pl.program_id(1)
    x = x_ref[...]
    gate = jnp.dot(x, wg_ref[...], preferred_element_type=F32)
    up = jnp.dot(x, wu_ref[...], preferred_element_type=F32)

    @pl.when(i % tiles_per_seq == 0)
    def _():
        gbuf[0:SUBLANES, :] = jnp.zeros((SUBLANES, gbuf.shape[1]), F32)

    @pl.when(i % tiles_per_seq != 0)
    def _():
        gbuf[0:SUBLANES, :] = gbuf[tm:tm + SUBLANES, :]

    gbuf[SUBLANES:SUBLANES + tm, :] = gate
    g1 = gbuf[SUBLANES - 1:SUBLANES - 1 + tm, :]
    g2 = gbuf[SUBLANES - 2:SUBLANES - 2 + tm, :]
    acc = cb_ref[...] + g2 * cw_ref[0:1, :] + g1 * cw_ref[1:2, :] + gate * cw_ref[2:3, :]
    o_ref[...] = (acc * jax.nn.sigmoid(acc) * up).astype(o_ref.dtype)


def _ffn_up(xb, w_up, conv_w, conv_b, seq, tm=1024, tn=512):
    t, dm = xb.shape
    dff = w_up.shape[1] // 2
    n_col = dff // tn
    return pl.pallas_call(
        functools.partial(_ffn_up_kernel, tm=tm, tiles_per_seq=seq // tm),
        grid=(n_col, t // tm),
        in_specs=[
            pl.BlockSpec((tm, dm), lambda j, i: (i, 0)),
            pl.BlockSpec((dm, tn), lambda j, i: (0, j)),
            pl.BlockSpec((dm, tn), lambda j, i: (0, j + n_col)),
            pl.BlockSpec((CONV_WIDTH, tn), lambda j, i: (0, j)),
            pl.BlockSpec((1, tn), lambda j, i: (0, j)),
        ],
        out_specs=pl.BlockSpec((tm, tn), lambda j, i: (i, j)),
        out_shape=jax.ShapeDtypeStruct((t, dff), BF16),
        scratch_shapes=[pltpu.VMEM((tm + 2 * SUBLANES, tn), F32)],
        compiler_params=_params("parallel", "arbitrary"),
        name="ffn_up_convglu",
    )(xb, w_up, w_up, conv_w, conv_b.reshape(1, dff))


ATTN_SPAN = 128
ATTN_TILE = 2048
ATTN_MERGE_ROWS = 64


def _block_attention(q, k, v, valid):
    s = lax.dot_general(q.astype(BF16), k.astype(BF16), (((1,), (1,)), ((), ())),
                        preferred_element_type=F32) * (ATTN_HEAD_DIM ** -0.5)
    s = jnp.where(valid, s, MASK_VALUE)
    m = jnp.max(s, axis=-1, keepdims=True)
    p = jnp.exp(s - m)
    l = jnp.sum(p, axis=-1, keepdims=True)
    o = jnp.dot(p.astype(BF16), v.astype(BF16), preferred_element_type=F32) * (1.0 / l)
    return o, m + jnp.log(l)


def _attn_kernel(q1, k1, v1, kp1, vp1, q4, k4, v4, kp4, vp4, q16, k16, v16, kp16, vp16,
                 o_ref, og, lg, *, tiles_per_seq):
    span, hd = ATTN_SPAN, ATTN_HEAD_DIM
    first_tile = pl.program_id(1) % tiles_per_seq == 0
    qi = lax.broadcasted_iota(jnp.int32, (span, 2 * span), 0)
    kj = lax.broadcasted_iota(jnp.int32, (span, 2 * span), 1)
    valid = (kj <= qi + span) & (kj >= qi)
    valid_edge = valid & (kj >= jnp.where(first_tile, span, 0))

    def emit(g, rows, q, k, v, mask):
        o, lse = _block_attention(q, k, v, mask)
        og[g, rows, :] = o
        lg[g, rows, :] = jnp.broadcast_to(lse, (span, hd))

    def cat(a, b):
        return jnp.concatenate([a, b], axis=0)

    n1 = ATTN_TILE // span
    rows = pl.ds(0, span)
    emit(0, rows, q1[rows, :], cat(kp1[...], k1[rows, :]), cat(vp1[...], v1[rows, :]), valid_edge)

    def body1(i, c):
        for u in range(5):
            j = 1 + 5 * i + u
            rows = pl.ds(pl.multiple_of(j * span, span), span)
            keys = pl.ds(pl.multiple_of((j - 1) * span, span), 2 * span)
            emit(0, rows, q1[rows, :], k1[keys, :], v1[keys, :], valid)
        return c

    lax.fori_loop(0, (n1 - 1) // 5, body1, 0)

    d4 = 4
    nb4 = ATTN_TILE // (span * d4)

    def body4(i, c):
        for u in range(2):
            r = 2 * i + u
            rows = pl.ds(r, span, stride=d4)
            emit(1, rows, q4[rows, :], cat(kp4[rows, :], k4[rows, :]), cat(vp4[rows, :], v4[rows, :]),
                 valid_edge)
            for j in range(1, nb4):
                rows = pl.ds(j * span * d4 + r, span, stride=d4)
                keys = pl.ds((j - 1) * span * d4 + r, 2 * span, stride=d4)
                emit(1, rows, q4[rows, :], k4[keys, :], v4[keys, :], valid)
        return c

    lax.fori_loop(0, d4 // 2, body4, 0)

    d16 = 16

    def body16(i, c):
        for u in range(8):
            rows = pl.ds(8 * i + u, span, stride=d16)
            emit(2, rows, q16[rows, :], cat(kp16[rows, :], k16[rows, :]), cat(vp16[rows, :], v16[rows, :]),
                 valid_edge)
        return c

    lax.fori_loop(0, d16 // 8, body16, 0)

    def merge(c, carry):
        rows = pl.ds(pl.multiple_of(c * ATTN_MERGE_ROWS, ATTN_MERGE_ROWS), ATTN_MERGE_ROWS)
        l0, l1, l2 = lg[0, rows, :], lg[1, rows, :], lg[2, rows, :]
        m = jnp.maximum(jnp.maximum(l0, l1), l2)
        e0, e1, e2 = jnp.exp(l0 - m), jnp.exp(l1 - m), jnp.exp(l2 - m)
        num = e0 * og[0, rows, :] + e1 * og[1, rows, :] + e2 * og[2, rows, :]
        o_ref[rows, :] = (num / (e0 + e1 + e2)).astype(o_ref.dtype)
        return carry

    lax.fori_loop(0, ATTN_TILE // ATTN_MERGE_ROWS, merge, 0)


def _dilated_attention(qkv, seq):
    t, width = qkv.shape
    hd, span, tile = ATTN_HEAD_DIM, ATTN_SPAN, ATTN_TILE
    heads = width // (3 * len(DILATED_GROUPS) * hd)
    tiles_per_seq = seq // tile

    def col(group, which):
        return lambda h: (group * 3 + which) * heads + h

    def cur(group, which):
        return pl.BlockSpec((tile, hd), lambda h, i: (i, col(group, which)(h)))

    def prev(group, which, rows):
        per = tile // rows
        return pl.BlockSpec((rows, hd), lambda h, i: (jnp.maximum(i * per - 1, 0), col(group, which)(h)))

    in_specs = []
    for group, (window, dilation) in enumerate(DILATED_GROUPS):
        assert window // dilation == span and tile % window == 0
        in_specs += [cur(group, 0), cur(group, 1), cur(group, 2), prev(group, 1, window), prev(group, 2, window)]
    return pl.pallas_call(
        functools.partial(_attn_kernel, tiles_per_seq=tiles_per_seq),
        grid=(heads, t // tile),
        in_specs=in_specs,
        out_specs=pl.BlockSpec((tile, hd), lambda h, i: (i, h)),
        out_shape=jax.ShapeDtypeStruct((t, heads * hd), BF16),
        scratch_shapes=[pltpu.VMEM((len(DILATED_GROUPS), tile, hd), F32)] * 2,
        compiler_params=_params("parallel", "parallel"),
        name="dilated_attention",
    )(*([qkv] * len(in_specs)))


def _pad_rank(p, q):
    r = p.shape[1]
    if r % LORA_PAD:
        extra = LORA_PAD - r % LORA_PAD
        p = jnp.pad(p, ((0, 0), (0, extra)))
        q = jnp.pad(q, ((0, extra), (0, 0)))
    return p.astype(BF16), q.astype(BF16)


def _rwkv_layer(x, mu, w_rkv, w0, w1, w2, a0, a1, a2, g1, g2, k_k, k_a, r_k, gn_g, gn_b, w_out, ln_g, ln_b):
    bsz, seq, dm = x.shape
    n = RWKV_HEAD_SIZE
    heads = dm // n
    t = bsz * seq
    mixed = _token_shift_mix(x, mu[jnp.array([0, 2, 3, 1, 4, 5])])
    rkv = _batched_matmul(mixed, w_rkv.astype(BF16), 3, F32)
    w1b, w2b = _pad_rank(w1, w2)
    a1b, a2b = _pad_rank(a1, a2)
    g1b, g2b = _pad_rank(g1, g2)
    decay, a, g = _rwkv_lora(mixed, w1b, w2b, w0.reshape(1, dm), a1b, a2b, a0.reshape(1, dm), g1b, g2b)

    def lanes_major(z):
        return z.reshape(bsz, seq, heads, n).transpose(1, 3, 0, 2).reshape(seq, n, bsz * heads)

    def lanes_param(p):
        return jnp.tile(p.reshape(heads, n).T, (1, bsz))

    y = _rwkv_scan(lanes_major(rkv[0]), lanes_major(rkv[1]), lanes_major(rkv[2]), lanes_major(decay),
                   lanes_major(a), lanes_param(k_k), lanes_param(k_a), lanes_param(r_k),
                   lanes_param(gn_g), lanes_param(gn_b))
    y = y.reshape(seq, n, bsz, heads).transpose(2, 0, 3, 1).reshape(t, dm)
    return _matmul_residual_ln(y, w_out.astype(BF16), x.reshape(t, dm), ln_g, ln_b, gate=g, tm=256, tk=dm)


def _attn_layer(x32, xb, w_in, w_out, ln_g, ln_b, bsz, seq):
    t, dm = x32.shape
    qkv = _batched_matmul(xb[None], w_in.astype(BF16)[None], 1, F32)[0]
    merged = _dilated_attention(qkv, seq)
    return _matmul_residual_ln(merged, w_out.astype(BF16), x32, ln_g, ln_b, tm=512, tk=dm)


def _ffn_layer(x32, xb, w_up, conv_w, conv_b, w_down, ln_g, ln_b, seq, emit_bf16):
    act = _ffn_up(xb, w_up.astype(BF16), conv_w, conv_b, seq)
    return _matmul_residual_ln(act, w_down.astype(BF16), x32, ln_g, ln_b, tm=512, tk=512, emit_bf16=emit_bf16)


def kernel(x, rwkv_mu, rwkv_w_rkv, rwkv_w0, rwkv_w1, rwkv_w2, rwkv_a0, rwkv_a1, rwkv_a2, rwkv_g1, rwkv_g2, rwkv_k_k, rwkv_k_a, rwkv_r_k, rwkv_gn_g, rwkv_gn_b, rwkv_w_out, attn_w_in, attn_w_out, ffn_w_up, ffn_conv_w, ffn_conv_b, ffn_w_down, ln_mix_g, ln_mix_b, ln_ffn_g, ln_ffn_b):
    bsz, seq, dm = x.shape
    x32, xb = _rwkv_layer(x, rwkv_mu[0], rwkv_w_rkv[0], rwkv_w0[0], rwkv_w1[0], rwkv_w2[0], rwkv_a0[0],
                          rwkv_a1[0], rwkv_a2[0], rwkv_g1[0], rwkv_g2[0], rwkv_k_k[0], rwkv_k_a[0],
                          rwkv_r_k[0], rwkv_gn_g[0], rwkv_gn_b[0], rwkv_w_out[0], ln_mix_g[0], ln_mix_b[0])
    x32, xb = _ffn_layer(x32, xb, ffn_w_up[0], ffn_conv_w[0], ffn_conv_b[0], ffn_w_down[0],
                         ln_ffn_g[0], ln_ffn_b[0], seq, True)
    x32, xb = _attn_layer(x32, xb, attn_w_in[0], attn_w_out[0], ln_mix_g[1], ln_mix_b[1], bsz, seq)
    x32, _ = _ffn_layer(x32, xb, ffn_w_up[1], ffn_conv_w[1], ffn_conv_b[1], ffn_w_down[1],
                        ln_ffn_g[1], ln_ffn_b[1], seq, False)
    return x32.reshape(bsz, seq, dm)
```

```python
import functools

import jax
import jax.numpy as jnp
from jax import lax
from jax.experimental import pallas as pl
from jax.experimental.pallas import tpu as pltpu

F32 = jnp.float32
BF16 = jnp.bfloat16

V7X_VMEM_BYTES = 64 * 1024 * 1024
VMEM_LIMIT_BYTES = V7X_VMEM_BYTES - 8 * 1024 * 1024
SUBLANES = 8
LANES = 128

RWKV_HEAD_SIZE = 64
N_SHIFT_MIX = 6
GN_EPS = 64e-5
ATTN_HEAD_DIM = 128
DILATED_GROUPS = ((128, 1), (512, 4), (2048, 16))
MASK_VALUE = -1e30
CONV_WIDTH = 3
LN_EPS = 1e-5
DEPTH = 2
DEEPNORM_ALPHA = (2.0 * DEPTH) ** 0.25
LORA_PAD = 128
LN_CHUNK_ROWS = 32


def _params(*semantics):
    return pltpu.CompilerParams(dimension_semantics=semantics, vmem_limit_bytes=VMEM_LIMIT_BYTES)


def _mix_kernel(x_ref, xp_ref, mu_ref, o_ref):
    s = pl.program_id(1)
    x = x_ref[...]
    before = jnp.where(s > 0, xp_ref[SUBLANES - 1:SUBLANES, :], 0.0)
    prev = pltpu.roll(x, 1, 0)
    row = lax.broadcasted_iota(jnp.int32, x.shape, 0)
    prev = jnp.where(row == 0, before, prev)
    xx = prev - x
    for c in range(N_SHIFT_MIX):
        o_ref[c] = (x + xx * mu_ref[c:c + 1, :]).astype(o_ref.dtype)


def _token_shift_mix(x, mu, ts=256):
    bsz, seq, dm = x.shape
    ts = min(ts, seq)
    nst = seq // ts
    return pl.pallas_call(
        _mix_kernel,
        grid=(bsz, nst),
        in_specs=[
            pl.BlockSpec((None, ts, dm), lambda b, s: (b, s, 0)),
            pl.BlockSpec((None, SUBLANES, dm), lambda b, s: (b, jnp.maximum(s * (ts // SUBLANES) - 1, 0), 0)),
            pl.BlockSpec((N_SHIFT_MIX, dm), lambda b, s: (0, 0)),
        ],
        out_specs=pl.BlockSpec((N_SHIFT_MIX, ts, dm), lambda b, s: (0, b * nst + s, 0)),
        out_shape=jax.ShapeDtypeStruct((N_SHIFT_MIX, bsz * seq, dm), BF16),
        compiler_params=_params("parallel", "parallel"),
        name="token_shift_mix",
    )(x, x, mu)


def _mm_kernel(a_ref, w_ref, o_ref):
    o_ref[...] = jnp.dot(a_ref[...], w_ref[...], preferred_element_type=F32).astype(o_ref.dtype)


def _batched_matmul(a, w, n_batch, out_dtype, tm=1024, tn=1024):
    _, m, kdim = a.shape
    tm = min(tm, m)
    n = w.shape[2]
    return pl.pallas_call(
        _mm_kernel,
        grid=(n_batch, n // tn, m // tm),
        in_specs=[
            pl.BlockSpec((None, tm, kdim), lambda c, j, i: (c, i, 0)),
            pl.BlockSpec((None, kdim, tn), lambda c, j, i: (c, 0, j)),
        ],
        out_specs=pl.BlockSpec((None, tm, tn), lambda c, j, i: (c, i, j)),
        out_shape=jax.ShapeDtypeStruct((n_batch, m, n), out_dtype),
        compiler_params=_params("parallel", "parallel", "parallel"),
        name="batched_matmul",
    )(a, w)


def _lora_kernel(xw_ref, xa_ref, xg_ref, w1_ref, w2_ref, w0_ref, a1_ref, a2_ref, a0_ref,
                 g1_ref, g2_ref, decay_ref, a_ref, g_ref):
    def two_stage(x_ref, p_ref, q_ref, act):
        h = act(jnp.dot(x_ref[...], p_ref[...], preferred_element_type=F32))
        return jnp.dot(h.astype(BF16), q_ref[...], preferred_element_type=F32)

    z = w0_ref[...] + two_stage(xw_ref, w1_ref, w2_ref, jnp.tanh)
    decay_ref[...] = jnp.exp(-jnp.exp(-0.5) * jax.nn.sigmoid(z))
    a_ref[...] = jax.nn.sigmoid(a0_ref[...] + two_stage(xa_ref, a1_ref, a2_ref, lambda h: h))
    g_ref[...] = two_stage(xg_ref, g1_ref, g2_ref, jax.nn.sigmoid)


def _rwkv_lora(mixed, w1, w2, w0, a1, a2, a0, g1, g2, tm=256):
    _, t, dm = mixed.shape

    def full(arr):
        return pl.BlockSpec(arr.shape, lambda i: (0,) * arr.ndim)

    def mix_spec(c):
        return pl.BlockSpec((None, tm, dm), lambda i: (c, i, 0))

    out_spec = pl.BlockSpec((tm, dm), lambda i: (i, 0))
    out = jax.ShapeDtypeStruct((t, dm), F32)
    return pl.pallas_call(
        _lora_kernel,
        grid=(t // tm,),
        in_specs=[mix_spec(3), mix_spec(4), mix_spec(5), full(w1), full(w2), full(w0),
                  full(a1), full(a2), full(a0), full(g1), full(g2)],
        out_specs=[out_spec, out_spec, out_spec],
        out_shape=[out, out, out],
        compiler_params=_params("parallel"),
        name="rwkv_lora",
    )(mixed, mixed, mixed, w1, w2, w0, a1, a2, a0, g1, g2)


SCAN_V_GROUPS = 2
SCAN_PARTIALS = 4


SCAN_BATCH = 4
SCAN_CHUNK = LANES // SCAN_BATCH


def _swap_halves(a0, a1, a2, a3, low_half):
    r0, r1, r2, r3 = (pltpu.roll(a, 2 * SCAN_CHUNK, 1) for a in (a0, a1, a2, a3))
    return (jnp.where(low_half, a0, r2), jnp.where(low_half, a1, r3),
            jnp.where(low_half, r0, a2), jnp.where(low_half, r1, a3))


def _swap_chunks(p0, p1, p2, p3, even_chunk):
    return (jnp.where(even_chunk, p0, pltpu.roll(p1, SCAN_CHUNK, 1)),
            jnp.where(even_chunk, pltpu.roll(p0, 3 * SCAN_CHUNK, 1), p1),
            jnp.where(even_chunk, p2, pltpu.roll(p3, SCAN_CHUNK, 1)),
            jnp.where(even_chunk, pltpu.roll(p2, 3 * SCAN_CHUNK, 1), p3))


def _scan_kernel(r_ref, k_ref, v_ref, w_ref, a_ref, kkp_ref, kap_ref, rkp_ref, gng_ref, gnb_ref,
                 y_ref, st_ref, kk_s, wr_s, w_s, b_s, k2_s, v_s, y_s, row_s, *, steps):
    n = RWKV_HEAD_SIZE
    n_vg = n // SUBLANES
    lane = lax.broadcasted_iota(jnp.int32, (SUBLANES, LANES), 1)
    low_half = lane < 2 * SCAN_CHUNK
    even_chunk = (lane & SCAN_CHUNK) == 0

    @pl.when(pl.program_id(0) == 0)
    def _():
        st_ref[...] = jnp.zeros_like(st_ref)

    def tile(i):
        return slice(i * SUBLANES, (i + 1) * SUBLANES)

    def eight_steps(s8, carry):
        s0 = pl.multiple_of(s8 * SUBLANES, SUBLANES)
        ss = br = kr = bonus = None
        pairs = ((r_ref, wr_s), (k_ref, kk_s), (v_ref, v_s), (w_ref, w_s), (a_ref, b_s))
        for q in range(n // SCAN_BATCH):
            for src, dst in pairs:
                halves = _swap_halves(*(src[bi, pl.ds(s0, SUBLANES), q * LANES:(q + 1) * LANES]
                                        for bi in range(SCAN_BATCH)), low_half)
                for j, val in enumerate(halves):
                    dst[tile(q * SCAN_BATCH + j), :] = val
        for q in range(n // SCAN_BATCH):
            for _, dst in pairs:
                outs = _swap_chunks(*(dst[tile(q * SCAN_BATCH + j), :] for j in range(SCAN_BATCH)), even_chunk)
                for j, val in enumerate(outs):
                    dst[tile(q * SCAN_BATCH + j), :] = val
        for q in range(n // SCAN_BATCH):
            for j in range(SCAN_BATCH):
                rows = slice((q * SCAN_BATCH + j) * SUBLANES, (q * SCAN_BATCH + j + 1) * SUBLANES)
                r, k, w, a = wr_s[rows, :], kk_s[rows, :], w_s[rows, :], b_s[rows, :]
                kk0 = k * kkp_ref[rows, :]
                k2 = k * (1.0 + (a - 1.0) * kap_ref[rows, :])
                b0 = kk0 * a
                rk2 = r * k2
                kk_s[rows, :] = kk0
                wr_s[rows, :] = w * r
                b_s[rows, :] = b0
                k2_s[rows, :] = k2
                terms = (kk0 * kk0, b0 * r, rk2, rk2 * rkp_ref[rows, :])
                if ss is None:
                    ss, br, kr, bonus = terms
                else:
                    ss, br, kr, bonus = ss + terms[0], br + terms[1], kr + terms[2], bonus + terms[3]
        inv2 = 1.0 / jnp.maximum(ss, 1e-24)
        row_s[0] = inv2
        row_s[1] = br
        row_s[2] = kr
        row_s[3] = bonus

        def step(t, c):
            inv2_row = row_s[0, pl.ds(t, 1), :]
            br_row = row_s[1, pl.ds(t, 1), :]
            kr_row = row_s[2, pl.ds(t, 1), :]
            for g0 in range(0, n_vg, SCAN_V_GROUPS):
                groups = range(g0, g0 + SCAN_V_GROUPS)
                acc_sa = {g: [None] * SCAN_PARTIALS for g in groups}
                acc_y = {g: [None] * SCAN_PARTIALS for g in groups}
                for ki in range(n):
                    kk_row = kk_s[pl.ds(ki * SUBLANES + t, 1), :]
                    wr_row = wr_s[pl.ds(ki * SUBLANES + t, 1), :]
                    p = ki % SCAN_PARTIALS
                    for g in groups:
                        s = st_ref[ki, g * SUBLANES:(g + 1) * SUBLANES, :]
                        t_sa = s * kk_row
                        t_y = s * wr_row
                        acc_sa[g][p] = t_sa if acc_sa[g][p] is None else acc_sa[g][p] + t_sa
                        acc_y[g][p] = t_y if acc_y[g][p] is None else acc_y[g][p] + t_y
                sa = {}
                vv = {}
                for g in groups:
                    v_rows = pl.ds(g * SUBLANES * SUBLANES + t, SUBLANES, stride=SUBLANES)
                    sa[g] = -functools.reduce(lambda x, y: x + y, acc_sa[g]) * inv2_row
                    vv[g] = v_s[v_rows, :]
                    y_s[v_rows, :] = (functools.reduce(lambda x, y: x + y, acc_y[g])
                                      + sa[g] * br_row + vv[g] * kr_row)
                for ki in range(n):
                    w_row = w_s[pl.ds(ki * SUBLANES + t, 1), :]
                    b_row = b_s[pl.ds(ki * SUBLANES + t, 1), :]
                    k_row = k2_s[pl.ds(ki * SUBLANES + t, 1), :]
                    for g in groups:
                        s = st_ref[ki, g * SUBLANES:(g + 1) * SUBLANES, :]
                        st_ref[ki, g * SUBLANES:(g + 1) * SUBLANES, :] = (
                            s * w_row + sa[g] * b_row + vv[g] * k_row)
            return c

        lax.fori_loop(0, SUBLANES, step, 0)

        tot = None
        for vi in range(n):
            yv = y_s[vi * SUBLANES:(vi + 1) * SUBLANES, :]
            tot = yv if tot is None else tot + yv
        mean = tot * (1.0 / n)
        sq = None
        for vi in range(n):
            yc = y_s[vi * SUBLANES:(vi + 1) * SUBLANES, :] - mean
            sq = yc * yc if sq is None else sq + yc * yc
        rstd = lax.rsqrt(sq * (1.0 / n) + GN_EPS)
        bonus_v = row_s[3]
        for q in range(n // SCAN_BATCH):
            outs = []
            for j in range(SCAN_BATCH):
                rows = tile(q * SCAN_BATCH + j)
                outs.append((y_s[rows, :] - mean) * rstd * gng_ref[rows, :] + gnb_ref[rows, :]
                            + bonus_v * v_s[rows, :])
            for j, val in enumerate(_swap_halves(*outs, low_half)):
                y_s[tile(q * SCAN_BATCH + j), :] = val
        for q in range(n // SCAN_BATCH):
            outs = _swap_chunks(*(y_s[tile(q * SCAN_BATCH + j), :] for j in range(SCAN_BATCH)), even_chunk)
            for bi, val in enumerate(outs):
                y_ref[bi, pl.ds(s0, SUBLANES), q * LANES:(q + 1) * LANES] = val
        return carry

    lax.fori_loop(0, steps // SUBLANES, eight_steps, 0)


def _rwkv_scan(r, k, v, w, a, kkp, kap, rkp, gng, gnb, steps=32):
    bsz, seq, dm = r.shape
    n = RWKV_HEAD_SIZE
    assert bsz == SCAN_BATCH and dm == n * SCAN_CHUNK
    seq_spec = pl.BlockSpec((bsz, steps, dm), lambda i: (0, i, 0))
    par_spec = pl.BlockSpec((n * SUBLANES, LANES), lambda i: (0, 0))
    rows = pltpu.VMEM((n * SUBLANES, LANES), F32)
    return pl.pallas_call(
        functools.partial(_scan_kernel, steps=steps),
        grid=(seq // steps,),
        in_specs=[seq_spec] * 5 + [par_spec] * 5,
        out_specs=seq_spec,
        out_shape=jax.ShapeDtypeStruct((bsz, seq, dm), F32),
        scratch_shapes=[pltpu.VMEM((n, n, LANES), F32)] + [rows] * 7 + [pltpu.VMEM((4, SUBLANES, LANES), F32)],
        compiler_params=_params("arbitrary"),
        name="rwkv7_scan",
    )(r, k, v, w, a, kkp, kap, rkp, gng, gnb)


def _mm_ln_kernel(*refs, nk, gated, emit_bf16):
    refs = list(refs)
    a_ref = refs.pop(0)
    gate_ref = refs.pop(0) if gated else None
    w_ref, x_ref, g_ref, b_ref, o_ref = refs[:5]
    ob_ref = refs[5] if emit_bf16 else None
    acc_ref = refs[-1]

    a = a_ref[...]
    if gated:
        a = (a * gate_ref[...]).astype(BF16)
    part = jnp.dot(a, w_ref[...], preferred_element_type=F32)

    def finalize():
        def chunk(c, carry):
            rows = pl.ds(pl.multiple_of(c * LN_CHUNK_ROWS, LN_CHUNK_ROWS), LN_CHUNK_ROWS)
            z = DEEPNORM_ALPHA * x_ref[rows, :] + acc_ref[rows, :]
            mu = jnp.mean(z, axis=-1, keepdims=True)
            zc = z - mu
            var = jnp.mean(zc * zc, axis=-1, keepdims=True)
            out = zc * lax.rsqrt(var + LN_EPS) * g_ref[...] + b_ref[...]
            o_ref[rows, :] = out
            if emit_bf16:
                ob_ref[rows, :] = out.astype(BF16)
            return carry

        lax.fori_loop(0, acc_ref.shape[0] // LN_CHUNK_ROWS, chunk, 0)

    if nk == 1:
        acc_ref[...] = part
        finalize()
    else:
        kk = pl.program_id(1)

        @pl.when(kk == 0)
        def _():
            acc_ref[...] = part

        @pl.when(kk > 0)
        def _():
            acc_ref[...] += part

        @pl.when(kk == nk - 1)
        def _():
            finalize()


def _matmul_residual_ln(a, w, x, g, b, *, gate=None, tm, tk, emit_bf16=True):
    m, kdim = a.shape
    n = w.shape[1]
    nk = kdim // tk
    row = lambda i, kk: (i, 0)
    in_specs = [pl.BlockSpec((tm, tk), lambda i, kk: (i, kk))]
    args = [a]
    if gate is not None:
        in_specs.append(pl.BlockSpec((tm, tk), lambda i, kk: (i, kk)))
        args.append(gate)
    in_specs += [pl.BlockSpec((tk, n), lambda i, kk: (kk, 0)),
                 pl.BlockSpec((tm, n), row),
                 pl.BlockSpec((1, n), lambda i, kk: (0, 0)),
                 pl.BlockSpec((1, n), lambda i, kk: (0, 0))]
    args += [w, x, g.reshape(1, n), b.reshape(1, n)]
    out_specs = [pl.BlockSpec((tm, n), row)]
    out_shape = [jax.ShapeDtypeStruct((m, n), F32)]
    if emit_bf16:
        out_specs.append(pl.BlockSpec((tm, n), row))
        out_shape.append(jax.ShapeDtypeStruct((m, n), BF16))
    outs = pl.pallas_call(
        functools.partial(_mm_ln_kernel, nk=nk, gated=gate is not None, emit_bf16=emit_bf16),
        grid=(m // tm, nk),
        in_specs=in_specs,
        out_specs=out_specs,
        out_shape=out_shape,
        scratch_shapes=[pltpu.VMEM((tm, n), F32)],
        compiler_params=_params("parallel", "arbitrary"),
        name="matmul_residual_ln",
    )(*args)
    return outs if emit_bf16 else (outs[0], None)


def _ffn_up_kernel(x_ref, wg_ref, wu_ref, cw_ref, cb_ref, o_ref, gbuf, *, tm, tiles_per_seq):
    i = pl.program_id(1)
    x = x_ref[...]
    gate = jnp.dot(x, wg_ref[...], preferred_element_type=F32)
    up = jnp.dot(x, wu_ref[...], preferred_element_type=F32)

    @pl.when(i % tiles_per_seq == 0)
    def _():
        gbuf[0:SUBLANES, :] = jnp.zeros((SUBLANES, gbuf.shape[1]), F32)

    @pl.when(i % tiles_per_seq != 0)
    def _():
        gbuf[0:SUBLANES, :] = gbuf[tm:tm + SUBLANES, :]

    gbuf[SUBLANES:SUBLANES + tm, :] = gate
    g1 = gbuf[SUBLANES - 1:SUBLANES - 1 + tm, :]
    g2 = gbuf[SUBLANES - 2:SUBLANES - 2 + tm, :]
    acc = cb_ref[...] + g2 * cw_ref[0:1, :] + g1 * cw_ref[1:2, :] + gate * cw_ref[2:3, :]
    o_ref[...] = (acc * jax.nn.sigmoid(acc) * up).astype(o_ref.dtype)


def _ffn_up(xb, w_up, conv_w, conv_b, seq, tm=1024, tn=512):
    t, dm = xb.shape
    dff = w_up.shape[1] // 2
    n_col = dff // tn
    return pl.pallas_call(
        functools.partial(_ffn_up_kernel, tm=tm, tiles_per_seq=seq // tm),
        grid=(n_col, t // tm),
        in_specs=[
            pl.BlockSpec((tm, dm), lambda j, i: (i, 0)),
            pl.BlockSpec((dm, tn), lambda j, i: (0, j)),
            pl.BlockSpec((dm, tn), lambda j, i: (0, j + n_col)),
            pl.BlockSpec((CONV_WIDTH, tn), lambda j, i: (0, j)),
            pl.BlockSpec((1, tn), lambda j, i: (0, j)),
        ],
        out_specs=pl.BlockSpec((tm, tn), lambda j, i: (i, j)),
        out_shape=jax.ShapeDtypeStruct((t, dff), BF16),
        scratch_shapes=[pltpu.VMEM((tm + 2 * SUBLANES, tn), F32)],
        compiler_params=_params("parallel", "arbitrary"),
        name="ffn_up_convglu",
    )(xb, w_up, w_up, conv_w, conv_b.reshape(1, dff))


ATTN_SPAN = 128
ATTN_TILE = 2048
ATTN_MERGE_ROWS = 64


def _block_attention(q, k, v, valid):
    s = lax.dot_general(q.astype(BF16), k.astype(BF16), (((1,), (1,)), ((), ())),
                        preferred_element_type=F32) * (ATTN_HEAD_DIM ** -0.5)
    s = jnp.where(valid, s, MASK_VALUE)
    m = jnp.max(s, axis=-1, keepdims=True)
    p = jnp.exp(s - m)
    l = jnp.sum(p, axis=-1, keepdims=True)
    o = jnp.dot(p.astype(BF16), v.astype(BF16), preferred_element_type=F32) * (1.0 / l)
    return o, m + jnp.log(l)


def _attn_kernel(q1, k1, v1, kp1, vp1, q4, k4, v4, kp4, vp4, q16, k16, v16, kp16, vp16,
                 o_ref, og, lg, *, tiles_per_seq):
    span, hd = ATTN_SPAN, ATTN_HEAD_DIM
    first_tile = pl.program_id(1) % tiles_per_seq == 0
    qi = lax.broadcasted_iota(jnp.int32, (span, 2 * span), 0)
    kj = lax.broadcasted_iota(jnp.int32, (span, 2 * span), 1)
    valid = (kj <= qi + span) & (kj >= qi)
    valid_edge = valid & (kj >= jnp.where(first_tile, span, 0))

    def emit(g, rows, q, k, v, mask):
        o, lse = _block_attention(q, k, v, mask)
        og[g, rows, :] = o
        lg[g, rows, :] = jnp.broadcast_to(lse, (span, hd))

    def cat(a, b):
        return jnp.concatenate([a, b], axis=0)

    n1 = ATTN_TILE // span
    rows = pl.ds(0, span)
    emit(0, rows, q1[rows, :], cat(kp1[...], k1[rows, :]), cat(vp1[...], v1[rows, :]), valid_edge)

    def body1(i, c):
        for u in range(5):
            j = 1 + 5 * i + u
            rows = pl.ds(pl.multiple_of(j * span, span), span)
            keys = pl.ds(pl.multiple_of((j - 1) * span, span), 2 * span)
            emit(0, rows, q1[rows, :], k1[keys, :], v1[keys, :], valid)
        return c

    lax.fori_loop(0, (n1 - 1) // 5, body1, 0)

    d4 = 4
    nb4 = ATTN_TILE // (span * d4)

    def body4(i, c):
        for u in range(2):
            r = 2 * i + u
            rows = pl.ds(r, span, stride=d4)
            emit(1, rows, q4[rows, :], cat(kp4[rows, :], k4[rows, :]), cat(vp4[rows, :], v4[rows, :]),
                 valid_edge)
            for j in range(1, nb4):
                rows = pl.ds(j * span * d4 + r, span, stride=d4)
                keys = pl.ds((j - 1) * span * d4 + r, 2 * span, stride=d4)
                emit(1, rows, q4[rows, :], k4[keys, :], v4[keys, :], valid)
        return c

    lax.fori_loop(0, d4 // 2, body4, 0)

    d16 = 16

    def body16(i, c):
        for u in range(8):
            rows = pl.ds(8 * i + u, span, stride=d16)
            emit(2, rows, q16[rows, :], cat(kp16[rows, :], k16[rows, :]), cat(vp16[rows, :], v16[rows, :]),
                 valid_edge)
        return c

    lax.fori_loop(0, d16 // 8, body16, 0)

    def merge(c, carry):
        rows = pl.ds(pl.multiple_of(c * ATTN_MERGE_ROWS, ATTN_MERGE_ROWS), ATTN_MERGE_ROWS)
        l0, l1, l2 = lg[0, rows, :], lg[1, rows, :], lg[2, rows, :]
        m = jnp.maximum(jnp.maximum(l0, l1), l2)
        e0, e1, e2 = jnp.exp(l0 - m), jnp.exp(l1 - m), jnp.exp(l2 - m)
        num = e0 * og[0, rows, :] + e1 * og[1, rows, :] + e2 * og[2, rows, :]
        o_ref[rows, :] = (num / (e0 + e1 + e2)).astype(o_ref.dtype)
        return carry

    lax.fori_loop(0, ATTN_TILE // ATTN_MERGE_ROWS, merge, 0)


def _dilated_attention(qkv, seq):
    t, width = qkv.shape
    hd, span, tile = ATTN_HEAD_DIM, ATTN_SPAN, ATTN_TILE
    heads = width // (3 * len(DILATED_GROUPS) * hd)
    tiles_per_seq = seq // tile

    def col(group, which):
        return lambda h: (group * 3 + which) * heads + h

    def cur(group, which):
        return pl.BlockSpec((tile, hd), lambda h, i: (i, col(group, which)(h)))

    def prev(group, which, rows):
        per = tile // rows
        return pl.BlockSpec((rows, hd), lambda h, i: (jnp.maximum(i * per - 1, 0), col(group, which)(h)))

    in_specs = []
    for group, (window, dilation) in enumerate(DILATED_GROUPS):
        assert window // dilation == span and tile % window == 0
        in_specs += [cur(group, 0), cur(group, 1), cur(group, 2), prev(group, 1, window), prev(group, 2, window)]
    return pl.pallas_call(
        functools.partial(_attn_kernel, tiles_per_seq=tiles_per_seq),
        grid=(heads, t // tile),
        in_specs=in_specs,
        out_specs=pl.BlockSpec((tile, hd), lambda h, i: (i, h)),
        out_shape=jax.ShapeDtypeStruct((t, heads * hd), BF16),
        scratch_shapes=[pltpu.VMEM((len(DILATED_GROUPS), tile, hd), F32)] * 2,
        compiler_params=_params("parallel", "parallel"),
        name="dilated_attention",
    )(*([qkv] * len(in_specs)))


def _pad_rank(p, q):
    r = p.shape[1]
    if r % LORA_PAD:
        extra = LORA_PAD - r % LORA_PAD
        p = jnp.pad(p, ((0, 0), (0, extra)))
        q = jnp.pad(q, ((0, extra), (0, 0)))
    return p.astype(BF16), q.astype(BF16)


def _rwkv_layer(x, mu, w_rkv, w0, w1, w2, a0, a1, a2, g1, g2, k_k, k_a, r_k, gn_g, gn_b, w_out, ln_g, ln_b):
    bsz, seq, dm = x.shape
    n = RWKV_HEAD_SIZE
    heads = dm // n
    t = bsz * seq
    def cols(wt):
        lead = wt.shape[:-1]
        return wt.reshape(*lead, heads, n).swapaxes(-1, -2).reshape(*lead, dm)

    mixed = _token_shift_mix(x, mu[jnp.array([0, 2, 3, 1, 4, 5])])
    rkv = _batched_matmul(mixed, cols(w_rkv).astype(BF16), 3, F32)
    w1b, w2b = _pad_rank(w1, cols(w2))
    a1b, a2b = _pad_rank(a1, cols(a2))
    g1b, g2b = _pad_rank(g1, cols(g2))
    decay, a, g = _rwkv_lora(mixed, w1b, w2b, cols(w0).reshape(1, dm), a1b, a2b, cols(a0).reshape(1, dm),
                             g1b, g2b)

    def lanes_param(p):
        return jnp.repeat(jnp.tile(p.reshape(heads, n).T, (1, bsz)), SUBLANES, axis=0)

    def seq_major(z):
        return z.reshape(bsz, seq, dm)

    y = _rwkv_scan(seq_major(rkv[0]), seq_major(rkv[1]), seq_major(rkv[2]), seq_major(decay), seq_major(a),
                   lanes_param(k_k), lanes_param(k_a), lanes_param(r_k), lanes_param(gn_g), lanes_param(gn_b))
    w_out_rows = w_out.reshape(heads, n, dm).swapaxes(0, 1).reshape(dm, dm)
    return _matmul_residual_ln(y.reshape(t, dm), w_out_rows.astype(BF16), x.reshape(t, dm), ln_g, ln_b,
                               gate=g, tm=256, tk=dm)


def _attn_layer(x32, xb, w_in, w_out, ln_g, ln_b, bsz, seq):
    t, dm = x32.shape
    qkv = _batched_matmul(xb[None], w_in.astype(BF16)[None], 1, F32)[0]
    merged = _dilated_attention(qkv, seq)
    return _matmul_residual_ln(merged, w_out.astype(BF16), x32, ln_g, ln_b, tm=512, tk=dm)


def _ffn_layer(x32, xb, w_up, conv_w, conv_b, w_down, ln_g, ln_b, seq, emit_bf16):
    act = _ffn_up(xb, w_up.astype(BF16), conv_w, conv_b, seq)
    return _matmul_residual_ln(act, w_down.astype(BF16), x32, ln_g, ln_b, tm=512, tk=512, emit_bf16=emit_bf16)


def kernel(x, rwkv_mu, rwkv_w_rkv, rwkv_w0, rwkv_w1, rwkv_w2, rwkv_a0, rwkv_a1, rwkv_a2, rwkv_g1, rwkv_g2, rwkv_k_k, rwkv_k_a, rwkv_r_k, rwkv_gn_g, rwkv_gn_b, rwkv_w_out, attn_w_in, attn_w_out, ffn_w_up, ffn_conv_w, ffn_conv_b, ffn_w_down, ln_mix_g, ln_mix_b, ln_ffn_g, ln_ffn_b):
    bsz, seq, dm = x.shape
    x32, xb = _rwkv_layer(x, rwkv_mu[0], rwkv_w_rkv[0], rwkv_w0[0], rwkv_w1[0], rwkv_w2[0], rwkv_a0[0],
                          rwkv_a1[0], rwkv_a2[0], rwkv_g1[0], rwkv_g2[0], rwkv_k_k[0], rwkv_k_a[0],
                          rwkv_r_k[0], rwkv_gn_g[0], rwkv_gn_b[0], rwkv_w_out[0], ln_mix_g[0], ln_mix_b[0])
    x32, xb = _ffn_layer(x32, xb, ffn_w_up[0], ffn_conv_w[0], ffn_conv_b[0], ffn_w_down[0],
                         ln_ffn_g[0], ln_ffn_b[0], seq, True)
    x32, xb = _attn_layer(x32, xb, attn_w_in[0], attn_w_out[0], ln_mix_g[1], ln_mix_b[1], bsz, seq)
    x32, _ = _ffn_layer(x32, xb, ffn_w_up[1], ffn_conv_w[1], ffn_conv_b[1], ffn_w_down[1],
                        ln_ffn_g[1], ln_ffn_b[1], seq, False)
    return x32.reshape(bsz, seq, dm)
```

```python
import functools

import jax
import jax.numpy as jnp
from jax import lax
from jax.experimental import pallas as pl
from jax.experimental.pallas import tpu as pltpu

F32 = jnp.float32
BF16 = jnp.bfloat16

V7X_VMEM_BYTES = 64 * 1024 * 1024
VMEM_LIMIT_BYTES = V7X_VMEM_BYTES - 8 * 1024 * 1024
SUBLANES = 8
LANES = 128

RWKV_HEAD_SIZE = 64
N_SHIFT_MIX = 6
GN_EPS = 64e-5
ATTN_HEAD_DIM = 128
DILATED_GROUPS = ((128, 1), (512, 4), (2048, 16))
MASK_VALUE = -1e30
CONV_WIDTH = 3
LN_EPS = 1e-5
DEPTH = 2
DEEPNORM_ALPHA = (2.0 * DEPTH) ** 0.25
LORA_PAD = 128
LN_CHUNK_ROWS = 32


def _params(*semantics):
    return pltpu.CompilerParams(dimension_semantics=semantics, vmem_limit_bytes=VMEM_LIMIT_BYTES)


def _mix_kernel(x_ref, xp_ref, mu_ref, o_ref):
    s = pl.program_id(1)
    x = x_ref[...]
    before = jnp.where(s > 0, xp_ref[SUBLANES - 1:SUBLANES, :], 0.0)
    prev = pltpu.roll(x, 1, 0)
    row = lax.broadcasted_iota(jnp.int32, x.shape, 0)
    prev = jnp.where(row == 0, before, prev)
    xx = prev - x
    for c in range(N_SHIFT_MIX):
        o_ref[c] = (x + xx * mu_ref[c:c + 1, :]).astype(o_ref.dtype)


def _token_shift_mix(x, mu, ts=256):
    bsz, seq, dm = x.shape
    ts = min(ts, seq)
    nst = seq // ts
    return pl.pallas_call(
        _mix_kernel,
        grid=(bsz, nst),
        in_specs=[
            pl.BlockSpec((None, ts, dm), lambda b, s: (b, s, 0)),
            pl.BlockSpec((None, SUBLANES, dm), lambda b, s: (b, jnp.maximum(s * (ts // SUBLANES) - 1, 0), 0)),
            pl.BlockSpec((N_SHIFT_MIX, dm), lambda b, s: (0, 0)),
        ],
        out_specs=pl.BlockSpec((N_SHIFT_MIX, ts, dm), lambda b, s: (0, b * nst + s, 0)),
        out_shape=jax.ShapeDtypeStruct((N_SHIFT_MIX, bsz * seq, dm), BF16),
        compiler_params=_params("parallel", "parallel"),
        name="token_shift_mix",
    )(x, x, mu)


def _mm_kernel(a_ref, w_ref, o_ref):
    o_ref[...] = jnp.dot(a_ref[...], w_ref[...], preferred_element_type=F32).astype(o_ref.dtype)


def _batched_matmul(a, w, n_batch, out_dtype, tm=1024, tn=1024):
    _, m, kdim = a.shape
    tm = min(tm, m)
    n = w.shape[2]
    return pl.pallas_call(
        _mm_kernel,
        grid=(n_batch, n // tn, m // tm),
        in_specs=[
            pl.BlockSpec((None, tm, kdim), lambda c, j, i: (c, i, 0)),
            pl.BlockSpec((None, kdim, tn), lambda c, j, i: (c, 0, j)),
        ],
        out_specs=pl.BlockSpec((None, tm, tn), lambda c, j, i: (c, i, j)),
        out_shape=jax.ShapeDtypeStruct((n_batch, m, n), out_dtype),
        compiler_params=_params("parallel", "parallel", "parallel"),
        name="batched_matmul",
    )(a, w)


def _lora_kernel(xw_ref, xa_ref, xg_ref, w1_ref, w2_ref, w0_ref, a1_ref, a2_ref, a0_ref,
                 g1_ref, g2_ref, decay_ref, a_ref, g_ref):
    def two_stage(x_ref, p_ref, q_ref, act):
        h = act(jnp.dot(x_ref[...], p_ref[...], preferred_element_type=F32))
        return jnp.dot(h.astype(BF16), q_ref[...], preferred_element_type=F32)

    z = w0_ref[...] + two_stage(xw_ref, w1_ref, w2_ref, jnp.tanh)
    decay_ref[...] = jnp.exp(-jnp.exp(-0.5) * jax.nn.sigmoid(z))
    a_ref[...] = jax.nn.sigmoid(a0_ref[...] + two_stage(xa_ref, a1_ref, a2_ref, lambda h: h))
    g_ref[...] = two_stage(xg_ref, g1_ref, g2_ref, jax.nn.sigmoid)


def _rwkv_lora(mixed, w1, w2, w0, a1, a2, a0, g1, g2, tm=256):
    _, t, dm = mixed.shape

    def full(arr):
        return pl.BlockSpec(arr.shape, lambda i: (0,) * arr.ndim)

    def mix_spec(c):
        return pl.BlockSpec((None, tm, dm), lambda i: (c, i, 0))

    out_spec = pl.BlockSpec((tm, dm), lambda i: (i, 0))
    out = jax.ShapeDtypeStruct((t, dm), F32)
    return pl.pallas_call(
        _lora_kernel,
        grid=(t // tm,),
        in_specs=[mix_spec(3), mix_spec(4), mix_spec(5), full(w1), full(w2), full(w0),
                  full(a1), full(a2), full(a0), full(g1), full(g2)],
        out_specs=[out_spec, out_spec, out_spec],
        out_shape=[out, out, out],
        compiler_params=_params("parallel"),
        name="rwkv_lora",
    )(mixed, mixed, mixed, w1, w2, w0, a1, a2, a0, g1, g2)


SCAN_V_GROUPS = 2
SCAN_PARTIALS = 4


SCAN_BATCH = 4
SCAN_CHUNK = LANES // SCAN_BATCH


def _swap_halves(a0, a1, a2, a3, low_half):
    r0, r1, r2, r3 = (pltpu.roll(a, 2 * SCAN_CHUNK, 1) for a in (a0, a1, a2, a3))
    return (jnp.where(low_half, a0, r2), jnp.where(low_half, a1, r3),
            jnp.where(low_half, r0, a2), jnp.where(low_half, r1, a3))


def _swap_chunks(p0, p1, p2, p3, even_chunk):
    return (jnp.where(even_chunk, p0, pltpu.roll(p1, SCAN_CHUNK, 1)),
            jnp.where(even_chunk, pltpu.roll(p0, 3 * SCAN_CHUNK, 1), p1),
            jnp.where(even_chunk, p2, pltpu.roll(p3, SCAN_CHUNK, 1)),
            jnp.where(even_chunk, pltpu.roll(p2, 3 * SCAN_CHUNK, 1), p3))


def _scan_kernel(r_ref, k_ref, v_ref, w_ref, a_ref, kkp_ref, kap_ref, rkp_ref, gng_ref, gnb_ref,
                 y_ref, st_ref, kk_s, wr_s, w_s, b_s, k2_s, v_s, y_s, row_s, *, steps):
    n = RWKV_HEAD_SIZE
    n_vg = n // SUBLANES
    lane = lax.broadcasted_iota(jnp.int32, (SUBLANES, LANES), 1)
    low_half = lane < 2 * SCAN_CHUNK
    even_chunk = (lane & SCAN_CHUNK) == 0

    @pl.when(pl.program_id(0) == 0)
    def _():
        st_ref[...] = jnp.zeros_like(st_ref)

    def tile(i):
        return slice(i * SUBLANES, (i + 1) * SUBLANES)

    def eight_steps(s8, carry):
        s0 = pl.multiple_of(s8 * SUBLANES, SUBLANES)
        ss = br = kr = bonus = None
        pairs = ((r_ref, wr_s), (k_ref, kk_s), (v_ref, v_s), (w_ref, w_s), (a_ref, b_s))
        for q in range(n // SCAN_BATCH):
            for src, dst in pairs:
                halves = _swap_halves(*(src[bi, pl.ds(s0, SUBLANES), q * LANES:(q + 1) * LANES]
                                        for bi in range(SCAN_BATCH)), low_half)
                for j, val in enumerate(halves):
                    dst[tile(q * SCAN_BATCH + j), :] = val
        for q in range(n // SCAN_BATCH):
            for _, dst in pairs:
                outs = _swap_chunks(*(dst[tile(q * SCAN_BATCH + j), :] for j in range(SCAN_BATCH)), even_chunk)
                for j, val in enumerate(outs):
                    dst[tile(q * SCAN_BATCH + j), :] = val
        for q in range(n // SCAN_BATCH):
            for j in range(SCAN_BATCH):
                rows = slice((q * SCAN_BATCH + j) * SUBLANES, (q * SCAN_BATCH + j + 1) * SUBLANES)
                r, k, w, a = wr_s[rows, :], kk_s[rows, :], w_s[rows, :], b_s[rows, :]
                kk0 = k * kkp_ref[rows, :]
                k2 = k * (1.0 + (a - 1.0) * kap_ref[rows, :])
                b0 = kk0 * a
                rk2 = r * k2
                kk_s[rows, :] = kk0
                wr_s[rows, :] = w * r
                b_s[rows, :] = b0
                k2_s[rows, :] = k2
                terms = (kk0 * kk0, b0 * r, rk2, rk2 * rkp_ref[rows, :])
                if ss is None:
                    ss, br, kr, bonus = terms
                else:
                    ss, br, kr, bonus = ss + terms[0], br + terms[1], kr + terms[2], bonus + terms[3]
        inv2 = 1.0 / jnp.maximum(ss, 1e-24)
        row_s[0] = inv2
        row_s[1] = br
        row_s[2] = kr
        row_s[3] = bonus

        def step(t, c):
            inv2_row = row_s[0, pl.ds(t, 1), :]
            br_row = row_s[1, pl.ds(t, 1), :]
            kr_row = row_s[2, pl.ds(t, 1), :]
            for g0 in range(0, n_vg, SCAN_V_GROUPS):
                groups = range(g0, g0 + SCAN_V_GROUPS)
                acc_sa = {g: [None] * SCAN_PARTIALS for g in groups}
                acc_y = {g: [None] * SCAN_PARTIALS for g in groups}
                for ki in range(n):
                    kk_row = kk_s[pl.ds(ki * SUBLANES + t, 1), :]
                    wr_row = wr_s[pl.ds(ki * SUBLANES + t, 1), :]
                    p = ki % SCAN_PARTIALS
                    for g in groups:
                        s = st_ref[ki, g * SUBLANES:(g + 1) * SUBLANES, :]
                        t_sa = s * kk_row
                        t_y = s * wr_row
                        acc_sa[g][p] = t_sa if acc_sa[g][p] is None else acc_sa[g][p] + t_sa
                        acc_y[g][p] = t_y if acc_y[g][p] is None else acc_y[g][p] + t_y
                sa = {}
                vv = {}
                for g in groups:
                    v_rows = pl.ds(g * SUBLANES * SUBLANES + t, SUBLANES, stride=SUBLANES)
                    sa[g] = -functools.reduce(lambda x, y: x + y, acc_sa[g]) * inv2_row
                    vv[g] = v_s[v_rows, :]
                    y_s[v_rows, :] = (functools.reduce(lambda x, y: x + y, acc_y[g])
                                      + sa[g] * br_row + vv[g] * kr_row)
                for ki in range(n):
                    w_row = w_s[pl.ds(ki * SUBLANES + t, 1), :]
                    b_row = b_s[pl.ds(ki * SUBLANES + t, 1), :]
                    k_row = k2_s[pl.ds(ki * SUBLANES + t, 1), :]
                    for g in groups:
                        s = st_ref[ki, g * SUBLANES:(g + 1) * SUBLANES, :]
                        st_ref[ki, g * SUBLANES:(g + 1) * SUBLANES, :] = (
                            s * w_row + sa[g] * b_row + vv[g] * k_row)
            return c

        lax.fori_loop(0, SUBLANES, step, 0)

        tot = None
        for vi in range(n):
            yv = y_s[vi * SUBLANES:(vi + 1) * SUBLANES, :]
            tot = yv if tot is None else tot + yv
        mean = tot * (1.0 / n)
        sq = None
        for vi in range(n):
            yc = y_s[vi * SUBLANES:(vi + 1) * SUBLANES, :] - mean
            sq = yc * yc if sq is None else sq + yc * yc
        rstd = lax.rsqrt(sq * (1.0 / n) + GN_EPS)
        bonus_v = row_s[3]
        for q in range(n // SCAN_BATCH):
            outs = []
            for j in range(SCAN_BATCH):
                rows = tile(q * SCAN_BATCH + j)
                outs.append((y_s[rows, :] - mean) * rstd * gng_ref[rows, :] + gnb_ref[rows, :]
                            + bonus_v * v_s[rows, :])
            for j, val in enumerate(_swap_halves(*outs, low_half)):
                y_s[tile(q * SCAN_BATCH + j), :] = val
        for q in range(n // SCAN_BATCH):
            outs = _swap_chunks(*(y_s[tile(q * SCAN_BATCH + j), :] for j in range(SCAN_BATCH)), even_chunk)
            for bi, val in enumerate(outs):
                y_ref[bi, pl.ds(s0, SUBLANES), q * LANES:(q + 1) * LANES] = val
        return carry

    lax.fori_loop(0, steps // SUBLANES, eight_steps, 0)


def _rwkv_scan(rkv, w, a, kkp, kap, rkp, gng, gnb, steps=32):
    bsz, seq, dm = w.shape
    n = RWKV_HEAD_SIZE
    assert bsz == SCAN_BATCH and dm == n * SCAN_CHUNK

    def rkv_spec(c):
        return pl.BlockSpec((None, bsz, steps, dm), lambda i: (c, 0, i, 0))

    seq_spec = pl.BlockSpec((bsz, steps, dm), lambda i: (0, i, 0))
    par_spec = pl.BlockSpec((n * SUBLANES, LANES), lambda i: (0, 0))
    rows = pltpu.VMEM((n * SUBLANES, LANES), F32)
    return pl.pallas_call(
        functools.partial(_scan_kernel, steps=steps),
        grid=(seq // steps,),
        in_specs=[rkv_spec(0), rkv_spec(1), rkv_spec(2), seq_spec, seq_spec] + [par_spec] * 5,
        out_specs=seq_spec,
        out_shape=jax.ShapeDtypeStruct((bsz, seq, dm), F32),
        scratch_shapes=[pltpu.VMEM((n, n, LANES), F32)] + [rows] * 7 + [pltpu.VMEM((4, SUBLANES, LANES), F32)],
        compiler_params=_params("arbitrary"),
        name="rwkv7_scan",
    )(rkv, rkv, rkv, w, a, kkp, kap, rkp, gng, gnb)


def _mm_ln_kernel(*refs, gated, emit_bf16):
    refs = list(refs)
    a_ref = refs.pop(0)
    gate_ref = refs.pop(0) if gated else None
    w_ref, x_ref, g_ref, b_ref, o_ref = refs[:5]
    ob_ref = refs[5] if emit_bf16 else None
    acc_ref = refs[-1]

    a = a_ref[...]
    if gated:
        a = (a * gate_ref[...]).astype(BF16)
    acc_ref[...] = jnp.dot(a, w_ref[...], preferred_element_type=F32)

    for c in range(acc_ref.shape[0] // LN_CHUNK_ROWS):
        rows = slice(c * LN_CHUNK_ROWS, (c + 1) * LN_CHUNK_ROWS)
        z = DEEPNORM_ALPHA * x_ref[rows, :] + acc_ref[rows, :]
        mu = jnp.mean(z, axis=-1, keepdims=True)
        zc = z - mu
        var = jnp.mean(zc * zc, axis=-1, keepdims=True)
        out = zc * lax.rsqrt(var + LN_EPS) * g_ref[...] + b_ref[...]
        o_ref[rows, :] = out
        if emit_bf16:
            ob_ref[rows, :] = out.astype(BF16)


def _matmul_residual_ln(a, w, x, g, b, *, gate=None, tm, emit_bf16=True):
    m, kdim = a.shape
    n = w.shape[1]
    row = lambda i: (i, 0)
    fixed = lambda i: (0, 0)
    in_specs = [pl.BlockSpec((tm, kdim), row)]
    args = [a]
    if gate is not None:
        in_specs.append(pl.BlockSpec((tm, kdim), row))
        args.append(gate)
    in_specs += [pl.BlockSpec((kdim, n), fixed, pipeline_mode=pl.Buffered(1)),
                 pl.BlockSpec((tm, n), row),
                 pl.BlockSpec((1, n), fixed),
                 pl.BlockSpec((1, n), fixed)]
    args += [w, x, g.reshape(1, n), b.reshape(1, n)]
    out_specs = [pl.BlockSpec((tm, n), row)]
    out_shape = [jax.ShapeDtypeStruct((m, n), F32)]
    if emit_bf16:
        out_specs.append(pl.BlockSpec((tm, n), row))
        out_shape.append(jax.ShapeDtypeStruct((m, n), BF16))
    outs = pl.pallas_call(
        functools.partial(_mm_ln_kernel, gated=gate is not None, emit_bf16=emit_bf16),
        grid=(m // tm,),
        in_specs=in_specs,
        out_specs=out_specs,
        out_shape=out_shape,
        scratch_shapes=[pltpu.VMEM((tm, n), F32)],
        compiler_params=_params("parallel"),
        name="matmul_residual_ln",
    )(*args)
    return outs if emit_bf16 else (outs[0], None)


FFN_COL_CHUNK = 256


def _ffn_up_kernel(x_ref, wg_ref, wu_ref, cw_ref, cb_ref, o_ref, tail_ref, *, tiles_per_seq):
    i = pl.program_id(1)
    x = x_ref[...]
    tm = x.shape[0]
    seq_start = i % tiles_per_seq == 0
    row = lax.broadcasted_iota(jnp.int32, (SUBLANES, FFN_COL_CHUNK), 0)
    for c in range(o_ref.shape[1] // FFN_COL_CHUNK):
        cols = slice(c * FFN_COL_CHUNK, (c + 1) * FFN_COL_CHUNK)
        gate = jnp.dot(x, wg_ref[:, cols], preferred_element_type=F32)
        up = jnp.dot(x, wu_ref[:, cols], preferred_element_type=F32)
        tail = jnp.where(seq_start, 0.0, tail_ref[:, cols])
        tail_ref[:, cols] = gate[tm - SUBLANES:, :]
        r1 = pltpu.roll(gate, 1, 0)
        r2 = pltpu.roll(gate, 2, 0)
        h1 = jnp.where(row == 0, tail[SUBLANES - 1:, :], r1[:SUBLANES, :])
        h2 = jnp.where(row == 0, tail[SUBLANES - 2:SUBLANES - 1, :],
                       jnp.where(row == 1, tail[SUBLANES - 1:, :], r2[:SUBLANES, :]))
        g1 = jnp.concatenate([h1, r1[SUBLANES:, :]], axis=0)
        g2 = jnp.concatenate([h2, r2[SUBLANES:, :]], axis=0)
        acc = (cb_ref[:, cols] + g2 * cw_ref[0:1, cols] + g1 * cw_ref[1:2, cols] + gate * cw_ref[2:3, cols])
        o_ref[:, cols] = (acc * jax.nn.sigmoid(acc) * up).astype(o_ref.dtype)


def _ffn_up(xb, w_up, conv_w, conv_b, seq, tm=1024, tn=512):
    t, dm = xb.shape
    dff = w_up.shape[1] // 2
    n_col = dff // tn
    return pl.pallas_call(
        functools.partial(_ffn_up_kernel, tiles_per_seq=seq // tm),
        grid=(n_col, t // tm),
        in_specs=[
            pl.BlockSpec((tm, dm), lambda j, i: (i, 0)),
            pl.BlockSpec((dm, tn), lambda j, i: (0, j)),
            pl.BlockSpec((dm, tn), lambda j, i: (0, j + n_col)),
            pl.BlockSpec((CONV_WIDTH, tn), lambda j, i: (0, j)),
            pl.BlockSpec((1, tn), lambda j, i: (0, j)),
        ],
        out_specs=pl.BlockSpec((tm, tn), lambda j, i: (i, j)),
        out_shape=jax.ShapeDtypeStruct((t, dff), BF16),
        scratch_shapes=[pltpu.VMEM((SUBLANES, tn), F32)],
        compiler_params=_params("parallel", "arbitrary"),
        name="ffn_up_convglu",
    )(xb, w_up, w_up, conv_w, conv_b.reshape(1, dff))


ATTN_SPAN = 128
ATTN_TILE = 2048
ATTN_MERGE_ROWS = 64


def _block_attention(q, k, v, valid):
    s = lax.dot_general(q.astype(BF16), k.astype(BF16), (((1,), (1,)), ((), ())),
                        preferred_element_type=F32) * (ATTN_HEAD_DIM ** -0.5)
    s = jnp.where(valid, s, MASK_VALUE)
    m = jnp.max(s, axis=-1, keepdims=True)
    p = jnp.exp(s - m)
    l = jnp.sum(p, axis=-1, keepdims=True)
    o = jnp.dot(p.astype(BF16), v.astype(BF16), preferred_element_type=F32) * (1.0 / l)
    return o, m + jnp.log(l)


def _attn_kernel(q1, k1, v1, kp1, vp1, q4, k4, v4, kp4, vp4, q16, k16, v16, kp16, vp16,
                 o_ref, og, lg, *, tiles_per_seq):
    span, hd = ATTN_SPAN, ATTN_HEAD_DIM
    first_tile = pl.program_id(1) % tiles_per_seq == 0
    qi = lax.broadcasted_iota(jnp.int32, (span, 2 * span), 0)
    kj = lax.broadcasted_iota(jnp.int32, (span, 2 * span), 1)
    valid = (kj <= qi + span) & (kj >= qi)
    valid_edge = valid & (kj >= jnp.where(first_tile, span, 0))

    def emit(g, rows, q, k, v, mask):
        o, lse = _block_attention(q, k, v, mask)
        og[g, rows, :] = o
        lg[g, rows, :] = jnp.broadcast_to(lse, (span, hd))

    def cat(a, b):
        return jnp.concatenate([a, b], axis=0)

    n1 = ATTN_TILE // span
    rows = pl.ds(0, span)
    emit(0, rows, q1[rows, :], cat(kp1[...], k1[rows, :]), cat(vp1[...], v1[rows, :]), valid_edge)

    def body1(i, c):
        for u in range(5):
            j = 1 + 5 * i + u
            rows = pl.ds(pl.multiple_of(j * span, span), span)
            keys = pl.ds(pl.multiple_of((j - 1) * span, span), 2 * span)
            emit(0, rows, q1[rows, :], k1[keys, :], v1[keys, :], valid)
        return c

    lax.fori_loop(0, (n1 - 1) // 5, body1, 0)

    d4 = 4
    nb4 = ATTN_TILE // (span * d4)

    def body4(i, c):
        for u in range(2):
            r = 2 * i + u
            rows = pl.ds(r, span, stride=d4)
            emit(1, rows, q4[rows, :], cat(kp4[rows, :], k4[rows, :]), cat(vp4[rows, :], v4[rows, :]),
                 valid_edge)
            for j in range(1, nb4):
                rows = pl.ds(j * span * d4 + r, span, stride=d4)
                keys = pl.ds((j - 1) * span * d4 + r, 2 * span, stride=d4)
                emit(1, rows, q4[rows, :], k4[keys, :], v4[keys, :], valid)
        return c

    lax.fori_loop(0, d4 // 2, body4, 0)

    d16 = 16

    def body16(i, c):
        for u in range(8):
            rows = pl.ds(8 * i + u, span, stride=d16)
            emit(2, rows, q16[rows, :], cat(kp16[rows, :], k16[rows, :]), cat(vp16[rows, :], v16[rows, :]),
                 valid_edge)
        return c

    lax.fori_loop(0, d16 // 8, body16, 0)

    def merge(c, carry):
        rows = pl.ds(pl.multiple_of(c * ATTN_MERGE_ROWS, ATTN_MERGE_ROWS), ATTN_MERGE_ROWS)
        l0, l1, l2 = lg[0, rows, :], lg[1, rows, :], lg[2, rows, :]
        m = jnp.maximum(jnp.maximum(l0, l1), l2)
        e0, e1, e2 = jnp.exp(l0 - m), jnp.exp(l1 - m), jnp.exp(l2 - m)
        num = e0 * og[0, rows, :] + e1 * og[1, rows, :] + e2 * og[2, rows, :]
        o_ref[rows, :] = (num / (e0 + e1 + e2)).astype(o_ref.dtype)
        return carry

    lax.fori_loop(0, ATTN_TILE // ATTN_MERGE_ROWS, merge, 0)


def _dilated_attention(qkv, seq):
    t, width = qkv.shape
    hd, span, tile = ATTN_HEAD_DIM, ATTN_SPAN, ATTN_TILE
    heads = width // (3 * len(DILATED_GROUPS) * hd)
    tiles_per_seq = seq // tile

    def col(group, which):
        return lambda h: (group * 3 + which) * heads + h

    def cur(group, which):
        return pl.BlockSpec((tile, hd), lambda h, i: (i, col(group, which)(h)))

    def prev(group, which, rows):
        per = tile // rows
        return pl.BlockSpec((rows, hd), lambda h, i: (jnp.maximum(i * per - 1, 0), col(group, which)(h)))

    in_specs = []
    for group, (window, dilation) in enumerate(DILATED_GROUPS):
        assert window // dilation == span and tile % window == 0
        in_specs += [cur(group, 0), cur(group, 1), cur(group, 2), prev(group, 1, window), prev(group, 2, window)]
    return pl.pallas_call(
        functools.partial(_attn_kernel, tiles_per_seq=tiles_per_seq),
        grid=(heads, t // tile),
        in_specs=in_specs,
        out_specs=pl.BlockSpec((tile, hd), lambda h, i: (i, h)),
        out_shape=jax.ShapeDtypeStruct((t, heads * hd), BF16),
        scratch_shapes=[pltpu.VMEM((len(DILATED_GROUPS), tile, hd), F32)] * 2,
        compiler_params=_params("parallel", "parallel"),
        name="dilated_attention",
    )(*([qkv] * len(in_specs)))


def _pad_rank(p, q):
    r = p.shape[1]
    if r % LORA_PAD:
        extra = LORA_PAD - r % LORA_PAD
        p = jnp.pad(p, ((0, 0), (0, extra)))
        q = jnp.pad(q, ((0, extra), (0, 0)))
    return p.astype(BF16), q.astype(BF16)


def _rwkv_layer(x, mu, w_rkv, w0, w1, w2, a0, a1, a2, g1, g2, k_k, k_a, r_k, gn_g, gn_b, w_out, ln_g, ln_b):
    bsz, seq, dm = x.shape
    n = RWKV_HEAD_SIZE
    heads = dm // n
    t = bsz * seq
    def cols(wt):
        lead = wt.shape[:-1]
        return wt.reshape(*lead, heads, n).swapaxes(-1, -2).reshape(*lead, dm)

    mixed = _token_shift_mix(x, mu[jnp.array([0, 2, 3, 1, 4, 5])])
    rkv = _batched_matmul(mixed, cols(w_rkv).astype(BF16), 3, F32)
    w1b, w2b = _pad_rank(w1, cols(w2))
    a1b, a2b = _pad_rank(a1, cols(a2))
    g1b, g2b = _pad_rank(g1, cols(g2))
    decay, a, g = _rwkv_lora(mixed, w1b, w2b, cols(w0).reshape(1, dm), a1b, a2b, cols(a0).reshape(1, dm),
                             g1b, g2b)

    def lanes_param(p):
        return jnp.repeat(jnp.tile(p.reshape(heads, n).T, (1, bsz)), SUBLANES, axis=0)

    def seq_major(z):
        return z.reshape(bsz, seq, dm)

    y = _rwkv_scan(rkv.reshape(3, bsz, seq, dm), seq_major(decay), seq_major(a),
                   lanes_param(k_k), lanes_param(k_a), lanes_param(r_k), lanes_param(gn_g), lanes_param(gn_b))
    w_out_rows = w_out.reshape(heads, n, dm).swapaxes(0, 1).reshape(dm, dm)
    return _matmul_residual_ln(y.reshape(t, dm), w_out_rows.astype(BF16), x.reshape(t, dm), ln_g, ln_b,
                               gate=g, tm=256)


def _attn_layer(x32, xb, w_in, w_out, ln_g, ln_b, bsz, seq):
    t, dm = x32.shape
    qkv = _batched_matmul(xb[None], w_in.astype(BF16)[None], 1, F32)[0]
    merged = _dilated_attention(qkv, seq)
    return _matmul_residual_ln(merged, w_out.astype(BF16), x32, ln_g, ln_b, tm=512)


def _ffn_layer(x32, xb, w_up, conv_w, conv_b, w_down, ln_g, ln_b, seq, emit_bf16):
    act = _ffn_up(xb, w_up.astype(BF16), conv_w, conv_b, seq)
    return _matmul_residual_ln(act, w_down.astype(BF16), x32, ln_g, ln_b, tm=256, emit_bf16=emit_bf16)


def kernel(x, rwkv_mu, rwkv_w_rkv, rwkv_w0, rwkv_w1, rwkv_w2, rwkv_a0, rwkv_a1, rwkv_a2, rwkv_g1, rwkv_g2, rwkv_k_k, rwkv_k_a, rwkv_r_k, rwkv_gn_g, rwkv_gn_b, rwkv_w_out, attn_w_in, attn_w_out, ffn_w_up, ffn_conv_w, ffn_conv_b, ffn_w_down, ln_mix_g, ln_mix_b, ln_ffn_g, ln_ffn_b):
    bsz, seq, dm = x.shape
    x32, xb = _rwkv_layer(x, rwkv_mu[0], rwkv_w_rkv[0], rwkv_w0[0], rwkv_w1[0], rwkv_w2[0], rwkv_a0[0],
                          rwkv_a1[0], rwkv_a2[0], rwkv_g1[0], rwkv_g2[0], rwkv_k_k[0], rwkv_k_a[0],
                          rwkv_r_k[0], rwkv_gn_g[0], rwkv_gn_b[0], rwkv_w_out[0], ln_mix_g[0], ln_mix_b[0])
    x32, xb = _ffn_layer(x32, xb, ffn_w_up[0], ffn_conv_w[0], ffn_conv_b[0], ffn_w_down[0],
                         ln_ffn_g[0], ln_ffn_b[0], seq, True)
    x32, xb = _attn_layer(x32, xb, attn_w_in[0], attn_w_out[0], ln_mix_g[1], ln_mix_b[1], bsz, seq)
    x32, _ = _ffn_layer(x32, xb, ffn_w_up[1], ffn_conv_w[1], ffn_conv_b[1], ffn_w_down[1],
                        ln_ffn_g[1], ln_ffn_b[1], seq, False)
    return x32.reshape(bsz, seq, dm)
```

```python
import functools

import jax
import jax.numpy as jnp
from jax import lax
from jax.experimental import pallas as pl
from jax.experimental.pallas import tpu as pltpu

F32 = jnp.float32
BF16 = jnp.bfloat16

V7X_VMEM_BYTES = 64 * 1024 * 1024
VMEM_LIMIT_BYTES = V7X_VMEM_BYTES - 8 * 1024 * 1024
SUBLANES = 8
LANES = 128

RWKV_HEAD_SIZE = 64
N_SHIFT_MIX = 6
GN_EPS = 64e-5
ATTN_HEAD_DIM = 128
DILATED_GROUPS = ((128, 1), (512, 4), (2048, 16))
MASK_VALUE = -1e30
CONV_WIDTH = 3
LN_EPS = 1e-5
DEPTH = 2
DEEPNORM_ALPHA = (2.0 * DEPTH) ** 0.25
LORA_PAD = 128
LN_CHUNK_ROWS = 32


def _params(*semantics):
    return pltpu.CompilerParams(dimension_semantics=semantics, vmem_limit_bytes=VMEM_LIMIT_BYTES)


def _mix_kernel(x_ref, xp_ref, mu_ref, o_ref):
    s = pl.program_id(1)
    x = x_ref[...]
    before = jnp.where(s > 0, xp_ref[SUBLANES - 1:SUBLANES, :], 0.0)
    prev = pltpu.roll(x, 1, 0)
    row = lax.broadcasted_iota(jnp.int32, x.shape, 0)
    prev = jnp.where(row == 0, before, prev)
    xx = prev - x
    for c in range(N_SHIFT_MIX):
        o_ref[c] = (x + xx * mu_ref[c:c + 1, :]).astype(o_ref.dtype)


def _token_shift_mix(x, mu, ts=256):
    bsz, seq, dm = x.shape
    ts = min(ts, seq)
    nst = seq // ts
    return pl.pallas_call(
        _mix_kernel,
        grid=(bsz, nst),
        in_specs=[
            pl.BlockSpec((None, ts, dm), lambda b, s: (b, s, 0)),
            pl.BlockSpec((None, SUBLANES, dm), lambda b, s: (b, jnp.maximum(s * (ts // SUBLANES) - 1, 0), 0)),
            pl.BlockSpec((N_SHIFT_MIX, dm), lambda b, s: (0, 0)),
        ],
        out_specs=pl.BlockSpec((N_SHIFT_MIX, ts, dm), lambda b, s: (0, b * nst + s, 0)),
        out_shape=jax.ShapeDtypeStruct((N_SHIFT_MIX, bsz * seq, dm), BF16),
        compiler_params=_params("parallel", "parallel"),
        name="token_shift_mix",
    )(x, x, mu)


def _mm_kernel(a_ref, w_ref, o_ref, wb_ref):
    @pl.when(pl.program_id(2) == 0)
    def _():
        wb_ref[...] = w_ref[...].astype(BF16)

    o_ref[...] = jnp.dot(a_ref[...], wb_ref[...], preferred_element_type=F32).astype(o_ref.dtype)


def _batched_matmul(a, w, n_batch, out_dtype, tm=1024, tn=1024):
    _, m, kdim = a.shape
    tm = min(tm, m)
    n = w.shape[2]
    return pl.pallas_call(
        _mm_kernel,
        grid=(n_batch, n // tn, m // tm),
        in_specs=[
            pl.BlockSpec((None, tm, kdim), lambda c, j, i: (c, i, 0)),
            pl.BlockSpec((None, kdim, tn), lambda c, j, i: (c, 0, j)),
        ],
        out_specs=pl.BlockSpec((None, tm, tn), lambda c, j, i: (c, i, j)),
        out_shape=jax.ShapeDtypeStruct((n_batch, m, n), out_dtype),
        scratch_shapes=[pltpu.VMEM((kdim, tn), BF16)],
        compiler_params=_params("parallel", "parallel", "arbitrary"),
        name="batched_matmul",
    )(a, w)


def _lora_kernel(xw_ref, xa_ref, xg_ref, w1_ref, w2_ref, w0_ref, a1_ref, a2_ref, a0_ref,
                 g1_ref, g2_ref, decay_ref, a_ref, g_ref):
    def two_stage(x_ref, p_ref, q_ref, act):
        h = act(jnp.dot(x_ref[...], p_ref[...], preferred_element_type=F32))
        return jnp.dot(h.astype(BF16), q_ref[...], preferred_element_type=F32)

    z = w0_ref[...] + two_stage(xw_ref, w1_ref, w2_ref, jnp.tanh)
    decay_ref[...] = jnp.exp(-jnp.exp(-0.5) * jax.nn.sigmoid(z))
    a_ref[...] = jax.nn.sigmoid(a0_ref[...] + two_stage(xa_ref, a1_ref, a2_ref, lambda h: h))
    g_ref[...] = two_stage(xg_ref, g1_ref, g2_ref, jax.nn.sigmoid)


def _rwkv_lora(mixed, w1, w2, w0, a1, a2, a0, g1, g2, tm=256):
    _, t, dm = mixed.shape

    def full(arr):
        return pl.BlockSpec(arr.shape, lambda i: (0,) * arr.ndim)

    def mix_spec(c):
        return pl.BlockSpec((None, tm, dm), lambda i: (c, i, 0))

    out_spec = pl.BlockSpec((tm, dm), lambda i: (i, 0))
    out = jax.ShapeDtypeStruct((t, dm), F32)
    return pl.pallas_call(
        _lora_kernel,
        grid=(t // tm,),
        in_specs=[mix_spec(3), mix_spec(4), mix_spec(5), full(w1), full(w2), full(w0),
                  full(a1), full(a2), full(a0), full(g1), full(g2)],
        out_specs=[out_spec, out_spec, out_spec],
        out_shape=[out, out, out],
        compiler_params=_params("parallel"),
        name="rwkv_lora",
    )(mixed, mixed, mixed, w1, w2, w0, a1, a2, a0, g1, g2)


SCAN_V_GROUPS = 2
SCAN_PARTIALS = 2


SCAN_BATCH = 4
SCAN_CHUNK = LANES // SCAN_BATCH


def _swap_halves(a0, a1, a2, a3, low_half):
    r0, r1, r2, r3 = (pltpu.roll(a, 2 * SCAN_CHUNK, 1) for a in (a0, a1, a2, a3))
    return (jnp.where(low_half, a0, r2), jnp.where(low_half, a1, r3),
            jnp.where(low_half, r0, a2), jnp.where(low_half, r1, a3))


def _swap_chunks(p0, p1, p2, p3, even_chunk):
    return (jnp.where(even_chunk, p0, pltpu.roll(p1, SCAN_CHUNK, 1)),
            jnp.where(even_chunk, pltpu.roll(p0, 3 * SCAN_CHUNK, 1), p1),
            jnp.where(even_chunk, p2, pltpu.roll(p3, SCAN_CHUNK, 1)),
            jnp.where(even_chunk, pltpu.roll(p2, 3 * SCAN_CHUNK, 1), p3))


def _scan_kernel(r_ref, k_ref, v_ref, w_ref, a_ref, kkp_ref, kap_ref, rkp_ref, gng_ref, gnb_ref,
                 y_ref, st_ref, kk_s, wr_s, w_s, b_s, k2_s, v_s, y_s, row_s, *, steps):
    n = RWKV_HEAD_SIZE
    n_vg = n // SUBLANES
    lane = lax.broadcasted_iota(jnp.int32, (SUBLANES, LANES), 1)
    low_half = lane < 2 * SCAN_CHUNK
    even_chunk = (lane & SCAN_CHUNK) == 0

    @pl.when(pl.program_id(0) == 0)
    def _():
        st_ref[...] = jnp.zeros_like(st_ref)

    def tile(i):
        return slice(i * SUBLANES, (i + 1) * SUBLANES)

    def eight_steps(s8, carry):
        s0 = pl.multiple_of(s8 * SUBLANES, SUBLANES)
        ss = br = kr = bonus = None
        pairs = ((r_ref, wr_s), (k_ref, kk_s), (v_ref, v_s), (w_ref, w_s), (a_ref, b_s))
        for q in range(n // SCAN_BATCH):
            for src, dst in pairs:
                halves = _swap_halves(*(src[bi, pl.ds(s0, SUBLANES), q * LANES:(q + 1) * LANES]
                                        for bi in range(SCAN_BATCH)), low_half)
                for j, val in enumerate(halves):
                    dst[tile(q * SCAN_BATCH + j), :] = val
        for q in range(n // SCAN_BATCH):
            for _, dst in pairs:
                outs = _swap_chunks(*(dst[tile(q * SCAN_BATCH + j), :] for j in range(SCAN_BATCH)), even_chunk)
                for j, val in enumerate(outs):
                    dst[tile(q * SCAN_BATCH + j), :] = val
        for q in range(n // SCAN_BATCH):
            for j in range(SCAN_BATCH):
                rows = slice((q * SCAN_BATCH + j) * SUBLANES, (q * SCAN_BATCH + j + 1) * SUBLANES)
                r, k, w, a = wr_s[rows, :], kk_s[rows, :], w_s[rows, :], b_s[rows, :]
                kk0 = k * kkp_ref[rows, :]
                k2 = k * (1.0 + (a - 1.0) * kap_ref[rows, :])
                b0 = kk0 * a
                rk2 = r * k2
                kk_s[rows, :] = kk0
                wr_s[rows, :] = w * r
                b_s[rows, :] = b0
                k2_s[rows, :] = k2
                terms = (kk0 * kk0, b0 * r, rk2, rk2 * rkp_ref[rows, :])
                if ss is None:
                    ss, br, kr, bonus = terms
                else:
                    ss, br, kr, bonus = ss + terms[0], br + terms[1], kr + terms[2], bonus + terms[3]
        inv2 = 1.0 / jnp.maximum(ss, 1e-24)
        row_s[0] = inv2
        row_s[1] = br
        row_s[2] = kr
        row_s[3] = bonus

        def step(t, c):
            inv2_row = row_s[0, pl.ds(t, 1), :]
            br_row = row_s[1, pl.ds(t, 1), :]
            kr_row = row_s[2, pl.ds(t, 1), :]
            for g0 in range(0, n_vg, SCAN_V_GROUPS):
                groups = range(g0, g0 + SCAN_V_GROUPS)
                acc_sa = {g: [None] * SCAN_PARTIALS for g in groups}
                acc_y = {g: [None] * SCAN_PARTIALS for g in groups}
                for ki in range(n):
                    kk_row = kk_s[pl.ds(ki * SUBLANES + t, 1), :]
                    wr_row = wr_s[pl.ds(ki * SUBLANES + t, 1), :]
                    p = ki % SCAN_PARTIALS
                    for g in groups:
                        s = st_ref[ki, g * SUBLANES:(g + 1) * SUBLANES, :]
                        t_sa = s * kk_row
                        t_y = s * wr_row
                        acc_sa[g][p] = t_sa if acc_sa[g][p] is None else acc_sa[g][p] + t_sa
                        acc_y[g][p] = t_y if acc_y[g][p] is None else acc_y[g][p] + t_y
                sa = {}
                vv = {}
                for g in groups:
                    v_rows = pl.ds(g * SUBLANES * SUBLANES + t, SUBLANES, stride=SUBLANES)
                    sa[g] = -functools.reduce(lambda x, y: x + y, acc_sa[g]) * inv2_row
                    vv[g] = v_s[v_rows, :]
                    y_s[v_rows, :] = (functools.reduce(lambda x, y: x + y, acc_y[g])
                                      + sa[g] * br_row + vv[g] * kr_row)
                for ki in range(n):
                    w_row = w_s[pl.ds(ki * SUBLANES + t, 1), :]
                    b_row = b_s[pl.ds(ki * SUBLANES + t, 1), :]
                    k_row = k2_s[pl.ds(ki * SUBLANES + t, 1), :]
                    for g in groups:
                        s = st_ref[ki, g * SUBLANES:(g + 1) * SUBLANES, :]
                        st_ref[ki, g * SUBLANES:(g + 1) * SUBLANES, :] = (
                            s * w_row + sa[g] * b_row + vv[g] * k_row)
            return c

        lax.fori_loop(0, SUBLANES, step, 0)

        tot = None
        for vi in range(n):
            yv = y_s[vi * SUBLANES:(vi + 1) * SUBLANES, :]
            tot = yv if tot is None else tot + yv
        mean = tot * (1.0 / n)
        sq = None
        for vi in range(n):
            yc = y_s[vi * SUBLANES:(vi + 1) * SUBLANES, :] - mean
            sq = yc * yc if sq is None else sq + yc * yc
        rstd = lax.rsqrt(sq * (1.0 / n) + GN_EPS)
        bonus_v = row_s[3]
        for q in range(n // SCAN_BATCH):
            outs = []
            for j in range(SCAN_BATCH):
                rows = tile(q * SCAN_BATCH + j)
                outs.append((y_s[rows, :] - mean) * rstd * gng_ref[rows, :] + gnb_ref[rows, :]
                            + bonus_v * v_s[rows, :])
            for j, val in enumerate(_swap_halves(*outs, low_half)):
                y_s[tile(q * SCAN_BATCH + j), :] = val
        for q in range(n // SCAN_BATCH):
            outs = _swap_chunks(*(y_s[tile(q * SCAN_BATCH + j), :] for j in range(SCAN_BATCH)), even_chunk)
            for bi, val in enumerate(outs):
                y_ref[bi, pl.ds(s0, SUBLANES), q * LANES:(q + 1) * LANES] = val
        return carry

    lax.fori_loop(0, steps // SUBLANES, eight_steps, 0)


def _rwkv_scan(rkv, w, a, kkp, kap, rkp, gng, gnb, steps=32):
    bsz, seq, dm = w.shape
    n = RWKV_HEAD_SIZE
    assert bsz == SCAN_BATCH and dm == n * SCAN_CHUNK

    def rkv_spec(c):
        return pl.BlockSpec((None, bsz, steps, dm), lambda i: (c, 0, i, 0))

    seq_spec = pl.BlockSpec((bsz, steps, dm), lambda i: (0, i, 0))
    par_spec = pl.BlockSpec((n * SUBLANES, LANES), lambda i: (0, 0))
    rows = pltpu.VMEM((n * SUBLANES, LANES), F32)
    return pl.pallas_call(
        functools.partial(_scan_kernel, steps=steps),
        grid=(seq // steps,),
        in_specs=[rkv_spec(0), rkv_spec(1), rkv_spec(2), seq_spec, seq_spec] + [par_spec] * 5,
        out_specs=seq_spec,
        out_shape=jax.ShapeDtypeStruct((bsz, seq, dm), F32),
        scratch_shapes=[pltpu.VMEM((n, n, LANES), F32)] + [rows] * 7 + [pltpu.VMEM((4, SUBLANES, LANES), F32)],
        compiler_params=_params("arbitrary"),
        name="rwkv7_scan",
    )(rkv, rkv, rkv, w, a, kkp, kap, rkp, gng, gnb)


def _mm_ln_kernel(*refs, gated, emit_bf16):
    refs = list(refs)
    a_ref = refs.pop(0)
    gate_ref = refs.pop(0) if gated else None
    w_ref, x_ref, g_ref, b_ref, o_ref = refs[:5]
    ob_ref = refs[5] if emit_bf16 else None
    acc_ref = refs[-1]

    a = a_ref[...]
    if gated:
        a = (a * gate_ref[...]).astype(BF16)
    acc_ref[...] = jnp.dot(a, w_ref[...], preferred_element_type=F32)

    for c in range(acc_ref.shape[0] // LN_CHUNK_ROWS):
        rows = slice(c * LN_CHUNK_ROWS, (c + 1) * LN_CHUNK_ROWS)
        z = DEEPNORM_ALPHA * x_ref[rows, :] + acc_ref[rows, :]
        mu = jnp.mean(z, axis=-1, keepdims=True)
        zc = z - mu
        var = jnp.mean(zc * zc, axis=-1, keepdims=True)
        out = zc * lax.rsqrt(var + LN_EPS) * g_ref[...] + b_ref[...]
        o_ref[rows, :] = out
        if emit_bf16:
            ob_ref[rows, :] = out.astype(BF16)


def _matmul_residual_ln(a, w, x, g, b, *, gate=None, tm, emit_bf16=True):
    m, kdim = a.shape
    n = w.shape[1]
    row = lambda i: (i, 0)
    fixed = lambda i: (0, 0)
    in_specs = [pl.BlockSpec((tm, kdim), row)]
    args = [a]
    if gate is not None:
        in_specs.append(pl.BlockSpec((tm, kdim), row))
        args.append(gate)
    in_specs += [pl.BlockSpec((kdim, n), fixed, pipeline_mode=pl.Buffered(1)),
                 pl.BlockSpec((tm, n), row),
                 pl.BlockSpec((1, n), fixed),
                 pl.BlockSpec((1, n), fixed)]
    args += [w, x, g.reshape(1, n), b.reshape(1, n)]
    out_specs = [pl.BlockSpec((tm, n), row)]
    out_shape = [jax.ShapeDtypeStruct((m, n), F32)]
    if emit_bf16:
        out_specs.append(pl.BlockSpec((tm, n), row))
        out_shape.append(jax.ShapeDtypeStruct((m, n), BF16))
    outs = pl.pallas_call(
        functools.partial(_mm_ln_kernel, gated=gate is not None, emit_bf16=emit_bf16),
        grid=(m // tm,),
        in_specs=in_specs,
        out_specs=out_specs,
        out_shape=out_shape,
        scratch_shapes=[pltpu.VMEM((tm, n), F32)],
        compiler_params=_params("parallel"),
        name="matmul_residual_ln",
    )(*args)
    return outs if emit_bf16 else (outs[0], None)


FFN_COL_CHUNK = 256


def _ffn_up_kernel(x_ref, wg32_ref, wu32_ref, cw_ref, cb_ref, o_ref, tail_ref, wg_ref, wu_ref, *,
                   tiles_per_seq):
    i = pl.program_id(1)

    @pl.when(i == 0)
    def _():
        wg_ref[...] = wg32_ref[...].astype(BF16)
        wu_ref[...] = wu32_ref[...].astype(BF16)

    x = x_ref[...]
    tm = x.shape[0]
    seq_start = i % tiles_per_seq == 0
    row = lax.broadcasted_iota(jnp.int32, (SUBLANES, FFN_COL_CHUNK), 0)
    for c in range(o_ref.shape[1] // FFN_COL_CHUNK):
        cols = slice(c * FFN_COL_CHUNK, (c + 1) * FFN_COL_CHUNK)
        gate = jnp.dot(x, wg_ref[:, cols], preferred_element_type=F32)
        up = jnp.dot(x, wu_ref[:, cols], preferred_element_type=F32)
        tail = jnp.where(seq_start, 0.0, tail_ref[:, cols])
        tail_ref[:, cols] = gate[tm - SUBLANES:, :]
        r1 = pltpu.roll(gate, 1, 0)
        r2 = pltpu.roll(gate, 2, 0)
        h1 = jnp.where(row == 0, tail[SUBLANES - 1:, :], r1[:SUBLANES, :])
        h2 = jnp.where(row == 0, tail[SUBLANES - 2:SUBLANES - 1, :],
                       jnp.where(row == 1, tail[SUBLANES - 1:, :], r2[:SUBLANES, :]))
        g1 = jnp.concatenate([h1, r1[SUBLANES:, :]], axis=0)
        g2 = jnp.concatenate([h2, r2[SUBLANES:, :]], axis=0)
        acc = (cb_ref[:, cols] + g2 * cw_ref[0:1, cols] + g1 * cw_ref[1:2, cols] + gate * cw_ref[2:3, cols])
        o_ref[:, cols] = (acc * jax.nn.sigmoid(acc) * up).astype(o_ref.dtype)


def _ffn_up(xb, w_up_all, layer, conv_w, conv_b, seq, tm=1024, tn=512):
    t, dm = xb.shape
    dff = w_up_all.shape[2] // 2
    n_col = dff // tn
    w_up = w_up_all
    return pl.pallas_call(
        functools.partial(_ffn_up_kernel, tiles_per_seq=seq // tm),
        grid=(n_col, t // tm),
        in_specs=[
            pl.BlockSpec((tm, dm), lambda j, i: (i, 0)),
            pl.BlockSpec((None, dm, tn), lambda j, i: (layer, 0, j)),
            pl.BlockSpec((None, dm, tn), lambda j, i: (layer, 0, j + n_col)),
            pl.BlockSpec((CONV_WIDTH, tn), lambda j, i: (0, j)),
            pl.BlockSpec((1, tn), lambda j, i: (0, j)),
        ],
        out_specs=pl.BlockSpec((tm, tn), lambda j, i: (i, j)),
        out_shape=jax.ShapeDtypeStruct((t, dff), BF16),
        scratch_shapes=[pltpu.VMEM((SUBLANES, tn), F32), pltpu.VMEM((dm, tn), BF16), pltpu.VMEM((dm, tn), BF16)],
        compiler_params=_params("parallel", "arbitrary"),
        name="ffn_up_convglu",
    )(xb, w_up, w_up, conv_w, conv_b.reshape(1, dff))


ATTN_SPAN = 128
ATTN_TILE = 2048
ATTN_MERGE_ROWS = 256


def _block_attention(q, k, v, valid):
    s = lax.dot_general(q.astype(BF16), k.astype(BF16), (((1,), (1,)), ((), ())),
                        preferred_element_type=F32) * (ATTN_HEAD_DIM ** -0.5)
    s = jnp.where(valid, s, MASK_VALUE)
    m = jnp.max(s, axis=-1, keepdims=True)
    p = jnp.exp(s - m)
    l = jnp.sum(p, axis=-1, keepdims=True)
    o = jnp.dot(p.astype(BF16), v.astype(BF16), preferred_element_type=F32) * (1.0 / l)
    return o, m + jnp.log(l)


def _attn_kernel(q1, k1, v1, kp1, vp1, q4, k4, v4, kp4, vp4, q16, k16, v16, kp16, vp16,
                 o_ref, og, lg, *, tiles_per_seq):
    span, hd = ATTN_SPAN, ATTN_HEAD_DIM
    first_tile = pl.program_id(1) % tiles_per_seq == 0
    qi = lax.broadcasted_iota(jnp.int32, (span, 2 * span), 0)
    kj = lax.broadcasted_iota(jnp.int32, (span, 2 * span), 1)
    valid = (kj <= qi + span) & (kj >= qi)
    valid_edge = valid & (kj >= jnp.where(first_tile, span, 0))

    def emit(g, rows, q, k, v, mask):
        o, lse = _block_attention(q, k, v, mask)
        og[g, rows, :] = o
        lg[g, rows, :] = jnp.broadcast_to(lse, (span, hd))

    def cat(a, b):
        return jnp.concatenate([a, b], axis=0)

    n1 = ATTN_TILE // span
    rows = pl.ds(0, span)
    emit(0, rows, q1[rows, :], cat(kp1[...], k1[rows, :]), cat(vp1[...], v1[rows, :]), valid_edge)

    for j in range(1, n1):
        rows = pl.ds(j * span, span)
        keys = pl.ds((j - 1) * span, 2 * span)
        emit(0, rows, q1[rows, :], k1[keys, :], v1[keys, :], valid)

    d4 = 4
    nb4 = ATTN_TILE // (span * d4)
    for r in range(d4):
        rows = pl.ds(r, span, stride=d4)
        emit(1, rows, q4[rows, :], cat(kp4[rows, :], k4[rows, :]), cat(vp4[rows, :], v4[rows, :]), valid_edge)
        for j in range(1, nb4):
            rows = pl.ds(j * span * d4 + r, span, stride=d4)
            keys = pl.ds((j - 1) * span * d4 + r, 2 * span, stride=d4)
            emit(1, rows, q4[rows, :], k4[keys, :], v4[keys, :], valid)

    d16 = 16
    for r in range(d16):
        rows = pl.ds(r, span, stride=d16)
        emit(2, rows, q16[rows, :], cat(kp16[rows, :], k16[rows, :]), cat(vp16[rows, :], v16[rows, :]), valid_edge)

    def merge(c, carry):
        rows = pl.ds(pl.multiple_of(c * ATTN_MERGE_ROWS, ATTN_MERGE_ROWS), ATTN_MERGE_ROWS)
        l0, l1, l2 = lg[0, rows, :], lg[1, rows, :], lg[2, rows, :]
        m = jnp.maximum(jnp.maximum(l0, l1), l2)
        e0, e1, e2 = jnp.exp(l0 - m), jnp.exp(l1 - m), jnp.exp(l2 - m)
        num = e0 * og[0, rows, :] + e1 * og[1, rows, :] + e2 * og[2, rows, :]
        o_ref[rows, :] = (num / (e0 + e1 + e2)).astype(o_ref.dtype)
        return carry

    lax.fori_loop(0, ATTN_TILE // ATTN_MERGE_ROWS, merge, 0)


def _dilated_attention(qkv, seq):
    t, width = qkv.shape
    hd, span, tile = ATTN_HEAD_DIM, ATTN_SPAN, ATTN_TILE
    heads = width // (3 * len(DILATED_GROUPS) * hd)
    tiles_per_seq = seq // tile

    def col(group, which):
        return lambda h: (group * 3 + which) * heads + h

    def cur(group, which):
        return pl.BlockSpec((tile, hd), lambda h, i: (i, col(group, which)(h)))

    def prev(group, which, rows):
        per = tile // rows
        return pl.BlockSpec((rows, hd), lambda h, i: (jnp.maximum(i * per - 1, 0), col(group, which)(h)))

    in_specs = []
    for group, (window, dilation) in enumerate(DILATED_GROUPS):
        assert window // dilation == span and tile % window == 0
        in_specs += [cur(group, 0), cur(group, 1), cur(group, 2), prev(group, 1, window), prev(group, 2, window)]
    return pl.pallas_call(
        functools.partial(_attn_kernel, tiles_per_seq=tiles_per_seq),
        grid=(heads, t // tile),
        in_specs=in_specs,
        out_specs=pl.BlockSpec((tile, hd), lambda h, i: (i, h)),
        out_shape=jax.ShapeDtypeStruct((t, heads * hd), BF16),
        scratch_shapes=[pltpu.VMEM((len(DILATED_GROUPS), tile, hd), F32)] * 2,
        compiler_params=_params("parallel", "parallel"),
        name="dilated_attention",
    )(*([qkv] * len(in_specs)))


def _pad_rank(p, q):
    r = p.shape[1]
    if r % LORA_PAD:
        extra = LORA_PAD - r % LORA_PAD
        p = jnp.pad(p, ((0, 0), (0, extra)))
        q = jnp.pad(q, ((0, extra), (0, 0)))
    return p.astype(BF16), q.astype(BF16)


def _rwkv_layer(x, mu, w_rkv, w0, w1, w2, a0, a1, a2, g1, g2, k_k, k_a, r_k, gn_g, gn_b, w_out, ln_g, ln_b):
    bsz, seq, dm = x.shape
    n = RWKV_HEAD_SIZE
    heads = dm // n
    t = bsz * seq
    def cols(wt):
        lead = wt.shape[:-1]
        return wt.reshape(*lead, heads, n).swapaxes(-1, -2).reshape(*lead, dm)

    mixed = _token_shift_mix(x, mu[jnp.array([0, 2, 3, 1, 4, 5])])
    rkv = _batched_matmul(mixed, cols(w_rkv).astype(BF16), 3, F32)
    w1b, w2b = _pad_rank(w1, cols(w2))
    a1b, a2b = _pad_rank(a1, cols(a2))
    g1b, g2b = _pad_rank(g1, cols(g2))
    decay, a, g = _rwkv_lora(mixed, w1b, w2b, cols(w0).reshape(1, dm), a1b, a2b, cols(a0).reshape(1, dm),
                             g1b, g2b)

    def lanes_param(p):
        return jnp.repeat(jnp.tile(p.reshape(heads, n).T, (1, bsz)), SUBLANES, axis=0)

    def seq_major(z):
        return z.reshape(bsz, seq, dm)

    y = _rwkv_scan(rkv.reshape(3, bsz, seq, dm), seq_major(decay), seq_major(a),
                   lanes_param(k_k), lanes_param(k_a), lanes_param(r_k), lanes_param(gn_g), lanes_param(gn_b))
    w_out_rows = w_out.reshape(heads, n, dm).swapaxes(0, 1).reshape(dm, dm)
    return _matmul_residual_ln(y.reshape(t, dm), w_out_rows.astype(BF16), x.reshape(t, dm), ln_g, ln_b,
                               gate=g, tm=256)


def _attn_layer(x32, xb, w_in, w_out, ln_g, ln_b, bsz, seq):
    t, dm = x32.shape
    qkv = _batched_matmul(xb[None], w_in[None], 1, F32)[0]
    merged = _dilated_attention(qkv, seq)
    return _matmul_residual_ln(merged, w_out.astype(BF16), x32, ln_g, ln_b, tm=512)


def _ffn_layer(x32, xb, w_up_all, layer, conv_w, conv_b, w_down, ln_g, ln_b, seq, emit_bf16):
    act = _ffn_up(xb, w_up_all, layer, conv_w, conv_b, seq)
    return _matmul_residual_ln(act, w_down.astype(BF16), x32, ln_g, ln_b, tm=256, emit_bf16=emit_bf16)


def kernel(x, rwkv_mu, rwkv_w_rkv, rwkv_w0, rwkv_w1, rwkv_w2, rwkv_a0, rwkv_a1, rwkv_a2, rwkv_g1, rwkv_g2, rwkv_k_k, rwkv_k_a, rwkv_r_k, rwkv_gn_g, rwkv_gn_b, rwkv_w_out, attn_w_in, attn_w_out, ffn_w_up, ffn_conv_w, ffn_conv_b, ffn_w_down, ln_mix_g, ln_mix_b, ln_ffn_g, ln_ffn_b):
    bsz, seq, dm = x.shape
    x32, xb = _rwkv_layer(x, rwkv_mu[0], rwkv_w_rkv[0], rwkv_w0[0], rwkv_w1[0], rwkv_w2[0], rwkv_a0[0],
                          rwkv_a1[0], rwkv_a2[0], rwkv_g1[0], rwkv_g2[0], rwkv_k_k[0], rwkv_k_a[0],
                          rwkv_r_k[0], rwkv_gn_g[0], rwkv_gn_b[0], rwkv_w_out[0], ln_mix_g[0], ln_mix_b[0])
    x32, xb = _ffn_layer(x32, xb, ffn_w_up, 0, ffn_conv_w[0], ffn_conv_b[0], ffn_w_down[0],
                         ln_ffn_g[0], ln_ffn_b[0], seq, True)
    x32, xb = _attn_layer(x32, xb, attn_w_in[0], attn_w_out[0], ln_mix_g[1], ln_mix_b[1], bsz, seq)
    x32, _ = _ffn_layer(x32, xb, ffn_w_up, 1, ffn_conv_w[1], ffn_conv_b[1], ffn_w_down[1],
                        ln_ffn_g[1], ln_ffn_b[1], seq, False)
    return x32.reshape(bsz, seq, dm)
```

```python
import functools

import jax
import jax.numpy as jnp
from jax import lax
from jax.experimental import pallas as pl
from jax.experimental.pallas import tpu as pltpu

F32 = jnp.float32
BF16 = jnp.bfloat16

V7X_VMEM_BYTES = 64 * 1024 * 1024
VMEM_LIMIT_BYTES = V7X_VMEM_BYTES - 8 * 1024 * 1024
SUBLANES = 8
LANES = 128

RWKV_HEAD_SIZE = 64
N_SHIFT_MIX = 6
GN_EPS = 64e-5
ATTN_HEAD_DIM = 128
DILATED_GROUPS = ((128, 1), (512, 4), (2048, 16))
MASK_VALUE = -1e30
CONV_WIDTH = 3
LN_EPS = 1e-5
DEPTH = 2
DEEPNORM_ALPHA = (2.0 * DEPTH) ** 0.25
LORA_PAD = 128
LN_CHUNK_ROWS = 32


def _params(*semantics):
    return pltpu.CompilerParams(dimension_semantics=semantics, vmem_limit_bytes=VMEM_LIMIT_BYTES)


def _mix_kernel(x_ref, xp_ref, mu_ref, o_ref):
    s = pl.program_id(1)
    x = x_ref[...]
    before = jnp.where(s > 0, xp_ref[SUBLANES - 1:SUBLANES, :], 0.0)
    prev = pltpu.roll(x, 1, 0)
    row = lax.broadcasted_iota(jnp.int32, x.shape, 0)
    prev = jnp.where(row == 0, before, prev)
    xx = prev - x
    for c in range(N_SHIFT_MIX):
        o_ref[c] = (x + xx * mu_ref[c:c + 1, :]).astype(o_ref.dtype)


def _token_shift_mix(x, mu, ts=256):
    bsz, seq, dm = x.shape
    ts = min(ts, seq)
    nst = seq // ts
    return pl.pallas_call(
        _mix_kernel,
        grid=(bsz, nst),
        in_specs=[
            pl.BlockSpec((None, ts, dm), lambda b, s: (b, s, 0)),
            pl.BlockSpec((None, SUBLANES, dm), lambda b, s: (b, jnp.maximum(s * (ts // SUBLANES) - 1, 0), 0)),
            pl.BlockSpec((N_SHIFT_MIX, dm), lambda b, s: (0, 0)),
        ],
        out_specs=pl.BlockSpec((N_SHIFT_MIX, ts, dm), lambda b, s: (0, b * nst + s, 0)),
        out_shape=jax.ShapeDtypeStruct((N_SHIFT_MIX, bsz * seq, dm), BF16),
        compiler_params=_params("parallel", "parallel"),
        name="token_shift_mix",
    )(x, x, mu)


def _mm_kernel(a_ref, w_ref, o_ref, wb_ref):
    @pl.when(pl.program_id(2) == 0)
    def _():
        wb_ref[...] = w_ref[...].astype(BF16)

    res = jnp.dot(a_ref[...], wb_ref[...], preferred_element_type=F32).astype(o_ref.dtype)
    if len(o_ref.shape) == 2:
        o_ref[...] = res
    else:
        for q in range(o_ref.shape[0]):
            o_ref[q] = res[:, q * LANES:(q + 1) * LANES]


def _batched_matmul(a, w, n_batch, out_dtype, tm=1024, tn=1024, rows_per_seq=None):
    _, m, kdim = a.shape
    tm = min(tm, m)
    n = w.shape[2]
    if rows_per_seq is None:
        out_spec = pl.BlockSpec((None, tm, tn), lambda c, j, i: (c, i, j))
        out_shape = jax.ShapeDtypeStruct((n_batch, m, n), out_dtype)
    else:
        tm = min(tm, rows_per_seq)
        per = rows_per_seq // tm
        out_spec = pl.BlockSpec((None, None, tn // LANES, tm, LANES), lambda c, j, i: (c, i // per, j, i % per, 0))
        out_shape = jax.ShapeDtypeStruct((n_batch, m // rows_per_seq, n // LANES, rows_per_seq, LANES), out_dtype)
    return pl.pallas_call(
        _mm_kernel,
        grid=(n_batch, n // tn, m // tm),
        in_specs=[
            pl.BlockSpec((None, tm, kdim), lambda c, j, i: (c, i, 0)),
            pl.BlockSpec((None, kdim, tn), lambda c, j, i: (c, 0, j)),
        ],
        out_specs=out_spec,
        out_shape=out_shape,
        scratch_shapes=[pltpu.VMEM((kdim, tn), BF16)],
        compiler_params=_params("parallel", "parallel", "arbitrary"),
        name="batched_matmul",
    )(a, w)


def _lora_kernel(xw_ref, xa_ref, xg_ref, w1_ref, w2_ref, w0_ref, a1_ref, a2_ref, a0_ref,
                 g1_ref, g2_ref, decay_ref, a_ref, g_ref):
    def two_stage(x_ref, p_ref, q_ref, act):
        h = act(jnp.dot(x_ref[...], p_ref[...], preferred_element_type=F32))
        return jnp.dot(h.astype(BF16), q_ref[...], preferred_element_type=F32)

    def split_lanes(o_ref, val):
        for q in range(o_ref.shape[0]):
            o_ref[q] = val[:, q * LANES:(q + 1) * LANES]

    z = w0_ref[...] + two_stage(xw_ref, w1_ref, w2_ref, jnp.tanh)
    split_lanes(decay_ref, jnp.exp(-jnp.exp(-0.5) * jax.nn.sigmoid(z)))
    split_lanes(a_ref, jax.nn.sigmoid(a0_ref[...] + two_stage(xa_ref, a1_ref, a2_ref, lambda h: h)))
    g_ref[...] = two_stage(xg_ref, g1_ref, g2_ref, jax.nn.sigmoid)


def _rwkv_lora(mixed, w1, w2, w0, a1, a2, a0, g1, g2, seq, tm=256):
    _, t, dm = mixed.shape
    tm = min(tm, seq)
    per = seq // tm

    def full(arr):
        return pl.BlockSpec(arr.shape, lambda i: (0,) * arr.ndim)

    def mix_spec(c):
        return pl.BlockSpec((None, tm, dm), lambda i: (c, i, 0))

    out_spec = pl.BlockSpec((tm, dm), lambda i: (i, 0))
    out = jax.ShapeDtypeStruct((t, dm), F32)
    split_spec = pl.BlockSpec((None, dm // LANES, tm, LANES), lambda i: (i // per, 0, i % per, 0))
    split = jax.ShapeDtypeStruct((t // seq, dm // LANES, seq, LANES), F32)
    return pl.pallas_call(
        _lora_kernel,
        grid=(t // tm,),
        in_specs=[mix_spec(3), mix_spec(4), mix_spec(5), full(w1), full(w2), full(w0),
                  full(a1), full(a2), full(a0), full(g1), full(g2)],
        out_specs=[split_spec, split_spec, out_spec],
        out_shape=[split, split, out],
        compiler_params=_params("parallel"),
        name="rwkv_lora",
    )(mixed, mixed, mixed, w1, w2, w0, a1, a2, a0, g1, g2)


SCAN_V_GROUPS = 2
SCAN_PARTIALS = 2


SCAN_BATCH = 4
SCAN_CHUNK = LANES // SCAN_BATCH


def _swap_halves(a0, a1, a2, a3, low_half):
    r0, r1, r2, r3 = (pltpu.roll(a, 2 * SCAN_CHUNK, 1) for a in (a0, a1, a2, a3))
    return (jnp.where(low_half, a0, r2), jnp.where(low_half, a1, r3),
            jnp.where(low_half, r0, a2), jnp.where(low_half, r1, a3))


def _swap_chunks(p0, p1, p2, p3, even_chunk):
    return (jnp.where(even_chunk, p0, pltpu.roll(p1, SCAN_CHUNK, 1)),
            jnp.where(even_chunk, pltpu.roll(p0, 3 * SCAN_CHUNK, 1), p1),
            jnp.where(even_chunk, p2, pltpu.roll(p3, SCAN_CHUNK, 1)),
            jnp.where(even_chunk, pltpu.roll(p2, 3 * SCAN_CHUNK, 1), p3))


Q_KK, Q_WR, Q_W, Q_B, Q_K2, Q_V = range(6)
R_INV2, R_BR, R_KR, R_BONUS = range(4)
SCAN_UNIT_Q = 2
SCAN_Q = RWKV_HEAD_SIZE // SCAN_BATCH


def _scan_pipelined_kernel(r_ref, k_ref, v_ref, w_ref, a_ref, kkp_ref, kap_ref, rkp_ref, gng_ref, gnb_ref,
                           y_ref, st_ref, buf_a, buf_b, rows_a, rows_b, y_s, yn_s, *, steps):
    n = RWKV_HEAD_SIZE
    n_vg = n // SUBLANES
    n_groups = steps // SUBLANES
    lane = lax.broadcasted_iota(jnp.int32, (SUBLANES, LANES), 1)
    low_half = lane < 2 * SCAN_CHUNK
    even_chunk = (lane & SCAN_CHUNK) == 0

    @pl.when(pl.program_id(0) == 0)
    def _():
        st_ref[...] = jnp.zeros_like(st_ref)
        yn_s[...] = jnp.zeros_like(yn_s)

    def tile_rows(i):
        if isinstance(i, int):
            return pl.ds(i * SUBLANES, SUBLANES)
        return pl.ds(pl.multiple_of(i * SUBLANES, SUBLANES), SUBLANES)

    def regroup(s0, qs, dst):
        srcs = (r_ref, k_ref, v_ref, w_ref, a_ref)
        halves = [[_swap_halves(*(src[bi, q, pl.ds(s0, SUBLANES), :] for bi in range(SCAN_BATCH)), low_half)
                   for src in srcs] for q in qs]
        sums = None
        for q, per_src in zip(qs, halves):
            rq, kq, vq, wq, aq = (_swap_chunks(*h, even_chunk) for h in per_src)
            for j in range(SCAN_BATCH):
                rows = tile_rows(q * SCAN_BATCH + j)
                r, k, v, w, a = rq[j], kq[j], vq[j], wq[j], aq[j]
                kk0 = k * kkp_ref[rows, :]
                k2 = k * (1.0 + (a - 1.0) * kap_ref[rows, :])
                b0 = kk0 * a
                rk2 = r * k2
                dst[Q_KK, rows, :] = kk0
                dst[Q_WR, rows, :] = w * r
                dst[Q_W, rows, :] = w
                dst[Q_B, rows, :] = b0
                dst[Q_K2, rows, :] = k2
                dst[Q_V, rows, :] = v
                terms = (kk0 * kk0, b0 * r, rk2, rk2 * rkp_ref[rows, :])
                sums = terms if sums is None else tuple(x + y for x, y in zip(sums, terms))
        return sums

    def finish_rows(sums, rows_ref):
        ss, br, kr, bonus = sums
        rows_ref[R_INV2] = 1.0 / jnp.maximum(ss, 1e-24)
        rows_ref[R_BR] = br
        rows_ref[R_KR] = kr
        rows_ref[R_BONUS] = bonus

    def unregroup(s0, qs):
        halves = [_swap_halves(*(yn_s[tile_rows(q * SCAN_BATCH + j), :] for j in range(SCAN_BATCH)), low_half)
                  for q in qs]
        for q, h in zip(qs, halves):
            for bi, val in enumerate(_swap_chunks(*h, even_chunk)):
                y_ref[bi, q, pl.ds(s0, SUBLANES), :] = val

    def step(t, cur, cur_rows):
        inv2_row = cur_rows[R_INV2, pl.ds(t, 1), :]
        br_row = cur_rows[R_BR, pl.ds(t, 1), :]
        kr_row = cur_rows[R_KR, pl.ds(t, 1), :]
        for g0 in range(0, n_vg, SCAN_V_GROUPS):
            groups = range(g0, g0 + SCAN_V_GROUPS)
            acc_sa = {g: [None] * SCAN_PARTIALS for g in groups}
            acc_y = {g: [None] * SCAN_PARTIALS for g in groups}
            for ki in range(n):
                kk_row = cur[Q_KK, pl.ds(ki * SUBLANES + t, 1), :]
                wr_row = cur[Q_WR, pl.ds(ki * SUBLANES + t, 1), :]
                p = ki % SCAN_PARTIALS
                for g in groups:
                    s = st_ref[ki, g * SUBLANES:(g + 1) * SUBLANES, :]
                    t_sa = s * kk_row
                    t_y = s * wr_row
                    acc_sa[g][p] = t_sa if acc_sa[g][p] is None else acc_sa[g][p] + t_sa
                    acc_y[g][p] = t_y if acc_y[g][p] is None else acc_y[g][p] + t_y
            sa = {}
            vv = {}
            for g in groups:
                v_rows = pl.ds(g * SUBLANES * SUBLANES + t, SUBLANES, stride=SUBLANES)
                sa[g] = -functools.reduce(lambda x, y: x + y, acc_sa[g]) * inv2_row
                vv[g] = cur[Q_V, v_rows, :]
                y_s[v_rows, :] = (functools.reduce(lambda x, y: x + y, acc_y[g])
                                  + sa[g] * br_row + vv[g] * kr_row)
            for ki in range(n):
                w_row = cur[Q_W, pl.ds(ki * SUBLANES + t, 1), :]
                b_row = cur[Q_B, pl.ds(ki * SUBLANES + t, 1), :]
                k_row = cur[Q_K2, pl.ds(ki * SUBLANES + t, 1), :]
                for g in groups:
                    s = st_ref[ki, g * SUBLANES:(g + 1) * SUBLANES, :]
                    st_ref[ki, g * SUBLANES:(g + 1) * SUBLANES, :] = s * w_row + sa[g] * b_row + vv[g] * k_row

    def normalise(cur, cur_rows):
        tot = None
        for vi in range(n):
            yv = y_s[vi * SUBLANES:(vi + 1) * SUBLANES, :]
            tot = yv if tot is None else tot + yv
        mean = tot * (1.0 / n)
        sq = None
        for vi in range(n):
            yc = y_s[vi * SUBLANES:(vi + 1) * SUBLANES, :] - mean
            sq = yc * yc if sq is None else sq + yc * yc
        rstd = lax.rsqrt(sq * (1.0 / n) + GN_EPS)
        bonus_v = cur_rows[R_BONUS]
        for vi in range(n):
            rows = slice(vi * SUBLANES, (vi + 1) * SUBLANES)
            yn_s[rows, :] = ((y_s[rows, :] - mean) * rstd * gng_ref[rows, :] + gnb_ref[rows, :]
                             + bonus_v * cur[Q_V, rows, :])

    def run_group(g, cur, cur_rows, nxt, nxt_rows):
        s0 = pl.multiple_of(g * SUBLANES, SUBLANES)
        s_next = pl.multiple_of(jnp.minimum(s0 + SUBLANES, steps - SUBLANES), SUBLANES)
        s_prev = pl.multiple_of(jnp.maximum(s0 - SUBLANES, 0), SUBLANES)

        def body(t, sums):
            step(t, cur, cur_rows)
            qs = [SCAN_UNIT_Q * t + u for u in range(SCAN_UNIT_Q)]
            part = regroup(s_next, qs, nxt)
            unregroup(s_prev, qs)
            return tuple(x + y for x, y in zip(sums, part))

        zero = jnp.zeros((SUBLANES, LANES), F32)
        sums = lax.fori_loop(0, SUBLANES, body, (zero, zero, zero, zero))
        finish_rows(sums, nxt_rows)
        normalise(cur, cur_rows)

    half = SCAN_Q // 2
    first = regroup(0, list(range(half)), buf_a)
    second = regroup(0, list(range(half, SCAN_Q)), buf_a)
    finish_rows(tuple(x + y for x, y in zip(first, second)), rows_a)

    def two_groups(i, carry):
        run_group(2 * i, buf_a, rows_a, buf_b, rows_b)
        run_group(2 * i + 1, buf_b, rows_b, buf_a, rows_a)
        return carry

    lax.fori_loop(0, n_groups // 2, two_groups, 0)

    unregroup(steps - SUBLANES, list(range(half)))
    unregroup(steps - SUBLANES, list(range(half, SCAN_Q)))


def _rwkv_scan_pipelined(rkv, w, a, kkp, kap, rkp, gng, gnb, steps=64):
    bsz, nq, seq, lanes = w.shape
    n = RWKV_HEAD_SIZE
    assert bsz == SCAN_BATCH and nq == SCAN_Q and lanes == LANES and (steps // SUBLANES) % 2 == 0

    def rkv_spec(c):
        return pl.BlockSpec((None, bsz, nq, steps, lanes), lambda i: (c, 0, 0, i, 0))

    seq_spec = pl.BlockSpec((bsz, nq, steps, lanes), lambda i: (0, 0, i, 0))
    par_spec = pl.BlockSpec((n * SUBLANES, LANES), lambda i: (0, 0))
    rows = pltpu.VMEM((n * SUBLANES, LANES), F32)
    group_buf = pltpu.VMEM((6, n * SUBLANES, LANES), F32)
    group_rows = pltpu.VMEM((4, SUBLANES, LANES), F32)
    return pl.pallas_call(
        functools.partial(_scan_pipelined_kernel, steps=steps),
        grid=(seq // steps,),
        in_specs=[rkv_spec(0), rkv_spec(1), rkv_spec(2), seq_spec, seq_spec] + [par_spec] * 5,
        out_specs=seq_spec,
        out_shape=jax.ShapeDtypeStruct((bsz, nq, seq, lanes), F32),
        scratch_shapes=[pltpu.VMEM((n, n, LANES), F32), group_buf, group_buf, group_rows, group_rows, rows, rows],
        compiler_params=_params("arbitrary"),
        name="rwkv7_scan",
    )(rkv, rkv, rkv, w, a, kkp, kap, rkp, gng, gnb)


def _mm_ln_kernel(*refs, gated, emit_bf16):
    refs = list(refs)
    a_ref = refs.pop(0)
    gate_ref = refs.pop(0) if gated else None
    w_ref, x_ref, g_ref, b_ref, o_ref = refs[:5]
    ob_ref = refs[5] if emit_bf16 else None
    acc_ref = refs[-1]

    if gated:
        a = jnp.concatenate([a_ref[q] for q in range(a_ref.shape[0])], axis=1)
        a = (a * gate_ref[...]).astype(BF16)
    else:
        a = a_ref[...]
    acc_ref[...] = jnp.dot(a, w_ref[...], preferred_element_type=F32)

    for c in range(acc_ref.shape[0] // LN_CHUNK_ROWS):
        rows = slice(c * LN_CHUNK_ROWS, (c + 1) * LN_CHUNK_ROWS)
        z = DEEPNORM_ALPHA * x_ref[rows, :] + acc_ref[rows, :]
        mu = jnp.mean(z, axis=-1, keepdims=True)
        zc = z - mu
        var = jnp.mean(zc * zc, axis=-1, keepdims=True)
        out = zc * lax.rsqrt(var + LN_EPS) * g_ref[...] + b_ref[...]
        o_ref[rows, :] = out
        if emit_bf16:
            ob_ref[rows, :] = out.astype(BF16)


def _matmul_residual_ln(a, w, x, g, b, *, gate=None, tm, emit_bf16=True):
    kdim, n = w.shape
    m = x.shape[0]
    row = lambda i: (i, 0)
    fixed = lambda i: (0, 0)
    args = [a]
    if gate is None:
        in_specs = [pl.BlockSpec((tm, kdim), row)]
    else:
        per = a.shape[2] // tm
        in_specs = [pl.BlockSpec((None, kdim // LANES, tm, LANES), lambda i: (i // per, 0, i % per, 0)),
                    pl.BlockSpec((tm, kdim), row)]
        args.append(gate)
    in_specs += [pl.BlockSpec((kdim, n), fixed, pipeline_mode=pl.Buffered(1)),
                 pl.BlockSpec((tm, n), row),
                 pl.BlockSpec((1, n), fixed),
                 pl.BlockSpec((1, n), fixed)]
    args += [w, x, g.reshape(1, n), b.reshape(1, n)]
    out_specs = [pl.BlockSpec((tm, n), row)]
    out_shape = [jax.ShapeDtypeStruct((m, n), F32)]
    if emit_bf16:
        out_specs.append(pl.BlockSpec((tm, n), row))
        out_shape.append(jax.ShapeDtypeStruct((m, n), BF16))
    outs = pl.pallas_call(
        functools.partial(_mm_ln_kernel, gated=gate is not None, emit_bf16=emit_bf16),
        grid=(m // tm,),
        in_specs=in_specs,
        out_specs=out_specs,
        out_shape=out_shape,
        scratch_shapes=[pltpu.VMEM((tm, n), F32)],
        compiler_params=_params("parallel"),
        name="matmul_residual_ln",
    )(*args)
    return outs if emit_bf16 else (outs[0], None)


FFN_COL_CHUNK = 256


def _ffn_up_kernel(x_ref, wg32_ref, wu32_ref, cw_ref, cb_ref, o_ref, tail_ref, wg_ref, wu_ref, *,
                   tiles_per_seq):
    i = pl.program_id(1)

    @pl.when(i == 0)
    def _():
        wg_ref[...] = wg32_ref[...].astype(BF16)
        wu_ref[...] = wu32_ref[...].astype(BF16)

    x = x_ref[...]
    tm = x.shape[0]
    seq_start = i % tiles_per_seq == 0
    row = lax.broadcasted_iota(jnp.int32, (SUBLANES, FFN_COL_CHUNK), 0)
    for c in range(o_ref.shape[1] // FFN_COL_CHUNK):
        cols = slice(c * FFN_COL_CHUNK, (c + 1) * FFN_COL_CHUNK)
        gate = jnp.dot(x, wg_ref[:, cols], preferred_element_type=F32)
        up = jnp.dot(x, wu_ref[:, cols], preferred_element_type=F32)
        tail = jnp.where(seq_start, 0.0, tail_ref[:, cols])
        tail_ref[:, cols] = gate[tm - SUBLANES:, :]
        r1 = pltpu.roll(gate, 1, 0)
        r2 = pltpu.roll(gate, 2, 0)
        h1 = jnp.where(row == 0, tail[SUBLANES - 1:, :], r1[:SUBLANES, :])
        h2 = jnp.where(row == 0, tail[SUBLANES - 2:SUBLANES - 1, :],
                       jnp.where(row == 1, tail[SUBLANES - 1:, :], r2[:SUBLANES, :]))
        g1 = jnp.concatenate([h1, r1[SUBLANES:, :]], axis=0)
        g2 = jnp.concatenate([h2, r2[SUBLANES:, :]], axis=0)
        acc = (cb_ref[:, cols] + g2 * cw_ref[0:1, cols] + g1 * cw_ref[1:2, cols] + gate * cw_ref[2:3, cols])
        o_ref[:, cols] = (acc * jax.nn.sigmoid(acc) * up).astype(o_ref.dtype)


def _ffn_up(xb, w_up_all, layer, conv_w, conv_b, seq, tm=1024, tn=512):
    t, dm = xb.shape
    dff = w_up_all.shape[2] // 2
    n_col = dff // tn
    w_up = w_up_all
    return pl.pallas_call(
        functools.partial(_ffn_up_kernel, tiles_per_seq=seq // tm),
        grid=(n_col, t // tm),
        in_specs=[
            pl.BlockSpec((tm, dm), lambda j, i: (i, 0)),
            pl.BlockSpec((None, dm, tn), lambda j, i: (layer, 0, j)),
            pl.BlockSpec((None, dm, tn), lambda j, i: (layer, 0, j + n_col)),
            pl.BlockSpec((CONV_WIDTH, tn), lambda j, i: (0, j)),
            pl.BlockSpec((1, tn), lambda j, i: (0, j)),
        ],
        out_specs=pl.BlockSpec((tm, tn), lambda j, i: (i, j)),
        out_shape=jax.ShapeDtypeStruct((t, dff), BF16),
        scratch_shapes=[pltpu.VMEM((SUBLANES, tn), F32), pltpu.VMEM((dm, tn), BF16), pltpu.VMEM((dm, tn), BF16)],
        compiler_params=_params("parallel", "arbitrary"),
        name="ffn_up_convglu",
    )(xb, w_up, w_up, conv_w, conv_b.reshape(1, dff))


ATTN_SPAN = 128
ATTN_TILE = 2048
ATTN_MERGE_ROWS = 256


def _block_attention(q, k, v, valid):
    s = lax.dot_general(q.astype(BF16), k.astype(BF16), (((1,), (1,)), ((), ())),
                        preferred_element_type=F32) * (ATTN_HEAD_DIM ** -0.5)
    s = jnp.where(valid, s, MASK_VALUE)
    m = jnp.max(s, axis=-1, keepdims=True)
    p = jnp.exp(s - m)
    l = jnp.sum(p, axis=-1, keepdims=True)
    o = jnp.dot(p.astype(BF16), v.astype(BF16), preferred_element_type=F32) * (1.0 / l)
    return o, m + jnp.log(l)


def _attn_kernel(q1, k1, v1, kp1, vp1, q4, k4, v4, kp4, vp4, q16, k16, v16, kp16, vp16,
                 o_ref, og, lg, *, tiles_per_seq):
    span, hd = ATTN_SPAN, ATTN_HEAD_DIM
    first_tile = pl.program_id(1) % tiles_per_seq == 0
    qi = lax.broadcasted_iota(jnp.int32, (span, 2 * span), 0)
    kj = lax.broadcasted_iota(jnp.int32, (span, 2 * span), 1)
    valid = (kj <= qi + span) & (kj >= qi)
    valid_edge = valid & (kj >= jnp.where(first_tile, span, 0))

    def emit(g, rows, q, k, v, mask):
        o, lse = _block_attention(q, k, v, mask)
        og[g, rows, :] = o
        lg[g, rows, :] = jnp.broadcast_to(lse, (span, hd))

    def cat(a, b):
        return jnp.concatenate([a, b], axis=0)

    n1 = ATTN_TILE // span
    rows = pl.ds(0, span)
    emit(0, rows, q1[rows, :], cat(kp1[...], k1[rows, :]), cat(vp1[...], v1[rows, :]), valid_edge)

    for j in range(1, n1):
        rows = pl.ds(j * span, span)
        keys = pl.ds((j - 1) * span, 2 * span)
        emit(0, rows, q1[rows, :], k1[keys, :], v1[keys, :], valid)

    d4 = 4
    nb4 = ATTN_TILE // (span * d4)
    for r in range(d4):
        rows = pl.ds(r, span, stride=d4)
        emit(1, rows, q4[rows, :], cat(kp4[rows, :], k4[rows, :]), cat(vp4[rows, :], v4[rows, :]), valid_edge)
        for j in range(1, nb4):
            rows = pl.ds(j * span * d4 + r, span, stride=d4)
            keys = pl.ds((j - 1) * span * d4 + r, 2 * span, stride=d4)
            emit(1, rows, q4[rows, :], k4[keys, :], v4[keys, :], valid)

    d16 = 16
    for r in range(d16):
        rows = pl.ds(r, span, stride=d16)
        emit(2, rows, q16[rows, :], cat(kp16[rows, :], k16[rows, :]), cat(vp16[rows, :], v16[rows, :]), valid_edge)

    def merge(c, carry):
        rows = pl.ds(pl.multiple_of(c * ATTN_MERGE_ROWS, ATTN_MERGE_ROWS), ATTN_MERGE_ROWS)
        l0, l1, l2 = lg[0, rows, :], lg[1, rows, :], lg[2, rows, :]
        m = jnp.maximum(jnp.maximum(l0, l1), l2)
        e0, e1, e2 = jnp.exp(l0 - m), jnp.exp(l1 - m), jnp.exp(l2 - m)
        num = e0 * og[0, rows, :] + e1 * og[1, rows, :] + e2 * og[2, rows, :]
        o_ref[rows, :] = (num / (e0 + e1 + e2)).astype(o_ref.dtype)
        return carry

    lax.fori_loop(0, ATTN_TILE // ATTN_MERGE_ROWS, merge, 0)


def _dilated_attention(qkv, seq):
    t, width = qkv.shape
    hd, span, tile = ATTN_HEAD_DIM, ATTN_SPAN, ATTN_TILE
    heads = width // (3 * len(DILATED_GROUPS) * hd)
    tiles_per_seq = seq // tile

    def col(group, which):
        return lambda h: (group * 3 + which) * heads + h

    def cur(group, which):
        return pl.BlockSpec((tile, hd), lambda h, i: (i, col(group, which)(h)))

    def prev(group, which, rows):
        per = tile // rows
        return pl.BlockSpec((rows, hd), lambda h, i: (jnp.maximum(i * per - 1, 0), col(group, which)(h)))

    in_specs = []
    for group, (window, dilation) in enumerate(DILATED_GROUPS):
        assert window // dilation == span and tile % window == 0
        in_specs += [cur(group, 0), cur(group, 1), cur(group, 2), prev(group, 1, window), prev(group, 2, window)]
    return pl.pallas_call(
        functools.partial(_attn_kernel, tiles_per_seq=tiles_per_seq),
        grid=(heads, t // tile),
        in_specs=in_specs,
        out_specs=pl.BlockSpec((tile, hd), lambda h, i: (i, h)),
        out_shape=jax.ShapeDtypeStruct((t, heads * hd), BF16),
        scratch_shapes=[pltpu.VMEM((len(DILATED_GROUPS), tile, hd), F32)] * 2,
        compiler_params=_params("parallel", "parallel"),
        name="dilated_attention",
    )(*([qkv] * len(in_specs)))


def _pad_rank(p, q):
    r = p.shape[1]
    if r % LORA_PAD:
        extra = LORA_PAD - r % LORA_PAD
        p = jnp.pad(p, ((0, 0), (0, extra)))
        q = jnp.pad(q, ((0, extra), (0, 0)))
    return p.astype(BF16), q.astype(BF16)


def _rwkv_layer(x, mu, w_rkv, w0, w1, w2, a0, a1, a2, g1, g2, k_k, k_a, r_k, gn_g, gn_b, w_out, ln_g, ln_b):
    bsz, seq, dm = x.shape
    n = RWKV_HEAD_SIZE
    heads = dm // n
    t = bsz * seq
    def cols(wt):
        lead = wt.shape[:-1]
        return wt.reshape(*lead, heads, n).swapaxes(-1, -2).reshape(*lead, dm)

    mixed = _token_shift_mix(x, mu[jnp.array([0, 2, 3, 1, 4, 5])])
    rkv = _batched_matmul(mixed, cols(w_rkv).astype(BF16), 3, F32, rows_per_seq=seq)
    w1b, w2b = _pad_rank(w1, cols(w2))
    a1b, a2b = _pad_rank(a1, cols(a2))
    g1b, g2b = _pad_rank(g1, cols(g2))
    decay, a, g = _rwkv_lora(mixed, w1b, w2b, cols(w0).reshape(1, dm), a1b, a2b, cols(a0).reshape(1, dm),
                             g1b, g2b, seq)

    def lanes_param(p):
        return jnp.repeat(jnp.tile(p.reshape(heads, n).T, (1, bsz)), SUBLANES, axis=0)

    y = _rwkv_scan_pipelined(rkv, decay, a, lanes_param(k_k), lanes_param(k_a), lanes_param(r_k),
                             lanes_param(gn_g), lanes_param(gn_b), steps=min(64, seq))
    w_out_rows = w_out.reshape(heads, n, dm).swapaxes(0, 1).reshape(dm, dm)
    return _matmul_residual_ln(y, w_out_rows.astype(BF16), x.reshape(t, dm), ln_g, ln_b, gate=g,
                               tm=min(256, seq))


def _attn_layer(x32, xb, w_in, w_out, ln_g, ln_b, bsz, seq):
    t, dm = x32.shape
    qkv = _batched_matmul(xb[None], w_in[None], 1, F32)[0]
    merged = _dilated_attention(qkv, seq)
    return _matmul_residual_ln(merged, w_out.astype(BF16), x32, ln_g, ln_b, tm=512)


def _ffn_layer(x32, xb, w_up_all, layer, conv_w, conv_b, w_down, ln_g, ln_b, seq, emit_bf16):
    act = _ffn_up(xb, w_up_all, layer, conv_w, conv_b, seq)
    return _matmul_residual_ln(act, w_down.astype(BF16), x32, ln_g, ln_b, tm=256, emit_bf16=emit_bf16)


def kernel(x, rwkv_mu, rwkv_w_rkv, rwkv_w0, rwkv_w1, rwkv_w2, rwkv_a0, rwkv_a1, rwkv_a2, rwkv_g1, rwkv_g2, rwkv_k_k, rwkv_k_a, rwkv_r_k, rwkv_gn_g, rwkv_gn_b, rwkv_w_out, attn_w_in, attn_w_out, ffn_w_up, ffn_conv_w, ffn_conv_b, ffn_w_down, ln_mix_g, ln_mix_b, ln_ffn_g, ln_ffn_b):
    bsz, seq, dm = x.shape
    x32, xb = _rwkv_layer(x, rwkv_mu[0], rwkv_w_rkv[0], rwkv_w0[0], rwkv_w1[0], rwkv_w2[0], rwkv_a0[0],
                          rwkv_a1[0], rwkv_a2[0], rwkv_g1[0], rwkv_g2[0], rwkv_k_k[0], rwkv_k_a[0],
                          rwkv_r_k[0], rwkv_gn_g[0], rwkv_gn_b[0], rwkv_w_out[0], ln_mix_g[0], ln_mix_b[0])
    x32, xb = _ffn_layer(x32, xb, ffn_w_up, 0, ffn_conv_w[0], ffn_conv_b[0], ffn_w_down[0],
                         ln_ffn_g[0], ln_ffn_b[0], seq, True)
    x32, xb = _attn_layer(x32, xb, attn_w_in[0], attn_w_out[0], ln_mix_g[1], ln_mix_b[1], bsz, seq)
    x32, _ = _ffn_layer(x32, xb, ffn_w_up, 1, ffn_conv_w[1], ffn_conv_b[1], ffn_w_down[1],
                        ln_ffn_g[1], ln_ffn_b[1], seq, False)
    return x32.reshape(bsz, seq, dm)
```

```python
import functools

import jax
import jax.numpy as jnp
from jax import lax
from jax.experimental import pallas as pl
from jax.experimental.pallas import tpu as pltpu

F32 = jnp.float32
BF16 = jnp.bfloat16

V7X_VMEM_BYTES = 64 * 1024 * 1024
VMEM_LIMIT_BYTES = V7X_VMEM_BYTES - 8 * 1024 * 1024
SUBLANES = 8
LANES = 128

RWKV_HEAD_SIZE = 64
N_SHIFT_MIX = 6
GN_EPS = 64e-5
ATTN_HEAD_DIM = 128
DILATED_GROUPS = ((128, 1), (512, 4), (2048, 16))
MASK_VALUE = -1e30
CONV_WIDTH = 3
LN_EPS = 1e-5
DEPTH = 2
DEEPNORM_ALPHA = (2.0 * DEPTH) ** 0.25
LORA_PAD = 128
LN_CHUNK_ROWS = 32


def _params(*semantics):
    return pltpu.CompilerParams(dimension_semantics=semantics, vmem_limit_bytes=VMEM_LIMIT_BYTES)


def _mix_kernel(x_ref, xp_ref, mu_ref, o_ref):
    s = pl.program_id(1)
    x = x_ref[...]
    before = jnp.where(s > 0, xp_ref[SUBLANES - 1:SUBLANES, :], 0.0)
    prev = pltpu.roll(x, 1, 0)
    row = lax.broadcasted_iota(jnp.int32, x.shape, 0)
    prev = jnp.where(row == 0, before, prev)
    xx = prev - x
    for c in range(N_SHIFT_MIX):
        o_ref[c] = (x + xx * mu_ref[c:c + 1, :]).astype(o_ref.dtype)


def _token_shift_mix(x, mu, ts=256):
    bsz, seq, dm = x.shape
    ts = min(ts, seq)
    nst = seq // ts
    return pl.pallas_call(
        _mix_kernel,
        grid=(bsz, nst),
        in_specs=[
            pl.BlockSpec((None, ts, dm), lambda b, s: (b, s, 0)),
            pl.BlockSpec((None, SUBLANES, dm), lambda b, s: (b, jnp.maximum(s * (ts // SUBLANES) - 1, 0), 0)),
            pl.BlockSpec((N_SHIFT_MIX, dm), lambda b, s: (0, 0)),
        ],
        out_specs=pl.BlockSpec((N_SHIFT_MIX, ts, dm), lambda b, s: (0, b * nst + s, 0)),
        out_shape=jax.ShapeDtypeStruct((N_SHIFT_MIX, bsz * seq, dm), BF16),
        compiler_params=_params("parallel", "parallel"),
        name="token_shift_mix",
    )(x, x, mu)


def _mm_kernel(a_ref, w_ref, o_ref, wb_ref):
    @pl.when(pl.program_id(2) == 0)
    def _():
        wb_ref[...] = w_ref[...].astype(BF16)

    res = jnp.dot(a_ref[...], wb_ref[...], preferred_element_type=F32).astype(o_ref.dtype)
    if len(o_ref.shape) == 2:
        o_ref[...] = res
    else:
        for q in range(o_ref.shape[0]):
            o_ref[q] = res[:, q * LANES:(q + 1) * LANES]


def _batched_matmul(a, w, n_batch, out_dtype, tm=1024, tn=1024, rows_per_seq=None):
    _, m, kdim = a.shape
    tm = min(tm, m)
    n = w.shape[2]
    if rows_per_seq is None:
        out_spec = pl.BlockSpec((None, tm, tn), lambda c, j, i: (c, i, j))
        out_shape = jax.ShapeDtypeStruct((n_batch, m, n), out_dtype)
    else:
        tm = min(tm, rows_per_seq)
        per = rows_per_seq // tm
        out_spec = pl.BlockSpec((None, None, tn // LANES, tm, LANES), lambda c, j, i: (c, i // per, j, i % per, 0))
        out_shape = jax.ShapeDtypeStruct((n_batch, m // rows_per_seq, n // LANES, rows_per_seq, LANES), out_dtype)
    return pl.pallas_call(
        _mm_kernel,
        grid=(n_batch, n // tn, m // tm),
        in_specs=[
            pl.BlockSpec((None, tm, kdim), lambda c, j, i: (c, i, 0)),
            pl.BlockSpec((None, kdim, tn), lambda c, j, i: (c, 0, j)),
        ],
        out_specs=out_spec,
        out_shape=out_shape,
        scratch_shapes=[pltpu.VMEM((kdim, tn), BF16)],
        compiler_params=_params("parallel", "parallel", "arbitrary"),
        name="batched_matmul",
    )(a, w)


def _lora_kernel(xw_ref, xa_ref, xg_ref, w1_ref, w2_ref, w0_ref, a1_ref, a2_ref, a0_ref,
                 g1_ref, g2_ref, decay_ref, a_ref, g_ref):
    def two_stage(x_ref, p_ref, q_ref, act):
        h = act(jnp.dot(x_ref[...], p_ref[...], preferred_element_type=F32))
        return jnp.dot(h.astype(BF16), q_ref[...], preferred_element_type=F32)

    def split_lanes(o_ref, val):
        for q in range(o_ref.shape[0]):
            o_ref[q] = val[:, q * LANES:(q + 1) * LANES]

    z = w0_ref[...] + two_stage(xw_ref, w1_ref, w2_ref, jnp.tanh)
    split_lanes(decay_ref, jnp.exp(-jnp.exp(-0.5) * jax.nn.sigmoid(z)))
    split_lanes(a_ref, jax.nn.sigmoid(a0_ref[...] + two_stage(xa_ref, a1_ref, a2_ref, lambda h: h)))
    g_ref[...] = two_stage(xg_ref, g1_ref, g2_ref, jax.nn.sigmoid)


def _rwkv_lora(mixed, w1, w2, w0, a1, a2, a0, g1, g2, seq, tm=256):
    _, t, dm = mixed.shape
    tm = min(tm, seq)
    per = seq // tm

    def full(arr):
        return pl.BlockSpec(arr.shape, lambda i: (0,) * arr.ndim)

    def mix_spec(c):
        return pl.BlockSpec((None, tm, dm), lambda i: (c, i, 0))

    out_spec = pl.BlockSpec((tm, dm), lambda i: (i, 0))
    out = jax.ShapeDtypeStruct((t, dm), F32)
    split_spec = pl.BlockSpec((None, dm // LANES, tm, LANES), lambda i: (i // per, 0, i % per, 0))
    split = jax.ShapeDtypeStruct((t // seq, dm // LANES, seq, LANES), F32)
    return pl.pallas_call(
        _lora_kernel,
        grid=(t // tm,),
        in_specs=[mix_spec(3), mix_spec(4), mix_spec(5), full(w1), full(w2), full(w0),
                  full(a1), full(a2), full(a0), full(g1), full(g2)],
        out_specs=[split_spec, split_spec, out_spec],
        out_shape=[split, split, out],
        compiler_params=_params("parallel"),
        name="rwkv_lora",
    )(mixed, mixed, mixed, w1, w2, w0, a1, a2, a0, g1, g2)


SCAN_V_GROUPS = 2
SCAN_PARTIALS = 2


SCAN_BATCH = 4
SCAN_CHUNK = LANES // SCAN_BATCH


def _swap_halves(a0, a1, a2, a3, low_half):
    r0, r1, r2, r3 = (pltpu.roll(a, 2 * SCAN_CHUNK, 1) for a in (a0, a1, a2, a3))
    return (jnp.where(low_half, a0, r2), jnp.where(low_half, a1, r3),
            jnp.where(low_half, r0, a2), jnp.where(low_half, r1, a3))


def _swap_chunks(p0, p1, p2, p3, even_chunk):
    return (jnp.where(even_chunk, p0, pltpu.roll(p1, SCAN_CHUNK, 1)),
            jnp.where(even_chunk, pltpu.roll(p0, 3 * SCAN_CHUNK, 1), p1),
            jnp.where(even_chunk, p2, pltpu.roll(p3, SCAN_CHUNK, 1)),
            jnp.where(even_chunk, pltpu.roll(p2, 3 * SCAN_CHUNK, 1), p3))


Q_KK, Q_WR, Q_W, Q_B, Q_K2, Q_V = range(6)
R_INV2, R_BR, R_KR, R_BONUS = range(4)
SCAN_UNIT_Q = 2
SCAN_Q = RWKV_HEAD_SIZE // SCAN_BATCH


def _scan_pipelined_kernel(r_ref, k_ref, v_ref, w_ref, a_ref, kkp_ref, kap_ref, rkp_ref, gng_ref, gnb_ref,
                           y_ref, st_ref, buf_a, buf_b, rows_a, rows_b, y_s, yn_s, *, steps):
    n = RWKV_HEAD_SIZE
    n_vg = n // SUBLANES
    n_groups = steps // SUBLANES
    lane = lax.broadcasted_iota(jnp.int32, (SUBLANES, LANES), 1)
    low_half = lane < 2 * SCAN_CHUNK
    even_chunk = (lane & SCAN_CHUNK) == 0

    @pl.when(pl.program_id(0) == 0)
    def _():
        st_ref[...] = jnp.zeros_like(st_ref)
        yn_s[...] = jnp.zeros_like(yn_s)

    def tile_rows(i):
        if isinstance(i, int):
            return pl.ds(i * SUBLANES, SUBLANES)
        return pl.ds(pl.multiple_of(i * SUBLANES, SUBLANES), SUBLANES)

    def regroup(s0, qs, dst):
        srcs = (r_ref, k_ref, v_ref, w_ref, a_ref)
        halves = [[_swap_halves(*(src[bi, q, pl.ds(s0, SUBLANES), :] for bi in range(SCAN_BATCH)), low_half)
                   for src in srcs] for q in qs]
        sums = None
        for q, per_src in zip(qs, halves):
            rq, kq, vq, wq, aq = (_swap_chunks(*h, even_chunk) for h in per_src)
            for j in range(SCAN_BATCH):
                rows = tile_rows(q * SCAN_BATCH + j)
                r, k, v, w, a = rq[j], kq[j], vq[j], wq[j], aq[j]
                kk0 = k * kkp_ref[rows, :]
                k2 = k * (1.0 + (a - 1.0) * kap_ref[rows, :])
                b0 = kk0 * a
                rk2 = r * k2
                dst[Q_KK, rows, :] = kk0
                dst[Q_WR, rows, :] = w * r
                dst[Q_W, rows, :] = w
                dst[Q_B, rows, :] = b0
                dst[Q_K2, rows, :] = k2
                dst[Q_V, rows, :] = v
                terms = (kk0 * kk0, b0 * r, rk2, rk2 * rkp_ref[rows, :])
                sums = terms if sums is None else tuple(x + y for x, y in zip(sums, terms))
        return sums

    def finish_rows(sums, rows_ref):
        ss, br, kr, bonus = sums
        rows_ref[R_INV2] = 1.0 / jnp.maximum(ss, 1e-24)
        rows_ref[R_BR] = br
        rows_ref[R_KR] = kr
        rows_ref[R_BONUS] = bonus

    def unregroup(s0, qs):
        halves = [_swap_halves(*(yn_s[tile_rows(q * SCAN_BATCH + j), :] for j in range(SCAN_BATCH)), low_half)
                  for q in qs]
        for q, h in zip(qs, halves):
            for bi, val in enumerate(_swap_chunks(*h, even_chunk)):
                y_ref[bi, q, pl.ds(s0, SUBLANES), :] = val

    def step(t, cur, cur_rows):
        inv2_row = cur_rows[R_INV2, pl.ds(t, 1), :]
        br_row = cur_rows[R_BR, pl.ds(t, 1), :]
        kr_row = cur_rows[R_KR, pl.ds(t, 1), :]
        for g0 in range(0, n_vg, SCAN_V_GROUPS):
            groups = range(g0, g0 + SCAN_V_GROUPS)
            acc_sa = {g: [None] * SCAN_PARTIALS for g in groups}
            acc_y = {g: [None] * SCAN_PARTIALS for g in groups}
            for ki in range(n):
                kk_row = cur[Q_KK, pl.ds(ki * SUBLANES + t, 1), :]
                wr_row = cur[Q_WR, pl.ds(ki * SUBLANES + t, 1), :]
                p = ki % SCAN_PARTIALS
                for g in groups:
                    s = st_ref[ki, g * SUBLANES:(g + 1) * SUBLANES, :]
                    t_sa = s * kk_row
                    t_y = s * wr_row
                    acc_sa[g][p] = t_sa if acc_sa[g][p] is None else acc_sa[g][p] + t_sa
                    acc_y[g][p] = t_y if acc_y[g][p] is None else acc_y[g][p] + t_y
            sa = {}
            vv = {}
            for g in groups:
                v_rows = pl.ds(g * SUBLANES * SUBLANES + t, SUBLANES, stride=SUBLANES)
                sa[g] = -functools.reduce(lambda x, y: x + y, acc_sa[g]) * inv2_row
                vv[g] = cur[Q_V, v_rows, :]
                y_s[v_rows, :] = (functools.reduce(lambda x, y: x + y, acc_y[g])
                                  + sa[g] * br_row + vv[g] * kr_row)
            for ki in range(n):
                w_row = cur[Q_W, pl.ds(ki * SUBLANES + t, 1), :]
                b_row = cur[Q_B, pl.ds(ki * SUBLANES + t, 1), :]
                k_row = cur[Q_K2, pl.ds(ki * SUBLANES + t, 1), :]
                for g in groups:
                    s = st_ref[ki, g * SUBLANES:(g + 1) * SUBLANES, :]
                    st_ref[ki, g * SUBLANES:(g + 1) * SUBLANES, :] = s * w_row + sa[g] * b_row + vv[g] * k_row

    def normalise(cur, cur_rows):
        tot = None
        for vi in range(n):
            yv = y_s[vi * SUBLANES:(vi + 1) * SUBLANES, :]
            tot = yv if tot is None else tot + yv
        mean = tot * (1.0 / n)
        sq = None
        for vi in range(n):
            yc = y_s[vi * SUBLANES:(vi + 1) * SUBLANES, :] - mean
            sq = yc * yc if sq is None else sq + yc * yc
        rstd = lax.rsqrt(sq * (1.0 / n) + GN_EPS)
        bonus_v = cur_rows[R_BONUS]
        for vi in range(n):
            rows = slice(vi * SUBLANES, (vi + 1) * SUBLANES)
            yn_s[rows, :] = ((y_s[rows, :] - mean) * rstd * gng_ref[rows, :] + gnb_ref[rows, :]
                             + bonus_v * cur[Q_V, rows, :])

    def run_group(g, cur, cur_rows, nxt, nxt_rows):
        s0 = pl.multiple_of(g * SUBLANES, SUBLANES)
        s_next = pl.multiple_of(jnp.minimum(s0 + SUBLANES, steps - SUBLANES), SUBLANES)
        s_prev = pl.multiple_of(jnp.maximum(s0 - SUBLANES, 0), SUBLANES)

        def body(t, sums):
            step(t, cur, cur_rows)
            qs = [SCAN_UNIT_Q * t + u for u in range(SCAN_UNIT_Q)]
            part = regroup(s_next, qs, nxt)
            unregroup(s_prev, qs)
            return tuple(x + y for x, y in zip(sums, part))

        zero = jnp.zeros((SUBLANES, LANES), F32)
        sums = lax.fori_loop(0, SUBLANES, body, (zero, zero, zero, zero))
        finish_rows(sums, nxt_rows)
        normalise(cur, cur_rows)

    half = SCAN_Q // 2
    first = regroup(0, list(range(half)), buf_a)
    second = regroup(0, list(range(half, SCAN_Q)), buf_a)
    finish_rows(tuple(x + y for x, y in zip(first, second)), rows_a)

    def two_groups(i, carry):
        run_group(2 * i, buf_a, rows_a, buf_b, rows_b)
        run_group(2 * i + 1, buf_b, rows_b, buf_a, rows_a)
        return carry

    lax.fori_loop(0, n_groups // 2, two_groups, 0)

    unregroup(steps - SUBLANES, list(range(half)))
    unregroup(steps - SUBLANES, list(range(half, SCAN_Q)))


def _rwkv_scan_pipelined(rkv, w, a, kkp, kap, rkp, gng, gnb, steps=64):
    bsz, nq, seq, lanes = w.shape
    n = RWKV_HEAD_SIZE
    assert bsz == SCAN_BATCH and nq == SCAN_Q and lanes == LANES and (steps // SUBLANES) % 2 == 0

    def rkv_spec(c):
        return pl.BlockSpec((None, bsz, nq, steps, lanes), lambda i: (c, 0, 0, i, 0))

    seq_spec = pl.BlockSpec((bsz, nq, steps, lanes), lambda i: (0, 0, i, 0))
    par_spec = pl.BlockSpec((n * SUBLANES, LANES), lambda i: (0, 0))
    rows = pltpu.VMEM((n * SUBLANES, LANES), F32)
    group_buf = pltpu.VMEM((6, n * SUBLANES, LANES), F32)
    group_rows = pltpu.VMEM((4, SUBLANES, LANES), F32)
    return pl.pallas_call(
        functools.partial(_scan_pipelined_kernel, steps=steps),
        grid=(seq // steps,),
        in_specs=[rkv_spec(0), rkv_spec(1), rkv_spec(2), seq_spec, seq_spec] + [par_spec] * 5,
        out_specs=seq_spec,
        out_shape=jax.ShapeDtypeStruct((bsz, nq, seq, lanes), F32),
        scratch_shapes=[pltpu.VMEM((n, n, LANES), F32), group_buf, group_buf, group_rows, group_rows, rows, rows],
        compiler_params=_params("arbitrary"),
        name="rwkv7_scan",
    )(rkv, rkv, rkv, w, a, kkp, kap, rkp, gng, gnb)


def _mm_ln_kernel(*refs, gated, emit_bf16):
    refs = list(refs)
    a_ref = refs.pop(0)
    gate_ref = refs.pop(0) if gated else None
    w_ref, x_ref, g_ref, b_ref, o_ref = refs[:5]
    ob_ref = refs[5] if emit_bf16 else None
    acc_ref = refs[-1]

    if gated:
        a = jnp.concatenate([a_ref[q] for q in range(a_ref.shape[0])], axis=1)
        a = (a * gate_ref[...]).astype(BF16)
    else:
        a = a_ref[...]
    acc_ref[...] = jnp.dot(a, w_ref[...], preferred_element_type=F32)

    for c in range(acc_ref.shape[0] // LN_CHUNK_ROWS):
        rows = slice(c * LN_CHUNK_ROWS, (c + 1) * LN_CHUNK_ROWS)
        z = DEEPNORM_ALPHA * x_ref[rows, :] + acc_ref[rows, :]
        mu = jnp.mean(z, axis=-1, keepdims=True)
        zc = z - mu
        var = jnp.mean(zc * zc, axis=-1, keepdims=True)
        out = zc * lax.rsqrt(var + LN_EPS) * g_ref[...] + b_ref[...]
        o_ref[rows, :] = out
        if emit_bf16:
            ob_ref[rows, :] = out.astype(BF16)


def _matmul_residual_ln(a, w, x, g, b, *, gate=None, tm, emit_bf16=True):
    kdim, n = w.shape
    m = x.shape[0]
    row = lambda i: (i, 0)
    fixed = lambda i: (0, 0)
    args = [a]
    if gate is None:
        in_specs = [pl.BlockSpec((tm, kdim), row)]
    else:
        per = a.shape[2] // tm
        in_specs = [pl.BlockSpec((None, kdim // LANES, tm, LANES), lambda i: (i // per, 0, i % per, 0)),
                    pl.BlockSpec((tm, kdim), row)]
        args.append(gate)
    in_specs += [pl.BlockSpec((kdim, n), fixed, pipeline_mode=pl.Buffered(1)),
                 pl.BlockSpec((tm, n), row),
                 pl.BlockSpec((1, n), fixed),
                 pl.BlockSpec((1, n), fixed)]
    args += [w, x, g.reshape(1, n), b.reshape(1, n)]
    out_specs = [pl.BlockSpec((tm, n), row)]
    out_shape = [jax.ShapeDtypeStruct((m, n), F32)]
    if emit_bf16:
        out_specs.append(pl.BlockSpec((tm, n), row))
        out_shape.append(jax.ShapeDtypeStruct((m, n), BF16))
    outs = pl.pallas_call(
        functools.partial(_mm_ln_kernel, gated=gate is not None, emit_bf16=emit_bf16),
        grid=(m // tm,),
        in_specs=in_specs,
        out_specs=out_specs,
        out_shape=out_shape,
        scratch_shapes=[pltpu.VMEM((tm, n), F32)],
        compiler_params=_params("parallel"),
        name="matmul_residual_ln",
    )(*args)
    return outs if emit_bf16 else (outs[0], None)


FFN_COL_CHUNK = 256


def _ffn_up_kernel(x_ref, wg32_ref, wu32_ref, cw_ref, cb_ref, o_ref, tail_ref, wg_ref, wu_ref, *,
                   tiles_per_seq):
    i = pl.program_id(1)

    @pl.when(i == 0)
    def _():
        wg_ref[...] = wg32_ref[...].astype(BF16)
        wu_ref[...] = wu32_ref[...].astype(BF16)

    x = x_ref[...]
    tm = x.shape[0]
    seq_start = i % tiles_per_seq == 0
    row = lax.broadcasted_iota(jnp.int32, (SUBLANES, FFN_COL_CHUNK), 0)
    for c in range(o_ref.shape[1] // FFN_COL_CHUNK):
        cols = slice(c * FFN_COL_CHUNK, (c + 1) * FFN_COL_CHUNK)
        gate = jnp.dot(x, wg_ref[:, cols], preferred_element_type=F32)
        up = jnp.dot(x, wu_ref[:, cols], preferred_element_type=F32)
        tail = jnp.where(seq_start, 0.0, tail_ref[:, cols])
        tail_ref[:, cols] = gate[tm - SUBLANES:, :]
        r1 = pltpu.roll(gate, 1, 0)
        r2 = pltpu.roll(gate, 2, 0)
        h1 = jnp.where(row == 0, tail[SUBLANES - 1:, :], r1[:SUBLANES, :])
        h2 = jnp.where(row == 0, tail[SUBLANES - 2:SUBLANES - 1, :],
                       jnp.where(row == 1, tail[SUBLANES - 1:, :], r2[:SUBLANES, :]))
        g1 = jnp.concatenate([h1, r1[SUBLANES:, :]], axis=0)
        g2 = jnp.concatenate([h2, r2[SUBLANES:, :]], axis=0)
        acc = (cb_ref[:, cols] + g2 * cw_ref[0:1, cols] + g1 * cw_ref[1:2, cols] + gate * cw_ref[2:3, cols])
        o_ref[:, cols] = (acc * jax.nn.sigmoid(acc) * up).astype(o_ref.dtype)


def _ffn_up(xb, w_up_all, layer, conv_w, conv_b, seq, tm=1024, tn=512):
    t, dm = xb.shape
    dff = w_up_all.shape[2] // 2
    n_col = dff // tn
    w_up = w_up_all
    return pl.pallas_call(
        functools.partial(_ffn_up_kernel, tiles_per_seq=seq // tm),
        grid=(n_col, t // tm),
        in_specs=[
            pl.BlockSpec((tm, dm), lambda j, i: (i, 0)),
            pl.BlockSpec((None, dm, tn), lambda j, i: (layer, 0, j)),
            pl.BlockSpec((None, dm, tn), lambda j, i: (layer, 0, j + n_col)),
            pl.BlockSpec((CONV_WIDTH, tn), lambda j, i: (0, j)),
            pl.BlockSpec((1, tn), lambda j, i: (0, j)),
        ],
        out_specs=pl.BlockSpec((tm, tn), lambda j, i: (i, j)),
        out_shape=jax.ShapeDtypeStruct((t, dff), BF16),
        scratch_shapes=[pltpu.VMEM((SUBLANES, tn), F32), pltpu.VMEM((dm, tn), BF16), pltpu.VMEM((dm, tn), BF16)],
        compiler_params=_params("parallel", "arbitrary"),
        name="ffn_up_convglu",
    )(xb, w_up, w_up, conv_w, conv_b.reshape(1, dff))


ATTN_SPAN = 128
ATTN_TILE = 2048
ATTN_MERGE_ROWS = 256


def _block_attention(q, k, v, valid):
    s = lax.dot_general(q.astype(BF16), k.astype(BF16), (((1,), (1,)), ((), ())),
                        preferred_element_type=F32) * (ATTN_HEAD_DIM ** -0.5)
    s = jnp.where(valid, s, MASK_VALUE)
    m = jnp.max(s, axis=-1, keepdims=True)
    p = jnp.exp(s - m)
    l = jnp.sum(p, axis=-1, keepdims=True)
    o = jnp.dot(p.astype(BF16), v.astype(BF16), preferred_element_type=F32) * (1.0 / l)
    return o, m + jnp.log(l)


def _attn_kernel(q1, k1, v1, kp1, vp1, q4, k4, v4, kp4, vp4, q16, k16, v16, kp16, vp16,
                 o_ref, og, lg, x4, *, tiles_per_seq):
    span, hd = ATTN_SPAN, ATTN_HEAD_DIM
    first_tile = pl.program_id(1) % tiles_per_seq == 0
    qi = lax.broadcasted_iota(jnp.int32, (span, 2 * span), 0)
    kj = lax.broadcasted_iota(jnp.int32, (span, 2 * span), 1)
    valid = (kj <= qi + span) & (kj >= qi)
    valid_edge = valid & (kj >= jnp.where(first_tile, span, 0))

    def emit(g, rows, q, k, v, mask):
        o, lse = _block_attention(q, k, v, mask)
        og[g, rows, :] = o
        lg[g, rows, :] = jnp.broadcast_to(lse, (span, hd))

    def cat(a, b):
        return jnp.concatenate([a, b], axis=0)

    n1 = ATTN_TILE // span
    rows = pl.ds(0, span)
    emit(0, rows, q1[rows, :], cat(kp1[...], k1[rows, :]), cat(vp1[...], v1[rows, :]), valid_edge)

    for j in range(1, n1):
        rows = pl.ds(j * span, span)
        keys = pl.ds((j - 1) * span, 2 * span)
        emit(0, rows, q1[rows, :], k1[keys, :], v1[keys, :], valid)

    d4 = 4
    nb4 = ATTN_TILE // (span * d4)
    for r in range(d4):
        rows = pl.ds(r, span, stride=d4)
        emit(1, rows, q4[rows, :], cat(kp4[rows, :], k4[rows, :]), cat(vp4[rows, :], v4[rows, :]), valid_edge)
        for j in range(1, nb4):
            rows = pl.ds(j * span * d4 + r, span, stride=d4)
            keys = pl.ds((j - 1) * span * d4 + r, 2 * span, stride=d4)
            emit(1, rows, q4[rows, :], k4[keys, :], v4[keys, :], valid)

    d16 = 16
    quarter = ATTN_TILE // d4
    for a, src in enumerate((q16, k16, v16, kp16, vp16)):
        for r4 in range(d4):
            x4[a, r4 * quarter:(r4 + 1) * quarter, :] = src[pl.ds(r4, quarter, stride=d4), :]
    for r in range(d16):
        sub = pl.ds((r % d4) * quarter + r // d4, span, stride=d4)
        o, lse = _block_attention(x4[0, sub, :], cat(x4[3, sub, :], x4[1, sub, :]),
                                  cat(x4[4, sub, :], x4[2, sub, :]), valid_edge)
        x4[0, sub, :] = o
        x4[3, sub, :] = jnp.broadcast_to(lse, (span, hd))
    for r4 in range(d4):
        og[2, pl.ds(r4, quarter, stride=d4), :] = x4[0, r4 * quarter:(r4 + 1) * quarter, :]
        lg[2, pl.ds(r4, quarter, stride=d4), :] = x4[3, r4 * quarter:(r4 + 1) * quarter, :]

    def merge(c, carry):
        rows = pl.ds(pl.multiple_of(c * ATTN_MERGE_ROWS, ATTN_MERGE_ROWS), ATTN_MERGE_ROWS)
        l0, l1, l2 = lg[0, rows, :], lg[1, rows, :], lg[2, rows, :]
        m = jnp.maximum(jnp.maximum(l0, l1), l2)
        e0, e1, e2 = jnp.exp(l0 - m), jnp.exp(l1 - m), jnp.exp(l2 - m)
        num = e0 * og[0, rows, :] + e1 * og[1, rows, :] + e2 * og[2, rows, :]
        o_ref[rows, :] = (num / (e0 + e1 + e2)).astype(o_ref.dtype)
        return carry

    lax.fori_loop(0, ATTN_TILE // ATTN_MERGE_ROWS, merge, 0)


def _dilated_attention(qkv, seq):
    t, width = qkv.shape
    hd, span, tile = ATTN_HEAD_DIM, ATTN_SPAN, ATTN_TILE
    heads = width // (3 * len(DILATED_GROUPS) * hd)
    tiles_per_seq = seq // tile

    def col(group, which):
        return lambda h: (group * 3 + which) * heads + h

    def cur(group, which):
        return pl.BlockSpec((tile, hd), lambda h, i: (i, col(group, which)(h)))

    def prev(group, which, rows):
        per = tile // rows
        return pl.BlockSpec((rows, hd), lambda h, i: (jnp.maximum(i * per - 1, 0), col(group, which)(h)))

    in_specs = []
    for group, (window, dilation) in enumerate(DILATED_GROUPS):
        assert window // dilation == span and tile % window == 0
        in_specs += [cur(group, 0), cur(group, 1), cur(group, 2), prev(group, 1, window), prev(group, 2, window)]
    return pl.pallas_call(
        functools.partial(_attn_kernel, tiles_per_seq=tiles_per_seq),
        grid=(heads, t // tile),
        in_specs=in_specs,
        out_specs=pl.BlockSpec((tile, hd), lambda h, i: (i, h)),
        out_shape=jax.ShapeDtypeStruct((t, heads * hd), BF16),
        scratch_shapes=[pltpu.VMEM((len(DILATED_GROUPS), tile, hd), F32)] * 2 + [pltpu.VMEM((5, tile, hd), F32)],
        compiler_params=_params("parallel", "parallel"),
        name="dilated_attention",
    )(*([qkv] * len(in_specs)))


def _pad_rank(p, q):
    r = p.shape[1]
    if r % LORA_PAD:
        extra = LORA_PAD - r % LORA_PAD
        p = jnp.pad(p, ((0, 0), (0, extra)))
        q = jnp.pad(q, ((0, extra), (0, 0)))
    return p.astype(BF16), q.astype(BF16)


def _rwkv_layer(x, mu, w_rkv, w0, w1, w2, a0, a1, a2, g1, g2, k_k, k_a, r_k, gn_g, gn_b, w_out, ln_g, ln_b):
    bsz, seq, dm = x.shape
    n = RWKV_HEAD_SIZE
    heads = dm // n
    t = bsz * seq
    def cols(wt):
        lead = wt.shape[:-1]
        return wt.reshape(*lead, heads, n).swapaxes(-1, -2).reshape(*lead, dm)

    mixed = _token_shift_mix(x, mu[jnp.array([0, 2, 3, 1, 4, 5])])
    rkv = _batched_matmul(mixed, cols(w_rkv).astype(BF16), 3, F32, rows_per_seq=seq)
    w1b, w2b = _pad_rank(w1, cols(w2))
    a1b, a2b = _pad_rank(a1, cols(a2))
    g1b, g2b = _pad_rank(g1, cols(g2))
    decay, a, g = _rwkv_lora(mixed, w1b, w2b, cols(w0).reshape(1, dm), a1b, a2b, cols(a0).reshape(1, dm),
                             g1b, g2b, seq)

    def lanes_param(p):
        return jnp.repeat(jnp.tile(p.reshape(heads, n).T, (1, bsz)), SUBLANES, axis=0)

    y = _rwkv_scan_pipelined(rkv, decay, a, lanes_param(k_k), lanes_param(k_a), lanes_param(r_k),
                             lanes_param(gn_g), lanes_param(gn_b), steps=min(64, seq))
    w_out_rows = w_out.reshape(heads, n, dm).swapaxes(0, 1).reshape(dm, dm)
    return _matmul_residual_ln(y, w_out_rows.astype(BF16), x.reshape(t, dm), ln_g, ln_b, gate=g,
                               tm=min(256, seq))


def _attn_layer(x32, xb, w_in, w_out, ln_g, ln_b, bsz, seq):
    t, dm = x32.shape
    qkv = _batched_matmul(xb[None], w_in[None], 1, F32)[0]
    merged = _dilated_attention(qkv, seq)
    return _matmul_residual_ln(merged, w_out.astype(BF16), x32, ln_g, ln_b, tm=512)


def _ffn_layer(x32, xb, w_up_all, layer, conv_w, conv_b, w_down, ln_g, ln_b, seq, emit_bf16):
    act = _ffn_up(xb, w_up_all, layer, conv_w, conv_b, seq)
    return _matmul_residual_ln(act, w_down.astype(BF16), x32, ln_g, ln_b, tm=256, emit_bf16=emit_bf16)


def kernel(x, rwkv_mu, rwkv_w_rkv, rwkv_w0, rwkv_w1, rwkv_w2, rwkv_a0, rwkv_a1, rwkv_a2, rwkv_g1, rwkv_g2, rwkv_k_k, rwkv_k_a, rwkv_r_k, rwkv_gn_g, rwkv_gn_b, rwkv_w_out, attn_w_in, attn_w_out, ffn_w_up, ffn_conv_w, ffn_conv_b, ffn_w_down, ln_mix_g, ln_mix_b, ln_ffn_g, ln_ffn_b):
    bsz, seq, dm = x.shape
    x32, xb = _rwkv_layer(x, rwkv_mu[0], rwkv_w_rkv[0], rwkv_w0[0], rwkv_w1[0], rwkv_w2[0], rwkv_a0[0],
                          rwkv_a1[0], rwkv_a2[0], rwkv_g1[0], rwkv_g2[0], rwkv_k_k[0], rwkv_k_a[0],
                          rwkv_r_k[0], rwkv_gn_g[0], rwkv_gn_b[0], rwkv_w_out[0], ln_mix_g[0], ln_mix_b[0])
    x32, xb = _ffn_layer(x32, xb, ffn_w_up, 0, ffn_conv_w[0], ffn_conv_b[0], ffn_w_down[0],
                         ln_ffn_g[0], ln_ffn_b[0], seq, True)
    x32, xb = _attn_layer(x32, xb, attn_w_in[0], attn_w_out[0], ln_mix_g[1], ln_mix_b[1], bsz, seq)
    x32, _ = _ffn_layer(x32, xb, ffn_w_up, 1, ffn_conv_w[1], ffn_conv_b[1], ffn_w_down[1],
                        ln_ffn_g[1], ln_ffn_b[1], seq, False)
    return x32.reshape(bsz, seq, dm)
```

```python
import functools

import jax
import jax.numpy as jnp
from jax import lax
from jax.experimental import pallas as pl
from jax.experimental.pallas import tpu as pltpu

F32 = jnp.float32
BF16 = jnp.bfloat16

V7X_VMEM_BYTES = 64 * 1024 * 1024
VMEM_LIMIT_BYTES = V7X_VMEM_BYTES - 8 * 1024 * 1024
SUBLANES = 8
LANES = 128

RWKV_HEAD_SIZE = 64
N_SHIFT_MIX = 6
GN_EPS = 64e-5
ATTN_HEAD_DIM = 128
DILATED_GROUPS = ((128, 1), (512, 4), (2048, 16))
MASK_VALUE = -1e30
CONV_WIDTH = 3
LN_EPS = 1e-5
DEPTH = 2
DEEPNORM_ALPHA = (2.0 * DEPTH) ** 0.25
LORA_PAD = 128
LN_CHUNK_ROWS = 32


def _params(*semantics):
    return pltpu.CompilerParams(dimension_semantics=semantics, vmem_limit_bytes=VMEM_LIMIT_BYTES)


def _mix_kernel(x_ref, xp_ref, mu_ref, o_ref):
    s = pl.program_id(1)
    x = x_ref[...]
    before = jnp.where(s > 0, xp_ref[SUBLANES - 1:SUBLANES, :], 0.0)
    prev = pltpu.roll(x, 1, 0)
    row = lax.broadcasted_iota(jnp.int32, x.shape, 0)
    prev = jnp.where(row == 0, before, prev)
    xx = prev - x
    for c in range(N_SHIFT_MIX):
        o_ref[c] = (x + xx * mu_ref[c:c + 1, :]).astype(o_ref.dtype)


def _token_shift_mix(x, mu, ts=256):
    bsz, seq, dm = x.shape
    ts = min(ts, seq)
    nst = seq // ts
    return pl.pallas_call(
        _mix_kernel,
        grid=(bsz, nst),
        in_specs=[
            pl.BlockSpec((None, ts, dm), lambda b, s: (b, s, 0)),
            pl.BlockSpec((None, SUBLANES, dm), lambda b, s: (b, jnp.maximum(s * (ts // SUBLANES) - 1, 0), 0)),
            pl.BlockSpec((N_SHIFT_MIX, dm), lambda b, s: (0, 0)),
        ],
        out_specs=pl.BlockSpec((N_SHIFT_MIX, ts, dm), lambda b, s: (0, b * nst + s, 0)),
        out_shape=jax.ShapeDtypeStruct((N_SHIFT_MIX, bsz * seq, dm), BF16),
        compiler_params=_params("parallel", "parallel"),
        name="token_shift_mix",
    )(x, x, mu)


def _mm_kernel(a_ref, w_ref, o_ref, wb_ref):
    @pl.when(pl.program_id(2) == 0)
    def _():
        wb_ref[...] = w_ref[...].astype(BF16)

    res = jnp.dot(a_ref[...], wb_ref[...], preferred_element_type=F32).astype(o_ref.dtype)
    if len(o_ref.shape) == 2:
        o_ref[...] = res
    else:
        for q in range(o_ref.shape[0]):
            o_ref[q] = res[:, q * LANES:(q + 1) * LANES]


def _batched_matmul(a, w, n_batch, out_dtype, tm=1024, tn=1024, rows_per_seq=None):
    _, m, kdim = a.shape
    tm = min(tm, m)
    n = w.shape[2]
    if rows_per_seq is None:
        out_spec = pl.BlockSpec((None, tm, tn), lambda c, j, i: (c, i, j))
        out_shape = jax.ShapeDtypeStruct((n_batch, m, n), out_dtype)
    else:
        tm = min(tm, rows_per_seq)
        per = rows_per_seq // tm
        out_spec = pl.BlockSpec((None, None, tn // LANES, tm, LANES), lambda c, j, i: (c, i // per, j, i % per, 0))
        out_shape = jax.ShapeDtypeStruct((n_batch, m // rows_per_seq, n // LANES, rows_per_seq, LANES), out_dtype)
    return pl.pallas_call(
        _mm_kernel,
        grid=(n_batch, n // tn, m // tm),
        in_specs=[
            pl.BlockSpec((None, tm, kdim), lambda c, j, i: (c, i, 0)),
            pl.BlockSpec((None, kdim, tn), lambda c, j, i: (c, 0, j)),
        ],
        out_specs=out_spec,
        out_shape=out_shape,
        scratch_shapes=[pltpu.VMEM((kdim, tn), BF16)],
        compiler_params=_params("parallel", "parallel", "arbitrary"),
        name="batched_matmul",
    )(a, w)


def _lora_kernel(xw_ref, xa_ref, xg_ref, w1_ref, w2_ref, w0_ref, a1_ref, a2_ref, a0_ref,
                 g1_ref, g2_ref, decay_ref, a_ref, g_ref):
    def two_stage(x_ref, p_ref, q_ref, act):
        h = act(jnp.dot(x_ref[...], p_ref[...], preferred_element_type=F32))
        return jnp.dot(h.astype(BF16), q_ref[...], preferred_element_type=F32)

    def split_lanes(o_ref, val):
        for q in range(o_ref.shape[0]):
            o_ref[q] = val[:, q * LANES:(q + 1) * LANES]

    z = w0_ref[...] + two_stage(xw_ref, w1_ref, w2_ref, jnp.tanh)
    split_lanes(decay_ref, jnp.exp(-jnp.exp(-0.5) * jax.nn.sigmoid(z)))
    split_lanes(a_ref, jax.nn.sigmoid(a0_ref[...] + two_stage(xa_ref, a1_ref, a2_ref, lambda h: h)))
    g_ref[...] = two_stage(xg_ref, g1_ref, g2_ref, jax.nn.sigmoid)


def _rwkv_lora(mixed, w1, w2, w0, a1, a2, a0, g1, g2, seq, tm=256):
    _, t, dm = mixed.shape
    tm = min(tm, seq)
    per = seq // tm

    def full(arr):
        return pl.BlockSpec(arr.shape, lambda i: (0,) * arr.ndim)

    def mix_spec(c):
        return pl.BlockSpec((None, tm, dm), lambda i: (c, i, 0))

    out_spec = pl.BlockSpec((tm, dm), lambda i: (i, 0))
    out = jax.ShapeDtypeStruct((t, dm), F32)
    split_spec = pl.BlockSpec((None, dm // LANES, tm, LANES), lambda i: (i // per, 0, i % per, 0))
    split = jax.ShapeDtypeStruct((t // seq, dm // LANES, seq, LANES), F32)
    return pl.pallas_call(
        _lora_kernel,
        grid=(t // tm,),
        in_specs=[mix_spec(3), mix_spec(4), mix_spec(5), full(w1), full(w2), full(w0),
                  full(a1), full(a2), full(a0), full(g1), full(g2)],
        out_specs=[split_spec, split_spec, out_spec],
        out_shape=[split, split, out],
        compiler_params=_params("parallel"),
        name="rwkv_lora",
    )(mixed, mixed, mixed, w1, w2, w0, a1, a2, a0, g1, g2)


SCAN_V_GROUPS = 4
SCAN_PARTIALS = 2


SCAN_BATCH = 4
SCAN_CHUNK = LANES // SCAN_BATCH


def _swap_halves(a0, a1, a2, a3, low_half):
    r0, r1, r2, r3 = (pltpu.roll(a, 2 * SCAN_CHUNK, 1) for a in (a0, a1, a2, a3))
    return (jnp.where(low_half, a0, r2), jnp.where(low_half, a1, r3),
            jnp.where(low_half, r0, a2), jnp.where(low_half, r1, a3))


def _swap_chunks(p0, p1, p2, p3, even_chunk):
    return (jnp.where(even_chunk, p0, pltpu.roll(p1, SCAN_CHUNK, 1)),
            jnp.where(even_chunk, pltpu.roll(p0, 3 * SCAN_CHUNK, 1), p1),
            jnp.where(even_chunk, p2, pltpu.roll(p3, SCAN_CHUNK, 1)),
            jnp.where(even_chunk, pltpu.roll(p2, 3 * SCAN_CHUNK, 1), p3))


def _chunk_transpose_direct(ins, at_chunk):
    outs = []
    for j in range(SCAN_BATCH):
        picked = [ins[b] if b == j else pltpu.roll(ins[b], ((b - j) % SCAN_BATCH) * SCAN_CHUNK, 1)
                  for b in range(SCAN_BATCH)]
        out = picked[SCAN_BATCH - 1]
        for b in range(SCAN_BATCH - 2, -1, -1):
            out = jnp.where(at_chunk[b], picked[b], out)
        outs.append(out)
    return tuple(outs)


Q_KK, Q_WR, Q_W, Q_B, Q_K2, Q_V = range(6)
R_INV2, R_BR, R_KR, R_BONUS = range(4)
SCAN_UNIT_Q = 2
SCAN_Q = RWKV_HEAD_SIZE // SCAN_BATCH
SCAN_STEP_UNROLL = 1


def _scan_pipelined_kernel(r_ref, k_ref, v_ref, w_ref, a_ref, kkp_ref, kap_ref, rkp_ref, gng_ref, gnb_ref,
                           y_ref, st_ref, buf_a, buf_b, rows_a, rows_b, y_s, yn_s, *, steps):
    n = RWKV_HEAD_SIZE
    n_vg = n // SUBLANES
    n_groups = steps // SUBLANES
    lane = lax.broadcasted_iota(jnp.int32, (SUBLANES, LANES), 1)
    low_half = lane < 2 * SCAN_CHUNK
    even_chunk = (lane & SCAN_CHUNK) == 0
    at_chunk = [(lane >= b * SCAN_CHUNK) & (lane < (b + 1) * SCAN_CHUNK) for b in range(SCAN_BATCH)]

    @pl.when(pl.program_id(0) == 0)
    def _():
        st_ref[...] = jnp.zeros_like(st_ref)
        yn_s[...] = jnp.zeros_like(yn_s)

    def tile_rows(i):
        if isinstance(i, int):
            return pl.ds(i * SUBLANES, SUBLANES)
        return pl.ds(pl.multiple_of(i * SUBLANES, SUBLANES), SUBLANES)

    def transposed(tiles_per_q, direct):
        if direct:
            return [_chunk_transpose_direct(t, at_chunk) for t in tiles_per_q]
        halves = [_swap_halves(*t, low_half) for t in tiles_per_q]
        return [_swap_chunks(*h, even_chunk) for h in halves]

    def regroup(s0, qs, dst, direct):
        srcs = (r_ref, k_ref, v_ref, w_ref, a_ref)
        tiles = [tuple(src[bi, q, pl.ds(s0, SUBLANES), :] for bi in range(SCAN_BATCH))
                 for q in qs for src in srcs]
        regrouped = transposed(tiles, direct)
        sums = None
        for qi, q in enumerate(qs):
            rq, kq, vq, wq, aq = regrouped[qi * len(srcs):(qi + 1) * len(srcs)]
            for j in range(SCAN_BATCH):
                rows = tile_rows(q * SCAN_BATCH + j)
                r, k, v, w, a = rq[j], kq[j], vq[j], wq[j], aq[j]
                kk0 = k * kkp_ref[rows, :]
                k2 = k * (1.0 + (a - 1.0) * kap_ref[rows, :])
                b0 = kk0 * a
                rk2 = r * k2
                dst[Q_KK, rows, :] = kk0
                dst[Q_WR, rows, :] = w * r
                dst[Q_W, rows, :] = w
                dst[Q_B, rows, :] = b0
                dst[Q_K2, rows, :] = k2
                dst[Q_V, rows, :] = v
                terms = (kk0 * kk0, b0 * r, rk2, rk2 * rkp_ref[rows, :])
                sums = terms if sums is None else tuple(x + y for x, y in zip(sums, terms))
        return sums

    def finish_rows(sums, rows_ref):
        ss, br, kr, bonus = sums
        rows_ref[R_INV2] = 1.0 / jnp.maximum(ss, 1e-24)
        rows_ref[R_BR] = br
        rows_ref[R_KR] = kr
        rows_ref[R_BONUS] = bonus

    def unregroup(s0, qs, direct):
        tiles = [tuple(yn_s[tile_rows(q * SCAN_BATCH + j), :] for j in range(SCAN_BATCH)) for q in qs]
        for q, outs in zip(qs, transposed(tiles, direct)):
            for bi, val in enumerate(outs):
                y_ref[bi, q, pl.ds(s0, SUBLANES), :] = val

    def step(t, cur, cur_rows):
        inv2_row = cur_rows[R_INV2, pl.ds(t, 1), :]
        br_row = cur_rows[R_BR, pl.ds(t, 1), :]
        kr_row = cur_rows[R_KR, pl.ds(t, 1), :]
        for g0 in range(0, n_vg, SCAN_V_GROUPS):
            groups = range(g0, g0 + SCAN_V_GROUPS)
            acc_sa = {g: [None] * SCAN_PARTIALS for g in groups}
            acc_y = {g: [None] * SCAN_PARTIALS for g in groups}
            for ki in range(n):
                kk_row = cur[Q_KK, pl.ds(ki * SUBLANES + t, 1), :]
                wr_row = cur[Q_WR, pl.ds(ki * SUBLANES + t, 1), :]
                p = ki % SCAN_PARTIALS
                for g in groups:
                    s = st_ref[ki, g * SUBLANES:(g + 1) * SUBLANES, :]
                    t_sa = s * kk_row
                    t_y = s * wr_row
                    acc_sa[g][p] = t_sa if acc_sa[g][p] is None else acc_sa[g][p] + t_sa
                    acc_y[g][p] = t_y if acc_y[g][p] is None else acc_y[g][p] + t_y
            sa = {}
            vv = {}
            for g in groups:
                v_rows = pl.ds(g * SUBLANES * SUBLANES + t, SUBLANES, stride=SUBLANES)
                sa[g] = -functools.reduce(lambda x, y: x + y, acc_sa[g]) * inv2_row
                vv[g] = cur[Q_V, v_rows, :]
                y_s[v_rows, :] = (functools.reduce(lambda x, y: x + y, acc_y[g])
                                  + sa[g] * br_row + vv[g] * kr_row)
            for ki in range(n):
                w_row = cur[Q_W, pl.ds(ki * SUBLANES + t, 1), :]
                b_row = cur[Q_B, pl.ds(ki * SUBLANES + t, 1), :]
                k_row = cur[Q_K2, pl.ds(ki * SUBLANES + t, 1), :]
                for g in groups:
                    s = st_ref[ki, g * SUBLANES:(g + 1) * SUBLANES, :]
                    st_ref[ki, g * SUBLANES:(g + 1) * SUBLANES, :] = s * w_row + sa[g] * b_row + vv[g] * k_row

    def normalise(cur, cur_rows):
        tot = None
        for vi in range(n):
            yv = y_s[vi * SUBLANES:(vi + 1) * SUBLANES, :]
            tot = yv if tot is None else tot + yv
        mean = tot * (1.0 / n)
        sq = None
        for vi in range(n):
            yc = y_s[vi * SUBLANES:(vi + 1) * SUBLANES, :] - mean
            sq = yc * yc if sq is None else sq + yc * yc
        rstd = lax.rsqrt(sq * (1.0 / n) + GN_EPS)
        bonus_v = cur_rows[R_BONUS]
        for vi in range(n):
            rows = slice(vi * SUBLANES, (vi + 1) * SUBLANES)
            yn_s[rows, :] = ((y_s[rows, :] - mean) * rstd * gng_ref[rows, :] + gnb_ref[rows, :]
                             + bonus_v * cur[Q_V, rows, :])

    def run_group(g, cur, cur_rows, nxt, nxt_rows):
        s0 = pl.multiple_of(g * SUBLANES, SUBLANES)
        s_next = pl.multiple_of(jnp.minimum(s0 + SUBLANES, steps - SUBLANES), SUBLANES)
        s_prev = pl.multiple_of(jnp.maximum(s0 - SUBLANES, 0), SUBLANES)

        def body(i, sums):
            for u in range(SCAN_STEP_UNROLL):
                t = SCAN_STEP_UNROLL * i + u
                qs = [SCAN_UNIT_Q * t + v for v in range(SCAN_UNIT_Q)]
                part = regroup(s_next, qs, nxt, True)
                unregroup(s_prev, qs, True)
                step(t, cur, cur_rows)
                sums = tuple(x + y for x, y in zip(sums, part))
            return sums

        zero = jnp.zeros((SUBLANES, LANES), F32)
        sums = lax.fori_loop(0, SUBLANES // SCAN_STEP_UNROLL, body, (zero, zero, zero, zero))
        finish_rows(sums, nxt_rows)
        normalise(cur, cur_rows)

    half = SCAN_Q // 2
    first = regroup(0, list(range(half)), buf_a, False)
    second = regroup(0, list(range(half, SCAN_Q)), buf_a, False)
    finish_rows(tuple(x + y for x, y in zip(first, second)), rows_a)

    def two_groups(i, carry):
        run_group(2 * i, buf_a, rows_a, buf_b, rows_b)
        run_group(2 * i + 1, buf_b, rows_b, buf_a, rows_a)
        return carry

    lax.fori_loop(0, n_groups // 2, two_groups, 0)

    unregroup(steps - SUBLANES, list(range(half)), False)
    unregroup(steps - SUBLANES, list(range(half, SCAN_Q)), False)


def _rwkv_scan_pipelined(rkv, w, a, kkp, kap, rkp, gng, gnb, steps=64):
    bsz, nq, seq, lanes = w.shape
    n = RWKV_HEAD_SIZE
    assert bsz == SCAN_BATCH and nq == SCAN_Q and lanes == LANES and (steps // SUBLANES) % 2 == 0

    def rkv_spec(c):
        return pl.BlockSpec((None, bsz, nq, steps, lanes), lambda i: (c, 0, 0, i, 0))

    seq_spec = pl.BlockSpec((bsz, nq, steps, lanes), lambda i: (0, 0, i, 0))
    par_spec = pl.BlockSpec((n * SUBLANES, LANES), lambda i: (0, 0))
    rows = pltpu.VMEM((n * SUBLANES, LANES), F32)
    group_buf = pltpu.VMEM((6, n * SUBLANES, LANES), F32)
    group_rows = pltpu.VMEM((4, SUBLANES, LANES), F32)
    return pl.pallas_call(
        functools.partial(_scan_pipelined_kernel, steps=steps),
        grid=(seq // steps,),
        in_specs=[rkv_spec(0), rkv_spec(1), rkv_spec(2), seq_spec, seq_spec] + [par_spec] * 5,
        out_specs=seq_spec,
        out_shape=jax.ShapeDtypeStruct((bsz, nq, seq, lanes), F32),
        scratch_shapes=[pltpu.VMEM((n, n, LANES), F32), group_buf, group_buf, group_rows, group_rows, rows, rows],
        compiler_params=_params("arbitrary"),
        name="rwkv7_scan",
    )(rkv, rkv, rkv, w, a, kkp, kap, rkp, gng, gnb)


def _mm_ln_kernel(*refs, gated, emit_bf16):
    refs = list(refs)
    a_ref = refs.pop(0)
    gate_ref = refs.pop(0) if gated else None
    w_ref, x_ref, g_ref, b_ref, o_ref = refs[:5]
    ob_ref = refs[5] if emit_bf16 else None
    acc_ref = refs[-1]

    if gated:
        a = jnp.concatenate([a_ref[q] for q in range(a_ref.shape[0])], axis=1)
        a = (a * gate_ref[...]).astype(BF16)
    else:
        a = a_ref[...]
    acc_ref[...] = jnp.dot(a, w_ref[...], preferred_element_type=F32)

    for c in range(acc_ref.shape[0] // LN_CHUNK_ROWS):
        rows = slice(c * LN_CHUNK_ROWS, (c + 1) * LN_CHUNK_ROWS)
        z = DEEPNORM_ALPHA * x_ref[rows, :] + acc_ref[rows, :]
        mu = jnp.mean(z, axis=-1, keepdims=True)
        zc = z - mu
        var = jnp.mean(zc * zc, axis=-1, keepdims=True)
        out = zc * lax.rsqrt(var + LN_EPS) * g_ref[...] + b_ref[...]
        o_ref[rows, :] = out
        if emit_bf16:
            ob_ref[rows, :] = out.astype(BF16)


def _matmul_residual_ln(a, w, x, g, b, *, gate=None, tm, emit_bf16=True):
    kdim, n = w.shape
    m = x.shape[0]
    row = lambda i: (i, 0)
    fixed = lambda i: (0, 0)
    args = [a]
    if gate is None:
        in_specs = [pl.BlockSpec((tm, kdim), row)]
    else:
        per = a.shape[2] // tm
        in_specs = [pl.BlockSpec((None, kdim // LANES, tm, LANES), lambda i: (i // per, 0, i % per, 0)),
                    pl.BlockSpec((tm, kdim), row)]
        args.append(gate)
    in_specs += [pl.BlockSpec((kdim, n), fixed, pipeline_mode=pl.Buffered(1)),
                 pl.BlockSpec((tm, n), row),
                 pl.BlockSpec((1, n), fixed),
                 pl.BlockSpec((1, n), fixed)]
    args += [w, x, g.reshape(1, n), b.reshape(1, n)]
    out_specs = [pl.BlockSpec((tm, n), row)]
    out_shape = [jax.ShapeDtypeStruct((m, n), F32)]
    if emit_bf16:
        out_specs.append(pl.BlockSpec((tm, n), row))
        out_shape.append(jax.ShapeDtypeStruct((m, n), BF16))
    outs = pl.pallas_call(
        functools.partial(_mm_ln_kernel, gated=gate is not None, emit_bf16=emit_bf16),
        grid=(m // tm,),
        in_specs=in_specs,
        out_specs=out_specs,
        out_shape=out_shape,
        scratch_shapes=[pltpu.VMEM((tm, n), F32)],
        compiler_params=_params("parallel"),
        name="matmul_residual_ln",
    )(*args)
    return outs if emit_bf16 else (outs[0], None)


FFN_COL_CHUNK = 256


def _ffn_up_kernel(x_ref, wg32_ref, wu32_ref, cw_ref, cb_ref, o_ref, tail_ref, wg_ref, wu_ref, *,
                   tiles_per_seq):
    i = pl.program_id(1)

    @pl.when(i == 0)
    def _():
        wg_ref[...] = wg32_ref[...].astype(BF16)
        wu_ref[...] = wu32_ref[...].astype(BF16)

    x = x_ref[...]
    tm = x.shape[0]
    seq_start = i % tiles_per_seq == 0
    row = lax.broadcasted_iota(jnp.int32, (SUBLANES, FFN_COL_CHUNK), 0)
    for c in range(o_ref.shape[1] // FFN_COL_CHUNK):
        cols = slice(c * FFN_COL_CHUNK, (c + 1) * FFN_COL_CHUNK)
        gate = jnp.dot(x, wg_ref[:, cols], preferred_element_type=F32)
        up = jnp.dot(x, wu_ref[:, cols], preferred_element_type=F32)
        tail = jnp.where(seq_start, 0.0, tail_ref[:, cols])
        tail_ref[:, cols] = gate[tm - SUBLANES:, :]
        r1 = pltpu.roll(gate, 1, 0)
        r2 = pltpu.roll(gate, 2, 0)
        h1 = jnp.where(row == 0, tail[SUBLANES - 1:, :], r1[:SUBLANES, :])
        h2 = jnp.where(row == 0, tail[SUBLANES - 2:SUBLANES - 1, :],
                       jnp.where(row == 1, tail[SUBLANES - 1:, :], r2[:SUBLANES, :]))
        g1 = jnp.concatenate([h1, r1[SUBLANES:, :]], axis=0)
        g2 = jnp.concatenate([h2, r2[SUBLANES:, :]], axis=0)
        acc = (cb_ref[:, cols] + g2 * cw_ref[0:1, cols] + g1 * cw_ref[1:2, cols] + gate * cw_ref[2:3, cols])
        o_ref[:, cols] = (acc * jax.nn.sigmoid(acc) * up).astype(o_ref.dtype)


def _ffn_up(xb, w_up_all, layer, conv_w, conv_b, seq, tm=1024, tn=512):
    t, dm = xb.shape
    dff = w_up_all.shape[2] // 2
    n_col = dff // tn
    w_up = w_up_all
    return pl.pallas_call(
        functools.partial(_ffn_up_kernel, tiles_per_seq=seq // tm),
        grid=(n_col, t // tm),
        in_specs=[
            pl.BlockSpec((tm, dm), lambda j, i: (i, 0)),
            pl.BlockSpec((None, dm, tn), lambda j, i: (layer, 0, j)),
            pl.BlockSpec((None, dm, tn), lambda j, i: (layer, 0, j + n_col)),
            pl.BlockSpec((CONV_WIDTH, tn), lambda j, i: (0, j)),
            pl.BlockSpec((1, tn), lambda j, i: (0, j)),
        ],
        out_specs=pl.BlockSpec((tm, tn), lambda j, i: (i, j)),
        out_shape=jax.ShapeDtypeStruct((t, dff), BF16),
        scratch_shapes=[pltpu.VMEM((SUBLANES, tn), F32), pltpu.VMEM((dm, tn), BF16), pltpu.VMEM((dm, tn), BF16)],
        compiler_params=_params("parallel", "arbitrary"),
        name="ffn_up_convglu",
    )(xb, w_up, w_up, conv_w, conv_b.reshape(1, dff))


ATTN_SPAN = 128
ATTN_TILE = 2048
ATTN_MERGE_ROWS = 256


def _block_attention(q, k, v, valid):
    s = lax.dot_general(q.astype(BF16), k.astype(BF16), (((1,), (1,)), ((), ())),
                        preferred_element_type=F32) * (ATTN_HEAD_DIM ** -0.5)
    s = jnp.where(valid, s, MASK_VALUE)
    m = jnp.max(s, axis=-1, keepdims=True)
    p = jnp.exp(s - m)
    l = jnp.sum(p, axis=-1, keepdims=True)
    o = jnp.dot(p.astype(BF16), v.astype(BF16), preferred_element_type=F32) * (1.0 / l)
    return o, m + jnp.log(l)


def _attn_kernel(q1, k1, v1, kp1, vp1, q4, k4, v4, kp4, vp4, q16, k16, v16, kp16, vp16,
                 o_ref, og, lg, x4, *, tiles_per_seq):
    span, hd = ATTN_SPAN, ATTN_HEAD_DIM
    first_tile = pl.program_id(1) % tiles_per_seq == 0
    qi = lax.broadcasted_iota(jnp.int32, (span, 2 * span), 0)
    kj = lax.broadcasted_iota(jnp.int32, (span, 2 * span), 1)
    valid = (kj <= qi + span) & (kj >= qi)
    valid_edge = valid & (kj >= jnp.where(first_tile, span, 0))

    def emit(g, rows, q, k, v, mask):
        o, lse = _block_attention(q, k, v, mask)
        og[g, rows, :] = o
        lg[g, rows, :] = jnp.broadcast_to(lse, (span, hd))

    def cat(a, b):
        return jnp.concatenate([a, b], axis=0)

    n1 = ATTN_TILE // span
    rows = pl.ds(0, span)
    emit(0, rows, q1[rows, :], cat(kp1[...], k1[rows, :]), cat(vp1[...], v1[rows, :]), valid_edge)

    for j in range(1, n1):
        rows = pl.ds(j * span, span)
        keys = pl.ds((j - 1) * span, 2 * span)
        emit(0, rows, q1[rows, :], k1[keys, :], v1[keys, :], valid)

    d4 = 4
    nb4 = ATTN_TILE // (span * d4)
    for r in range(d4):
        rows = pl.ds(r, span, stride=d4)
        emit(1, rows, q4[rows, :], cat(kp4[rows, :], k4[rows, :]), cat(vp4[rows, :], v4[rows, :]), valid_edge)
        for j in range(1, nb4):
            rows = pl.ds(j * span * d4 + r, span, stride=d4)
            keys = pl.ds((j - 1) * span * d4 + r, 2 * span, stride=d4)
            emit(1, rows, q4[rows, :], k4[keys, :], v4[keys, :], valid)

    d16 = 16
    quarter = ATTN_TILE // d4
    for a, src in enumerate((q16, k16, v16, kp16, vp16)):
        for r4 in range(d4):
            x4[a, r4 * quarter:(r4 + 1) * quarter, :] = src[pl.ds(r4, quarter, stride=d4), :]
    for r in range(d16):
        sub = pl.ds((r % d4) * quarter + r // d4, span, stride=d4)
        o, lse = _block_attention(x4[0, sub, :], cat(x4[3, sub, :], x4[1, sub, :]),
                                  cat(x4[4, sub, :], x4[2, sub, :]), valid_edge)
        x4[0, sub, :] = o
        x4[3, sub, :] = jnp.broadcast_to(lse, (span, hd))
    for r4 in range(d4):
        og[2, pl.ds(r4, quarter, stride=d4), :] = x4[0, r4 * quarter:(r4 + 1) * quarter, :]
        lg[2, pl.ds(r4, quarter, stride=d4), :] = x4[3, r4 * quarter:(r4 + 1) * quarter, :]

    def merge(c, carry):
        rows = pl.ds(pl.multiple_of(c * ATTN_MERGE_ROWS, ATTN_MERGE_ROWS), ATTN_MERGE_ROWS)
        l0, l1, l2 = lg[0, rows, :], lg[1, rows, :], lg[2, rows, :]
        m = jnp.maximum(jnp.maximum(l0, l1), l2)
        e0, e1, e2 = jnp.exp(l0 - m), jnp.exp(l1 - m), jnp.exp(l2 - m)
        num = e0 * og[0, rows, :] + e1 * og[1, rows, :] + e2 * og[2, rows, :]
        o_ref[rows, :] = (num / (e0 + e1 + e2)).astype(o_ref.dtype)
        return carry

    lax.fori_loop(0, ATTN_TILE // ATTN_MERGE_ROWS, merge, 0)


def _dilated_attention(qkv, seq):
    t, width = qkv.shape
    hd, span, tile = ATTN_HEAD_DIM, ATTN_SPAN, ATTN_TILE
    heads = width // (3 * len(DILATED_GROUPS) * hd)
    tiles_per_seq = seq // tile

    def col(group, which):
        return lambda h: (group * 3 + which) * heads + h

    def cur(group, which):
        return pl.BlockSpec((tile, hd), lambda h, i: (i, col(group, which)(h)))

    def prev(group, which, rows):
        per = tile // rows
        return pl.BlockSpec((rows, hd), lambda h, i: (jnp.maximum(i * per - 1, 0), col(group, which)(h)))

    in_specs = []
    for group, (window, dilation) in enumerate(DILATED_GROUPS):
        assert window // dilation == span and tile % window == 0
        in_specs += [cur(group, 0), cur(group, 1), cur(group, 2), prev(group, 1, window), prev(group, 2, window)]
    return pl.pallas_call(
        functools.partial(_attn_kernel, tiles_per_seq=tiles_per_seq),
        grid=(heads, t // tile),
        in_specs=in_specs,
        out_specs=pl.BlockSpec((tile, hd), lambda h, i: (i, h)),
        out_shape=jax.ShapeDtypeStruct((t, heads * hd), BF16),
        scratch_shapes=[pltpu.VMEM((len(DILATED_GROUPS), tile, hd), F32)] * 2 + [pltpu.VMEM((5, tile, hd), F32)],
        compiler_params=_params("parallel", "parallel"),
        name="dilated_attention",
    )(*([qkv] * len(in_specs)))


def _pad_rank(p, q):
    r = p.shape[1]
    if r % LORA_PAD:
        extra = LORA_PAD - r % LORA_PAD
        p = jnp.pad(p, ((0, 0), (0, extra)))
        q = jnp.pad(q, ((0, extra), (0, 0)))
    return p.astype(BF16), q.astype(BF16)


def _rwkv_layer(x, mu, w_rkv, w0, w1, w2, a0, a1, a2, g1, g2, k_k, k_a, r_k, gn_g, gn_b, w_out, ln_g, ln_b):
    bsz, seq, dm = x.shape
    n = RWKV_HEAD_SIZE
    heads = dm // n
    t = bsz * seq
    def cols(wt):
        lead = wt.shape[:-1]
        return wt.reshape(*lead, heads, n).swapaxes(-1, -2).reshape(*lead, dm)

    mixed = _token_shift_mix(x, mu[jnp.array([0, 2, 3, 1, 4, 5])])
    rkv = _batched_matmul(mixed, cols(w_rkv).astype(BF16), 3, F32, rows_per_seq=seq)
    w1b, w2b = _pad_rank(w1, cols(w2))
    a1b, a2b = _pad_rank(a1, cols(a2))
    g1b, g2b = _pad_rank(g1, cols(g2))
    decay, a, g = _rwkv_lora(mixed, w1b, w2b, cols(w0).reshape(1, dm), a1b, a2b, cols(a0).reshape(1, dm),
                             g1b, g2b, seq)

    def lanes_param(p):
        return jnp.repeat(jnp.tile(p.reshape(heads, n).T, (1, bsz)), SUBLANES, axis=0)

    y = _rwkv_scan_pipelined(rkv, decay, a, lanes_param(k_k), lanes_param(k_a), lanes_param(r_k),
                             lanes_param(gn_g), lanes_param(gn_b), steps=min(64, seq))
    w_out_rows = w_out.reshape(heads, n, dm).swapaxes(0, 1).reshape(dm, dm)
    return _matmul_residual_ln(y, w_out_rows.astype(BF16), x.reshape(t, dm), ln_g, ln_b, gate=g,
                               tm=min(256, seq))


def _attn_layer(x32, xb, w_in, w_out, ln_g, ln_b, bsz, seq):
    t, dm = x32.shape
    qkv = _batched_matmul(xb[None], w_in[None], 1, F32)[0]
    merged = _dilated_attention(qkv, seq)
    return _matmul_residual_ln(merged, w_out.astype(BF16), x32, ln_g, ln_b, tm=512)


def _ffn_layer(x32, xb, w_up_all, layer, conv_w, conv_b, w_down, ln_g, ln_b, seq, emit_bf16):
    act = _ffn_up(xb, w_up_all, layer, conv_w, conv_b, seq)
    return _matmul_residual_ln(act, w_down.astype(BF16), x32, ln_g, ln_b, tm=256, emit_bf16=emit_bf16)


def kernel(x, rwkv_mu, rwkv_w_rkv, rwkv_w0, rwkv_w1, rwkv_w2, rwkv_a0, rwkv_a1, rwkv_a2, rwkv_g1, rwkv_g2, rwkv_k_k, rwkv_k_a, rwkv_r_k, rwkv_gn_g, rwkv_gn_b, rwkv_w_out, attn_w_in, attn_w_out, ffn_w_up, ffn_conv_w, ffn_conv_b, ffn_w_down, ln_mix_g, ln_mix_b, ln_ffn_g, ln_ffn_b):
    bsz, seq, dm = x.shape
    x32, xb = _rwkv_layer(x, rwkv_mu[0], rwkv_w_rkv[0], rwkv_w0[0], rwkv_w1[0], rwkv_w2[0], rwkv_a0[0],
                          rwkv_a1[0], rwkv_a2[0], rwkv_g1[0], rwkv_g2[0], rwkv_k_k[0], rwkv_k_a[0],
                          rwkv_r_k[0], rwkv_gn_g[0], rwkv_gn_b[0], rwkv_w_out[0], ln_mix_g[0], ln_mix_b[0])
    x32, xb = _ffn_layer(x32, xb, ffn_w_up, 0, ffn_conv_w[0], ffn_conv_b[0], ffn_w_down[0],
                         ln_ffn_g[0], ln_ffn_b[0], seq, True)
    x32, xb = _attn_layer(x32, xb, attn_w_in[0], attn_w_out[0], ln_mix_g[1], ln_mix_b[1], bsz, seq)
    x32, _ = _ffn_layer(x32, xb, ffn_w_up, 1, ffn_conv_w[1], ffn_conv_b[1], ffn_w_down[1],
                        ln_ffn_g[1], ln_ffn_b[1], seq, False)
    return x32.reshape(bsz, seq, dm)
```

```python
import functools

import jax
import jax.numpy as jnp
from jax import lax
from jax.experimental import pallas as pl
from jax.experimental.pallas import tpu as pltpu

F32 = jnp.float32
BF16 = jnp.bfloat16

V7X_VMEM_BYTES = 64 * 1024 * 1024
VMEM_LIMIT_BYTES = V7X_VMEM_BYTES - 8 * 1024 * 1024
SUBLANES = 8
LANES = 128

RWKV_HEAD_SIZE = 64
N_SHIFT_MIX = 6
GN_EPS = 64e-5
ATTN_HEAD_DIM = 128
DILATED_GROUPS = ((128, 1), (512, 4), (2048, 16))
MASK_VALUE = -1e30
CONV_WIDTH = 3
LN_EPS = 1e-5
DEPTH = 2
DEEPNORM_ALPHA = (2.0 * DEPTH) ** 0.25
LORA_PAD = 128
LN_CHUNK_ROWS = 32
MM_LN_PANEL_ROWS = 256


def _params(*semantics):
    return pltpu.CompilerParams(dimension_semantics=semantics, vmem_limit_bytes=VMEM_LIMIT_BYTES)


def _mix_kernel(x_ref, xp_ref, mu_ref, o_ref):
    s = pl.program_id(1)
    x = x_ref[...]
    before = jnp.where(s > 0, xp_ref[SUBLANES - 1:SUBLANES, :], 0.0)
    prev = pltpu.roll(x, 1, 0)
    row = lax.broadcasted_iota(jnp.int32, x.shape, 0)
    prev = jnp.where(row == 0, before, prev)
    xx = prev - x
    for c in range(N_SHIFT_MIX):
        o_ref[c] = (x + xx * mu_ref[c:c + 1, :]).astype(o_ref.dtype)


def _token_shift_mix(x, mu, ts=256):
    bsz, seq, dm = x.shape
    ts = min(ts, seq)
    nst = seq // ts
    return pl.pallas_call(
        _mix_kernel,
        grid=(bsz, nst),
        in_specs=[
            pl.BlockSpec((None, ts, dm), lambda b, s: (b, s, 0)),
            pl.BlockSpec((None, SUBLANES, dm), lambda b, s: (b, jnp.maximum(s * (ts // SUBLANES) - 1, 0), 0)),
            pl.BlockSpec((N_SHIFT_MIX, dm), lambda b, s: (0, 0)),
        ],
        out_specs=pl.BlockSpec((N_SHIFT_MIX, ts, dm), lambda b, s: (0, b * nst + s, 0)),
        out_shape=jax.ShapeDtypeStruct((N_SHIFT_MIX, bsz * seq, dm), BF16),
        compiler_params=_params("parallel", "parallel"),
        name="token_shift_mix",
    )(x, x, mu)


def _mm_kernel(a_ref, w_ref, o_ref, wb_ref):
    @pl.when(pl.program_id(2) == 0)
    def _():
        wb_ref[...] = w_ref[...].astype(BF16)

    res = jnp.dot(a_ref[...], wb_ref[...], preferred_element_type=F32).astype(o_ref.dtype)
    if len(o_ref.shape) == 2:
        o_ref[...] = res
    else:
        for q in range(o_ref.shape[0]):
            o_ref[q] = res[:, q * LANES:(q + 1) * LANES]


def _batched_matmul(a, w, n_batch, out_dtype, tm=1024, tn=1024, rows_per_seq=None):
    _, m, kdim = a.shape
    tm = min(tm, m)
    n = w.shape[2]
    if rows_per_seq is None:
        out_spec = pl.BlockSpec((None, tm, tn), lambda c, j, i: (c, i, j))
        out_shape = jax.ShapeDtypeStruct((n_batch, m, n), out_dtype)
    else:
        tm = min(tm, rows_per_seq)
        per = rows_per_seq // tm
        out_spec = pl.BlockSpec((None, None, tn // LANES, tm, LANES), lambda c, j, i: (c, i // per, j, i % per, 0))
        out_shape = jax.ShapeDtypeStruct((n_batch, m // rows_per_seq, n // LANES, rows_per_seq, LANES), out_dtype)
    return pl.pallas_call(
        _mm_kernel,
        grid=(n_batch, n // tn, m // tm),
        in_specs=[
            pl.BlockSpec((None, tm, kdim), lambda c, j, i: (c, i, 0)),
            pl.BlockSpec((None, kdim, tn), lambda c, j, i: (c, 0, j)),
        ],
        out_specs=out_spec,
        out_shape=out_shape,
        scratch_shapes=[pltpu.VMEM((kdim, tn), BF16)],
        compiler_params=_params("parallel", "parallel", "arbitrary"),
        name="batched_matmul",
    )(a, w)


def _lora_kernel(xw_ref, xa_ref, xg_ref, w1_ref, w2_ref, w0_ref, a1_ref, a2_ref, a0_ref,
                 g1_ref, g2_ref, decay_ref, a_ref, g_ref):
    def two_stage(x_ref, p_ref, q_ref, act):
        h = act(jnp.dot(x_ref[...], p_ref[...], preferred_element_type=F32))
        return jnp.dot(h.astype(BF16), q_ref[...], preferred_element_type=F32)

    def split_lanes(o_ref, val):
        for q in range(o_ref.shape[0]):
            o_ref[q] = val[:, q * LANES:(q + 1) * LANES]

    z = w0_ref[...] + two_stage(xw_ref, w1_ref, w2_ref, jnp.tanh)
    split_lanes(decay_ref, jnp.exp(-jnp.exp(-0.5) * jax.nn.sigmoid(z)))
    split_lanes(a_ref, jax.nn.sigmoid(a0_ref[...] + two_stage(xa_ref, a1_ref, a2_ref, lambda h: h)))
    g_ref[...] = two_stage(xg_ref, g1_ref, g2_ref, jax.nn.sigmoid)


def _rwkv_lora(mixed, w1, w2, w0, a1, a2, a0, g1, g2, seq, tm=512):
    _, t, dm = mixed.shape
    tm = min(tm, seq)
    per = seq // tm

    def full(arr):
        return pl.BlockSpec(arr.shape, lambda i: (0,) * arr.ndim)

    def mix_spec(c):
        return pl.BlockSpec((None, tm, dm), lambda i: (c, i, 0))

    out_spec = pl.BlockSpec((tm, dm), lambda i: (i, 0))
    out = jax.ShapeDtypeStruct((t, dm), F32)
    split_spec = pl.BlockSpec((None, dm // LANES, tm, LANES), lambda i: (i // per, 0, i % per, 0))
    split = jax.ShapeDtypeStruct((t // seq, dm // LANES, seq, LANES), F32)
    return pl.pallas_call(
        _lora_kernel,
        grid=(t // tm,),
        in_specs=[mix_spec(3), mix_spec(4), mix_spec(5), full(w1), full(w2), full(w0),
                  full(a1), full(a2), full(a0), full(g1), full(g2)],
        out_specs=[split_spec, split_spec, out_spec],
        out_shape=[split, split, out],
        compiler_params=_params("parallel"),
        name="rwkv_lora",
    )(mixed, mixed, mixed, w1, w2, w0, a1, a2, a0, g1, g2)


SCAN_V_GROUPS = 4
SCAN_PARTIALS = 2


SCAN_BATCH = 4
SCAN_CHUNK = LANES // SCAN_BATCH


def _swap_halves(a0, a1, a2, a3, low_half):
    r0, r1, r2, r3 = (pltpu.roll(a, 2 * SCAN_CHUNK, 1) for a in (a0, a1, a2, a3))
    return (jnp.where(low_half, a0, r2), jnp.where(low_half, a1, r3),
            jnp.where(low_half, r0, a2), jnp.where(low_half, r1, a3))


def _swap_chunks(p0, p1, p2, p3, even_chunk):
    return (jnp.where(even_chunk, p0, pltpu.roll(p1, SCAN_CHUNK, 1)),
            jnp.where(even_chunk, pltpu.roll(p0, 3 * SCAN_CHUNK, 1), p1),
            jnp.where(even_chunk, p2, pltpu.roll(p3, SCAN_CHUNK, 1)),
            jnp.where(even_chunk, pltpu.roll(p2, 3 * SCAN_CHUNK, 1), p3))


def _chunk_transpose_direct(ins, at_chunk):
    outs = []
    for j in range(SCAN_BATCH):
        picked = [ins[b] if b == j else pltpu.roll(ins[b], ((b - j) % SCAN_BATCH) * SCAN_CHUNK, 1)
                  for b in range(SCAN_BATCH)]
        out = picked[SCAN_BATCH - 1]
        for b in range(SCAN_BATCH - 2, -1, -1):
            out = jnp.where(at_chunk[b], picked[b], out)
        outs.append(out)
    return tuple(outs)


Q_KK, Q_WR, Q_W, Q_B, Q_K2, Q_V = range(6)
R_INV2, R_BR, R_KR, R_BONUS = range(4)
SCAN_UNIT_Q = 2
SCAN_Q = RWKV_HEAD_SIZE // SCAN_BATCH
SCAN_STEP_UNROLL = 1


def _scan_pipelined_kernel(r_ref, k_ref, v_ref, w_ref, a_ref, kkp_ref, kap_ref, rkp_ref, gng_ref, gnb_ref,
                           y_ref, st_ref, buf_a, buf_b, rows_a, rows_b, y_s, yn_s, *, steps):
    n = RWKV_HEAD_SIZE
    n_vg = n // SUBLANES
    n_groups = steps // SUBLANES
    lane = lax.broadcasted_iota(jnp.int32, (SUBLANES, LANES), 1)
    low_half = lane < 2 * SCAN_CHUNK
    even_chunk = (lane & SCAN_CHUNK) == 0
    at_chunk = [(lane >= b * SCAN_CHUNK) & (lane < (b + 1) * SCAN_CHUNK) for b in range(SCAN_BATCH)]

    @pl.when(pl.program_id(0) == 0)
    def _():
        st_ref[...] = jnp.zeros_like(st_ref)
        yn_s[...] = jnp.zeros_like(yn_s)

    def tile_rows(i):
        if isinstance(i, int):
            return pl.ds(i * SUBLANES, SUBLANES)
        return pl.ds(pl.multiple_of(i * SUBLANES, SUBLANES), SUBLANES)

    def transposed(tiles_per_q, direct):
        if direct:
            return [_chunk_transpose_direct(t, at_chunk) for t in tiles_per_q]
        halves = [_swap_halves(*t, low_half) for t in tiles_per_q]
        return [_swap_chunks(*h, even_chunk) for h in halves]

    def regroup(s0, qs, dst, direct):
        srcs = (r_ref, k_ref, v_ref, w_ref, a_ref)
        tiles = [tuple(src[bi, q, pl.ds(s0, SUBLANES), :] for bi in range(SCAN_BATCH))
                 for q in qs for src in srcs]
        regrouped = transposed(tiles, direct)
        sums = None
        for qi, q in enumerate(qs):
            rq, kq, vq, wq, aq = regrouped[qi * len(srcs):(qi + 1) * len(srcs)]
            for j in range(SCAN_BATCH):
                rows = tile_rows(q * SCAN_BATCH + j)
                r, k, v, w, a = rq[j], kq[j], vq[j], wq[j], aq[j]
                kk0 = k * kkp_ref[rows, :]
                k2 = k * (1.0 + (a - 1.0) * kap_ref[rows, :])
                b0 = kk0 * a
                rk2 = r * k2
                dst[Q_KK, rows, :] = kk0
                dst[Q_WR, rows, :] = w * r
                dst[Q_W, rows, :] = w
                dst[Q_B, rows, :] = b0
                dst[Q_K2, rows, :] = k2
                dst[Q_V, rows, :] = v
                terms = (kk0 * kk0, b0 * r, rk2, rk2 * rkp_ref[rows, :])
                sums = terms if sums is None else tuple(x + y for x, y in zip(sums, terms))
        return sums

    def finish_rows(sums, rows_ref):
        ss, br, kr, bonus = sums
        rows_ref[R_INV2] = 1.0 / jnp.maximum(ss, 1e-24)
        rows_ref[R_BR] = br
        rows_ref[R_KR] = kr
        rows_ref[R_BONUS] = bonus

    def unregroup(s0, qs, direct):
        tiles = [tuple(yn_s[tile_rows(q * SCAN_BATCH + j), :] for j in range(SCAN_BATCH)) for q in qs]
        for q, outs in zip(qs, transposed(tiles, direct)):
            for bi, val in enumerate(outs):
                y_ref[bi, q, pl.ds(s0, SUBLANES), :] = val

    def step(t, cur, cur_rows):
        inv2_row = cur_rows[R_INV2, pl.ds(t, 1), :]
        br_row = cur_rows[R_BR, pl.ds(t, 1), :]
        kr_row = cur_rows[R_KR, pl.ds(t, 1), :]
        for g0 in range(0, n_vg, SCAN_V_GROUPS):
            groups = range(g0, g0 + SCAN_V_GROUPS)
            acc_sa = {g: [None] * SCAN_PARTIALS for g in groups}
            acc_y = {g: [None] * SCAN_PARTIALS for g in groups}
            for ki in range(n):
                kk_row = cur[Q_KK, pl.ds(ki * SUBLANES + t, 1), :]
                wr_row = cur[Q_WR, pl.ds(ki * SUBLANES + t, 1), :]
                p = ki % SCAN_PARTIALS
                for g in groups:
                    s = st_ref[ki, g * SUBLANES:(g + 1) * SUBLANES, :]
                    t_sa = s * kk_row
                    t_y = s * wr_row
                    acc_sa[g][p] = t_sa if acc_sa[g][p] is None else acc_sa[g][p] + t_sa
                    acc_y[g][p] = t_y if acc_y[g][p] is None else acc_y[g][p] + t_y
            sa = {}
            vv = {}
            for g in groups:
                v_rows = pl.ds(g * SUBLANES * SUBLANES + t, SUBLANES, stride=SUBLANES)
                sa[g] = -functools.reduce(lambda x, y: x + y, acc_sa[g]) * inv2_row
                vv[g] = cur[Q_V, v_rows, :]
                y_s[v_rows, :] = (functools.reduce(lambda x, y: x + y, acc_y[g])
                                  + sa[g] * br_row + vv[g] * kr_row)
            for ki in range(n):
                w_row = cur[Q_W, pl.ds(ki * SUBLANES + t, 1), :]
                b_row = cur[Q_B, pl.ds(ki * SUBLANES + t, 1), :]
                k_row = cur[Q_K2, pl.ds(ki * SUBLANES + t, 1), :]
                for g in groups:
                    s = st_ref[ki, g * SUBLANES:(g + 1) * SUBLANES, :]
                    st_ref[ki, g * SUBLANES:(g + 1) * SUBLANES, :] = s * w_row + sa[g] * b_row + vv[g] * k_row

    def normalise(cur, cur_rows):
        tot = None
        for vi in range(n):
            yv = y_s[vi * SUBLANES:(vi + 1) * SUBLANES, :]
            tot = yv if tot is None else tot + yv
        mean = tot * (1.0 / n)
        sq = None
        for vi in range(n):
            yc = y_s[vi * SUBLANES:(vi + 1) * SUBLANES, :] - mean
            sq = yc * yc if sq is None else sq + yc * yc
        rstd = lax.rsqrt(sq * (1.0 / n) + GN_EPS)
        bonus_v = cur_rows[R_BONUS]
        for vi in range(n):
            rows = slice(vi * SUBLANES, (vi + 1) * SUBLANES)
            yn_s[rows, :] = ((y_s[rows, :] - mean) * rstd * gng_ref[rows, :] + gnb_ref[rows, :]
                             + bonus_v * cur[Q_V, rows, :])

    def run_group(g, cur, cur_rows, nxt, nxt_rows):
        s0 = pl.multiple_of(g * SUBLANES, SUBLANES)
        s_next = pl.multiple_of(jnp.minimum(s0 + SUBLANES, steps - SUBLANES), SUBLANES)
        s_prev = pl.multiple_of(jnp.maximum(s0 - SUBLANES, 0), SUBLANES)

        def body(i, sums):
            for u in range(SCAN_STEP_UNROLL):
                t = SCAN_STEP_UNROLL * i + u
                qs = [SCAN_UNIT_Q * t + v for v in range(SCAN_UNIT_Q)]
                part = regroup(s_next, qs, nxt, True)
                unregroup(s_prev, qs, True)
                step(t, cur, cur_rows)
                sums = tuple(x + y for x, y in zip(sums, part))
            return sums

        zero = jnp.zeros((SUBLANES, LANES), F32)
        sums = lax.fori_loop(0, SUBLANES // SCAN_STEP_UNROLL, body, (zero, zero, zero, zero))
        finish_rows(sums, nxt_rows)
        normalise(cur, cur_rows)

    half = SCAN_Q // 2
    first = regroup(0, list(range(half)), buf_a, False)
    second = regroup(0, list(range(half, SCAN_Q)), buf_a, False)
    finish_rows(tuple(x + y for x, y in zip(first, second)), rows_a)

    def two_groups(i, carry):
        run_group(2 * i, buf_a, rows_a, buf_b, rows_b)
        run_group(2 * i + 1, buf_b, rows_b, buf_a, rows_a)
        return carry

    lax.fori_loop(0, n_groups // 2, two_groups, 0)

    unregroup(steps - SUBLANES, list(range(half)), False)
    unregroup(steps - SUBLANES, list(range(half, SCAN_Q)), False)


def _rwkv_scan_pipelined(rkv, w, a, kkp, kap, rkp, gng, gnb, steps=64):
    bsz, nq, seq, lanes = w.shape
    n = RWKV_HEAD_SIZE
    assert bsz == SCAN_BATCH and nq == SCAN_Q and lanes == LANES and (steps // SUBLANES) % 2 == 0

    def rkv_spec(c):
        return pl.BlockSpec((None, bsz, nq, steps, lanes), lambda i: (c, 0, 0, i, 0))

    seq_spec = pl.BlockSpec((bsz, nq, steps, lanes), lambda i: (0, 0, i, 0))
    par_spec = pl.BlockSpec((n * SUBLANES, LANES), lambda i: (0, 0))
    rows = pltpu.VMEM((n * SUBLANES, LANES), F32)
    group_buf = pltpu.VMEM((6, n * SUBLANES, LANES), F32)
    group_rows = pltpu.VMEM((4, SUBLANES, LANES), F32)
    return pl.pallas_call(
        functools.partial(_scan_pipelined_kernel, steps=steps),
        grid=(seq // steps,),
        in_specs=[rkv_spec(0), rkv_spec(1), rkv_spec(2), seq_spec, seq_spec] + [par_spec] * 5,
        out_specs=seq_spec,
        out_shape=jax.ShapeDtypeStruct((bsz, nq, seq, lanes), F32),
        scratch_shapes=[pltpu.VMEM((n, n, LANES), F32), group_buf, group_buf, group_rows, group_rows, rows, rows],
        compiler_params=_params("arbitrary"),
        name="rwkv7_scan",
    )(rkv, rkv, rkv, w, a, kkp, kap, rkp, gng, gnb)


def _mm_ln_kernel(*refs, gated, emit_bf16):
    refs = list(refs)
    a_ref = refs.pop(0)
    gate_ref = refs.pop(0) if gated else None
    w_ref, x_ref, g_ref, b_ref, o_ref = refs[:5]
    ob_ref = refs[5] if emit_bf16 else None
    acc_ref = refs[-1]

    if gated:
        a = jnp.concatenate([a_ref[q] for q in range(a_ref.shape[0])], axis=1)
        a = (a * gate_ref[...]).astype(BF16)
    else:
        a = a_ref[...]

    panel_rows = min(MM_LN_PANEL_ROWS, acc_ref.shape[0])
    for p in range(0, acc_ref.shape[0], panel_rows):
        panel = slice(p, p + panel_rows)
        acc_ref[panel, :] = jnp.dot(a[panel, :], w_ref[...], preferred_element_type=F32)
        for c in range(p // LN_CHUNK_ROWS, (p + panel_rows) // LN_CHUNK_ROWS):
            rows = slice(c * LN_CHUNK_ROWS, (c + 1) * LN_CHUNK_ROWS)
            z = DEEPNORM_ALPHA * x_ref[rows, :] + acc_ref[rows, :]
            mu = jnp.mean(z, axis=-1, keepdims=True)
            zc = z - mu
            var = jnp.mean(zc * zc, axis=-1, keepdims=True)
            out = zc * lax.rsqrt(var + LN_EPS) * g_ref[...] + b_ref[...]
            o_ref[rows, :] = out
            if emit_bf16:
                ob_ref[rows, :] = out.astype(BF16)


def _matmul_residual_ln(a, w, x, g, b, *, gate=None, tm, emit_bf16=True):
    kdim, n = w.shape
    m = x.shape[0]
    row = lambda i: (i, 0)
    fixed = lambda i: (0, 0)
    args = [a]
    if gate is None:
        in_specs = [pl.BlockSpec((tm, kdim), row)]
    else:
        per = a.shape[2] // tm
        in_specs = [pl.BlockSpec((None, kdim // LANES, tm, LANES), lambda i: (i // per, 0, i % per, 0)),
                    pl.BlockSpec((tm, kdim), row)]
        args.append(gate)
    in_specs += [pl.BlockSpec((kdim, n), fixed, pipeline_mode=pl.Buffered(1)),
                 pl.BlockSpec((tm, n), row),
                 pl.BlockSpec((1, n), fixed),
                 pl.BlockSpec((1, n), fixed)]
    args += [w, x, g.reshape(1, n), b.reshape(1, n)]
    out_specs = [pl.BlockSpec((tm, n), row)]
    out_shape = [jax.ShapeDtypeStruct((m, n), F32)]
    if emit_bf16:
        out_specs.append(pl.BlockSpec((tm, n), row))
        out_shape.append(jax.ShapeDtypeStruct((m, n), BF16))
    outs = pl.pallas_call(
        functools.partial(_mm_ln_kernel, gated=gate is not None, emit_bf16=emit_bf16),
        grid=(m // tm,),
        in_specs=in_specs,
        out_specs=out_specs,
        out_shape=out_shape,
        scratch_shapes=[pltpu.VMEM((tm, n), F32)],
        compiler_params=_params("parallel"),
        name="matmul_residual_ln",
    )(*args)
    return outs if emit_bf16 else (outs[0], None)


FFN_COL_CHUNK = 256
FFN_PANEL_ROWS = 1024


def _ffn_up_kernel(x_ref, wg32_ref, wu32_ref, cw_ref, cb_ref, o_ref, tail_ref, wg_ref, wu_ref, *,
                   tiles_per_seq):
    i = pl.program_id(1)

    @pl.when(i == 0)
    def _():
        wg_ref[...] = wg32_ref[...].astype(BF16)
        wu_ref[...] = wu32_ref[...].astype(BF16)

    seq_start = i % tiles_per_seq == 0
    row = lax.broadcasted_iota(jnp.int32, (SUBLANES, FFN_COL_CHUNK), 0)
    n_chunks = o_ref.shape[1] // FFN_COL_CHUNK
    tails = [None] * n_chunks
    for p in range(0, x_ref.shape[0], FFN_PANEL_ROWS):
        panel = slice(p, p + FFN_PANEL_ROWS)
        x = x_ref[panel, :]
        for c in range(n_chunks):
            cols = slice(c * FFN_COL_CHUNK, (c + 1) * FFN_COL_CHUNK)
            gate = jnp.dot(x, wg_ref[:, cols], preferred_element_type=F32)
            up = jnp.dot(x, wu_ref[:, cols], preferred_element_type=F32)
            tail = jnp.where(seq_start, 0.0, tail_ref[:, cols]) if p == 0 else tails[c]
            tails[c] = gate[FFN_PANEL_ROWS - SUBLANES:, :]
            r1 = pltpu.roll(gate, 1, 0)
            r2 = pltpu.roll(gate, 2, 0)
            h1 = jnp.where(row == 0, tail[SUBLANES - 1:, :], r1[:SUBLANES, :])
            h2 = jnp.where(row == 0, tail[SUBLANES - 2:SUBLANES - 1, :],
                           jnp.where(row == 1, tail[SUBLANES - 1:, :], r2[:SUBLANES, :]))
            g1 = jnp.concatenate([h1, r1[SUBLANES:, :]], axis=0)
            g2 = jnp.concatenate([h2, r2[SUBLANES:, :]], axis=0)
            acc = (cb_ref[:, cols] + g2 * cw_ref[0:1, cols] + g1 * cw_ref[1:2, cols]
                   + gate * cw_ref[2:3, cols])
            o_ref[panel, cols] = (acc * jax.nn.sigmoid(acc) * up).astype(o_ref.dtype)
    for c in range(n_chunks):
        tail_ref[:, c * FFN_COL_CHUNK:(c + 1) * FFN_COL_CHUNK] = tails[c]


def _ffn_up(xb, w_up_all, layer, conv_w, conv_b, seq, tm=2 * FFN_PANEL_ROWS, tn=512):
    t, dm = xb.shape
    dff = w_up_all.shape[2] // 2
    n_col = dff // tn
    w_up = w_up_all
    return pl.pallas_call(
        functools.partial(_ffn_up_kernel, tiles_per_seq=seq // tm),
        grid=(n_col, t // tm),
        in_specs=[
            pl.BlockSpec((tm, dm), lambda j, i: (i, 0)),
            pl.BlockSpec((None, dm, tn), lambda j, i: (layer, 0, j)),
            pl.BlockSpec((None, dm, tn), lambda j, i: (layer, 0, j + n_col)),
            pl.BlockSpec((CONV_WIDTH, tn), lambda j, i: (0, j)),
            pl.BlockSpec((1, tn), lambda j, i: (0, j)),
        ],
        out_specs=pl.BlockSpec((tm, tn), lambda j, i: (i, j)),
        out_shape=jax.ShapeDtypeStruct((t, dff), BF16),
        scratch_shapes=[pltpu.VMEM((SUBLANES, tn), F32), pltpu.VMEM((dm, tn), BF16), pltpu.VMEM((dm, tn), BF16)],
        compiler_params=_params("parallel", "arbitrary"),
        name="ffn_up_convglu",
    )(xb, w_up, w_up, conv_w, conv_b.reshape(1, dff))


ATTN_SPAN = 128
ATTN_TILE = 2048
ATTN_MERGE_ROWS = 256


def _block_attention(q, k, v, valid):
    s = lax.dot_general(q.astype(BF16), k.astype(BF16), (((1,), (1,)), ((), ())),
                        preferred_element_type=F32) * (ATTN_HEAD_DIM ** -0.5)
    s = jnp.where(valid, s, MASK_VALUE)
    m = jnp.max(s, axis=-1, keepdims=True)
    p = jnp.exp(s - m)
    l = jnp.sum(p, axis=-1, keepdims=True)
    o = jnp.dot(p.astype(BF16), v.astype(BF16), preferred_element_type=F32) * (1.0 / l)
    return o, m + jnp.log(l)


def _attn_kernel(q1, k1, v1, kp1, vp1, q4, k4, v4, kp4, vp4, q16, k16, v16, kp16, vp16,
                 o_ref, og, lg, x4, *, tiles_per_seq):
    span, hd = ATTN_SPAN, ATTN_HEAD_DIM
    first_tile = pl.program_id(1) % tiles_per_seq == 0
    qi = lax.broadcasted_iota(jnp.int32, (span, 2 * span), 0)
    kj = lax.broadcasted_iota(jnp.int32, (span, 2 * span), 1)
    valid = (kj <= qi + span) & (kj >= qi)
    valid_edge = valid & (kj >= jnp.where(first_tile, span, 0))

    def emit(g, rows, q, k, v, mask):
        o, lse = _block_attention(q, k, v, mask)
        og[g, rows, :] = o
        lg[g, rows, :] = jnp.broadcast_to(lse, (span, hd))

    def cat(a, b):
        return jnp.concatenate([a, b], axis=0)

    n1 = ATTN_TILE // span
    rows = pl.ds(0, span)
    emit(0, rows, q1[rows, :], cat(kp1[...], k1[rows, :]), cat(vp1[...], v1[rows, :]), valid_edge)

    for j in range(1, n1):
        rows = pl.ds(j * span, span)
        keys = pl.ds((j - 1) * span, 2 * span)
        emit(0, rows, q1[rows, :], k1[keys, :], v1[keys, :], valid)

    d4 = 4
    nb4 = ATTN_TILE // (span * d4)
    for r in range(d4):
        rows = pl.ds(r, span, stride=d4)
        emit(1, rows, q4[rows, :], cat(kp4[rows, :], k4[rows, :]), cat(vp4[rows, :], v4[rows, :]), valid_edge)
        for j in range(1, nb4):
            rows = pl.ds(j * span * d4 + r, span, stride=d4)
            keys = pl.ds((j - 1) * span * d4 + r, 2 * span, stride=d4)
            emit(1, rows, q4[rows, :], k4[keys, :], v4[keys, :], valid)

    d16 = 16
    quarter = ATTN_TILE // d4
    for a, src in enumerate((q16, k16, v16, kp16, vp16)):
        for r4 in range(d4):
            x4[a, r4 * quarter:(r4 + 1) * quarter, :] = src[pl.ds(r4, quarter, stride=d4), :]
    for r in range(d16):
        sub = pl.ds((r % d4) * quarter + r // d4, span, stride=d4)
        o, lse = _block_attention(x4[0, sub, :], cat(x4[3, sub, :], x4[1, sub, :]),
                                  cat(x4[4, sub, :], x4[2, sub, :]), valid_edge)
        x4[0, sub, :] = o
        x4[3, sub, :] = jnp.broadcast_to(lse, (span, hd))
    for r4 in range(d4):
        og[2, pl.ds(r4, quarter, stride=d4), :] = x4[0, r4 * quarter:(r4 + 1) * quarter, :]
        lg[2, pl.ds(r4, quarter, stride=d4), :] = x4[3, r4 * quarter:(r4 + 1) * quarter, :]

    def merge(c, carry):
        rows = pl.ds(pl.multiple_of(c * ATTN_MERGE_ROWS, ATTN_MERGE_ROWS), ATTN_MERGE_ROWS)
        l0, l1, l2 = lg[0, rows, :], lg[1, rows, :], lg[2, rows, :]
        m = jnp.maximum(jnp.maximum(l0, l1), l2)
        e0, e1, e2 = jnp.exp(l0 - m), jnp.exp(l1 - m), jnp.exp(l2 - m)
        num = e0 * og[0, rows, :] + e1 * og[1, rows, :] + e2 * og[2, rows, :]
        o_ref[rows, :] = (num / (e0 + e1 + e2)).astype(o_ref.dtype)
        return carry

    lax.fori_loop(0, ATTN_TILE // ATTN_MERGE_ROWS, merge, 0)


def _dilated_attention(qkv, seq):
    t, width = qkv.shape
    hd, span, tile = ATTN_HEAD_DIM, ATTN_SPAN, ATTN_TILE
    heads = width // (3 * len(DILATED_GROUPS) * hd)
    tiles_per_seq = seq // tile

    def col(group, which):
        return lambda h: (group * 3 + which) * heads + h

    def cur(group, which):
        return pl.BlockSpec((tile, hd), lambda h, i: (i, col(group, which)(h)))

    def prev(group, which, rows):
        per = tile // rows
        return pl.BlockSpec((rows, hd), lambda h, i: (jnp.maximum(i * per - 1, 0), col(group, which)(h)))

    in_specs = []
    for group, (window, dilation) in enumerate(DILATED_GROUPS):
        assert window // dilation == span and tile % window == 0
        in_specs += [cur(group, 0), cur(group, 1), cur(group, 2), prev(group, 1, window), prev(group, 2, window)]
    return pl.pallas_call(
        functools.partial(_attn_kernel, tiles_per_seq=tiles_per_seq),
        grid=(heads, t // tile),
        in_specs=in_specs,
        out_specs=pl.BlockSpec((tile, hd), lambda h, i: (i, h)),
        out_shape=jax.ShapeDtypeStruct((t, heads * hd), BF16),
        scratch_shapes=[pltpu.VMEM((len(DILATED_GROUPS), tile, hd), F32)] * 2 + [pltpu.VMEM((5, tile, hd), F32)],
        compiler_params=_params("parallel", "parallel"),
        name="dilated_attention",
    )(*([qkv] * len(in_specs)))


def _pad_rank(p, q):
    r = p.shape[1]
    if r % LORA_PAD:
        extra = LORA_PAD - r % LORA_PAD
        p = jnp.pad(p, ((0, 0), (0, extra)))
        q = jnp.pad(q, ((0, extra), (0, 0)))
    return p.astype(BF16), q.astype(BF16)


def _rwkv_layer(x, mu, w_rkv, w0, w1, w2, a0, a1, a2, g1, g2, k_k, k_a, r_k, gn_g, gn_b, w_out, ln_g, ln_b):
    bsz, seq, dm = x.shape
    n = RWKV_HEAD_SIZE
    heads = dm // n
    t = bsz * seq
    def cols(wt):
        lead = wt.shape[:-1]
        return wt.reshape(*lead, heads, n).swapaxes(-1, -2).reshape(*lead, dm)

    mixed = _token_shift_mix(x, mu[jnp.array([0, 2, 3, 1, 4, 5])])
    rkv = _batched_matmul(mixed, cols(w_rkv).astype(BF16), 3, F32, rows_per_seq=seq)
    w1b, w2b = _pad_rank(w1, cols(w2))
    a1b, a2b = _pad_rank(a1, cols(a2))
    g1b, g2b = _pad_rank(g1, cols(g2))
    decay, a, g = _rwkv_lora(mixed, w1b, w2b, cols(w0).reshape(1, dm), a1b, a2b, cols(a0).reshape(1, dm),
                             g1b, g2b, seq)

    def lanes_param(p):
        return jnp.repeat(jnp.tile(p.reshape(heads, n).T, (1, bsz)), SUBLANES, axis=0)

    y = _rwkv_scan_pipelined(rkv, decay, a, lanes_param(k_k), lanes_param(k_a), lanes_param(r_k),
                             lanes_param(gn_g), lanes_param(gn_b), steps=min(64, seq))
    w_out_rows = w_out.reshape(heads, n, dm).swapaxes(0, 1).reshape(dm, dm)
    return _matmul_residual_ln(y, w_out_rows.astype(BF16), x.reshape(t, dm), ln_g, ln_b, gate=g,
                               tm=min(256, seq))


def _attn_layer(x32, xb, w_in, w_out, ln_g, ln_b, bsz, seq):
    t, dm = x32.shape
    qkv = _batched_matmul(xb[None], w_in[None], 1, F32)[0]
    merged = _dilated_attention(qkv, seq)
    return _matmul_residual_ln(merged, w_out.astype(BF16), x32, ln_g, ln_b, tm=512)


def _ffn_layer(x32, xb, w_up_all, layer, conv_w, conv_b, w_down, ln_g, ln_b, seq, emit_bf16):
    act = _ffn_up(xb, w_up_all, layer, conv_w, conv_b, seq)
    return _matmul_residual_ln(act, w_down.astype(BF16), x32, ln_g, ln_b, tm=256, emit_bf16=emit_bf16)


def kernel(x, rwkv_mu, rwkv_w_rkv, rwkv_w0, rwkv_w1, rwkv_w2, rwkv_a0, rwkv_a1, rwkv_a2, rwkv_g1, rwkv_g2, rwkv_k_k, rwkv_k_a, rwkv_r_k, rwkv_gn_g, rwkv_gn_b, rwkv_w_out, attn_w_in, attn_w_out, ffn_w_up, ffn_conv_w, ffn_conv_b, ffn_w_down, ln_mix_g, ln_mix_b, ln_ffn_g, ln_ffn_b):
    bsz, seq, dm = x.shape
    x32, xb = _rwkv_layer(x, rwkv_mu[0], rwkv_w_rkv[0], rwkv_w0[0], rwkv_w1[0], rwkv_w2[0], rwkv_a0[0],
                          rwkv_a1[0], rwkv_a2[0], rwkv_g1[0], rwkv_g2[0], rwkv_k_k[0], rwkv_k_a[0],
                          rwkv_r_k[0], rwkv_gn_g[0], rwkv_gn_b[0], rwkv_w_out[0], ln_mix_g[0], ln_mix_b[0])
    x32, xb = _ffn_layer(x32, xb, ffn_w_up, 0, ffn_conv_w[0], ffn_conv_b[0], ffn_w_down[0],
                         ln_ffn_g[0], ln_ffn_b[0], seq, True)
    x32, xb = _attn_layer(x32, xb, attn_w_in[0], attn_w_out[0], ln_mix_g[1], ln_mix_b[1], bsz, seq)
    x32, _ = _ffn_layer(x32, xb, ffn_w_up, 1, ffn_conv_w[1], ffn_conv_b[1], ffn_w_down[1],
                        ln_ffn_g[1], ln_ffn_b[1], seq, False)
    return x32.reshape(bsz, seq, dm)
```

```python
import functools

import jax
import jax.numpy as jnp
from jax import lax
from jax.experimental import pallas as pl
from jax.experimental.pallas import tpu as pltpu

F32 = jnp.float32
BF16 = jnp.bfloat16

V7X_VMEM_BYTES = 64 * 1024 * 1024
VMEM_LIMIT_BYTES = V7X_VMEM_BYTES - 8 * 1024 * 1024
SUBLANES = 8
LANES = 128

RWKV_HEAD_SIZE = 64
N_SHIFT_MIX = 6
GN_EPS = 64e-5
ATTN_HEAD_DIM = 128
DILATED_GROUPS = ((128, 1), (512, 4), (2048, 16))
MASK_VALUE = -1e30
CONV_WIDTH = 3
LN_EPS = 1e-5
DEPTH = 2
DEEPNORM_ALPHA = (2.0 * DEPTH) ** 0.25
LORA_PAD = 128
LN_CHUNK_ROWS = 32
MM_LN_PANEL_ROWS = 256


def _params(*semantics):
    return pltpu.CompilerParams(dimension_semantics=semantics, vmem_limit_bytes=VMEM_LIMIT_BYTES)


def _mix_kernel(x_ref, xp_ref, mu_ref, o_ref):
    s = pl.program_id(1)
    x = x_ref[...]
    before = jnp.where(s > 0, xp_ref[SUBLANES - 1:SUBLANES, :], 0.0)
    prev = pltpu.roll(x, 1, 0)
    row = lax.broadcasted_iota(jnp.int32, x.shape, 0)
    prev = jnp.where(row == 0, before, prev)
    xx = prev - x
    for c in range(N_SHIFT_MIX):
        o_ref[c] = (x + xx * mu_ref[c:c + 1, :]).astype(o_ref.dtype)


def _token_shift_mix(x, mu, ts=256):
    bsz, seq, dm = x.shape
    ts = min(ts, seq)
    nst = seq // ts
    return pl.pallas_call(
        _mix_kernel,
        grid=(bsz, nst),
        in_specs=[
            pl.BlockSpec((None, ts, dm), lambda b, s: (b, s, 0)),
            pl.BlockSpec((None, SUBLANES, dm), lambda b, s: (b, jnp.maximum(s * (ts // SUBLANES) - 1, 0), 0)),
            pl.BlockSpec((N_SHIFT_MIX, dm), lambda b, s: (0, 0)),
        ],
        out_specs=pl.BlockSpec((N_SHIFT_MIX, ts, dm), lambda b, s: (0, b * nst + s, 0)),
        out_shape=jax.ShapeDtypeStruct((N_SHIFT_MIX, bsz * seq, dm), BF16),
        compiler_params=_params("parallel", "parallel"),
        name="token_shift_mix",
    )(x, x, mu)


def _mm_kernel(a_ref, w_ref, o_ref, wb_ref):
    @pl.when(pl.program_id(2) == 0)
    def _():
        wb_ref[...] = w_ref[...].astype(BF16)

    res = jnp.dot(a_ref[...], wb_ref[...], preferred_element_type=F32).astype(o_ref.dtype)
    if len(o_ref.shape) == 2:
        o_ref[...] = res
    else:
        for q in range(o_ref.shape[0]):
            o_ref[q] = res[:, q * LANES:(q + 1) * LANES]


def _batched_matmul(a, w, n_batch, out_dtype, tm=1024, tn=1024, rows_per_seq=None):
    _, m, kdim = a.shape
    tm = min(tm, m)
    n = w.shape[2]
    if rows_per_seq is None:
        out_spec = pl.BlockSpec((None, tm, tn), lambda c, j, i: (c, i, j))
        out_shape = jax.ShapeDtypeStruct((n_batch, m, n), out_dtype)
    else:
        tm = min(tm, rows_per_seq)
        per = rows_per_seq // tm
        out_spec = pl.BlockSpec((None, None, tn // LANES, tm, LANES), lambda c, j, i: (c, i // per, j, i % per, 0))
        out_shape = jax.ShapeDtypeStruct((n_batch, m // rows_per_seq, n // LANES, rows_per_seq, LANES), out_dtype)
    return pl.pallas_call(
        _mm_kernel,
        grid=(n_batch, n // tn, m // tm),
        in_specs=[
            pl.BlockSpec((None, tm, kdim), lambda c, j, i: (c, i, 0)),
            pl.BlockSpec((None, kdim, tn), lambda c, j, i: (c, 0, j)),
        ],
        out_specs=out_spec,
        out_shape=out_shape,
        scratch_shapes=[pltpu.VMEM((kdim, tn), BF16)],
        compiler_params=_params("parallel", "parallel", "arbitrary"),
        name="batched_matmul",
    )(a, w)


def _lora_kernel(xw_ref, xa_ref, xg_ref, w1_ref, w2_ref, w0_ref, a1_ref, a2_ref, a0_ref,
                 g1_ref, g2_ref, decay_ref, a_ref, g_ref):
    def two_stage(x_ref, p_ref, q_ref, act):
        h = act(jnp.dot(x_ref[...], p_ref[...], preferred_element_type=F32))
        return jnp.dot(h.astype(BF16), q_ref[...], preferred_element_type=F32)

    def split_lanes(o_ref, val):
        for q in range(o_ref.shape[0]):
            o_ref[q] = val[:, q * LANES:(q + 1) * LANES]

    z = w0_ref[...] + two_stage(xw_ref, w1_ref, w2_ref, jnp.tanh)
    split_lanes(decay_ref, jnp.exp(-jnp.exp(-0.5) * jax.nn.sigmoid(z)))
    split_lanes(a_ref, jax.nn.sigmoid(a0_ref[...] + two_stage(xa_ref, a1_ref, a2_ref, lambda h: h)))
    g_ref[...] = two_stage(xg_ref, g1_ref, g2_ref, jax.nn.sigmoid)


def _rwkv_lora(mixed, w1, w2, w0, a1, a2, a0, g1, g2, seq, tm=512):
    _, t, dm = mixed.shape
    tm = min(tm, seq)
    per = seq // tm

    def full(arr):
        return pl.BlockSpec(arr.shape, lambda i: (0,) * arr.ndim)

    def mix_spec(c):
        return pl.BlockSpec((None, tm, dm), lambda i: (c, i, 0))

    out_spec = pl.BlockSpec((tm, dm), lambda i: (i, 0))
    out = jax.ShapeDtypeStruct((t, dm), F32)
    split_spec = pl.BlockSpec((None, dm // LANES, tm, LANES), lambda i: (i // per, 0, i % per, 0))
    split = jax.ShapeDtypeStruct((t // seq, dm // LANES, seq, LANES), F32)
    return pl.pallas_call(
        _lora_kernel,
        grid=(t // tm,),
        in_specs=[mix_spec(3), mix_spec(4), mix_spec(5), full(w1), full(w2), full(w0),
                  full(a1), full(a2), full(a0), full(g1), full(g2)],
        out_specs=[split_spec, split_spec, out_spec],
        out_shape=[split, split, out],
        compiler_params=_params("parallel"),
        name="rwkv_lora",
    )(mixed, mixed, mixed, w1, w2, w0, a1, a2, a0, g1, g2)


SCAN_V_GROUPS = 4
SCAN_PARTIALS = 2


SCAN_BATCH = 4
SCAN_CHUNK = LANES // SCAN_BATCH


def _swap_halves(a0, a1, a2, a3, low_half):
    r0, r1, r2, r3 = (pltpu.roll(a, 2 * SCAN_CHUNK, 1) for a in (a0, a1, a2, a3))
    return (jnp.where(low_half, a0, r2), jnp.where(low_half, a1, r3),
            jnp.where(low_half, r0, a2), jnp.where(low_half, r1, a3))


def _swap_chunks(p0, p1, p2, p3, even_chunk):
    return (jnp.where(even_chunk, p0, pltpu.roll(p1, SCAN_CHUNK, 1)),
            jnp.where(even_chunk, pltpu.roll(p0, 3 * SCAN_CHUNK, 1), p1),
            jnp.where(even_chunk, p2, pltpu.roll(p3, SCAN_CHUNK, 1)),
            jnp.where(even_chunk, pltpu.roll(p2, 3 * SCAN_CHUNK, 1), p3))


def _chunk_transpose_direct(ins, at_chunk):
    outs = []
    for j in range(SCAN_BATCH):
        picked = [ins[b] if b == j else pltpu.roll(ins[b], ((b - j) % SCAN_BATCH) * SCAN_CHUNK, 1)
                  for b in range(SCAN_BATCH)]
        out = picked[SCAN_BATCH - 1]
        for b in range(SCAN_BATCH - 2, -1, -1):
            out = jnp.where(at_chunk[b], picked[b], out)
        outs.append(out)
    return tuple(outs)


Q_KK, Q_WR, Q_W, Q_B, Q_K2, Q_V = range(6)
R_INV2, R_BR, R_KR, R_BONUS = range(4)
SCAN_UNIT_Q = 2
SCAN_Q = RWKV_HEAD_SIZE // SCAN_BATCH
SCAN_STEP_UNROLL = 1


def _scan_pipelined_kernel(r_ref, k_ref, v_ref, w_ref, a_ref, kkp_ref, kap_ref, rkp_ref, gng_ref, gnb_ref,
                           y_ref, st_ref, buf_a, buf_b, rows_a, rows_b, y_s, yn_s, *, steps):
    n = RWKV_HEAD_SIZE
    n_vg = n // SUBLANES
    n_groups = steps // SUBLANES
    lane = lax.broadcasted_iota(jnp.int32, (SUBLANES, LANES), 1)
    low_half = lane < 2 * SCAN_CHUNK
    even_chunk = (lane & SCAN_CHUNK) == 0
    at_chunk = [(lane >= b * SCAN_CHUNK) & (lane < (b + 1) * SCAN_CHUNK) for b in range(SCAN_BATCH)]

    @pl.when(pl.program_id(0) == 0)
    def _():
        st_ref[...] = jnp.zeros_like(st_ref)
        yn_s[...] = jnp.zeros_like(yn_s)

    def tile_rows(i):
        if isinstance(i, int):
            return pl.ds(i * SUBLANES, SUBLANES)
        return pl.ds(pl.multiple_of(i * SUBLANES, SUBLANES), SUBLANES)

    def transposed(tiles_per_q, direct):
        if direct:
            return [_chunk_transpose_direct(t, at_chunk) for t in tiles_per_q]
        halves = [_swap_halves(*t, low_half) for t in tiles_per_q]
        return [_swap_chunks(*h, even_chunk) for h in halves]

    def regroup(s0, qs, dst, direct):
        srcs = (r_ref, k_ref, v_ref, w_ref, a_ref)
        tiles = [tuple(src[bi, q, pl.ds(s0, SUBLANES), :] for bi in range(SCAN_BATCH))
                 for q in qs for src in srcs]
        regrouped = transposed(tiles, direct)
        sums = None
        for qi, q in enumerate(qs):
            rq, kq, vq, wq, aq = regrouped[qi * len(srcs):(qi + 1) * len(srcs)]
            for j in range(SCAN_BATCH):
                rows = tile_rows(q * SCAN_BATCH + j)
                r, k, v, w, a = rq[j], kq[j], vq[j], wq[j], aq[j]
                kk0 = k * kkp_ref[rows, :]
                k2 = k * (1.0 + (a - 1.0) * kap_ref[rows, :])
                b0 = kk0 * a
                rk2 = r * k2
                dst[Q_KK, rows, :] = kk0
                dst[Q_WR, rows, :] = w * r
                dst[Q_W, rows, :] = w
                dst[Q_B, rows, :] = b0
                dst[Q_K2, rows, :] = k2
                dst[Q_V, rows, :] = v
                terms = (kk0 * kk0, b0 * r, rk2, rk2 * rkp_ref[rows, :])
                sums = terms if sums is None else tuple(x + y for x, y in zip(sums, terms))
        return sums

    def finish_rows(sums, rows_ref):
        ss, br, kr, bonus = sums
        rows_ref[R_INV2] = 1.0 / jnp.maximum(ss, 1e-24)
        rows_ref[R_BR] = br
        rows_ref[R_KR] = kr
        rows_ref[R_BONUS] = bonus

    def unregroup(s0, qs, direct):
        tiles = [tuple(yn_s[tile_rows(q * SCAN_BATCH + j), :] for j in range(SCAN_BATCH)) for q in qs]
        for q, outs in zip(qs, transposed(tiles, direct)):
            for bi, val in enumerate(outs):
                y_ref[bi, q, pl.ds(s0, SUBLANES), :] = val

    def step(t, cur, cur_rows):
        inv2_row = cur_rows[R_INV2, pl.ds(t, 1), :]
        br_row = cur_rows[R_BR, pl.ds(t, 1), :]
        kr_row = cur_rows[R_KR, pl.ds(t, 1), :]
        for g0 in range(0, n_vg, SCAN_V_GROUPS):
            groups = range(g0, g0 + SCAN_V_GROUPS)
            acc_sa = {g: [None] * SCAN_PARTIALS for g in groups}
            acc_y = {g: [None] * SCAN_PARTIALS for g in groups}
            for ki in range(n):
                kk_row = cur[Q_KK, pl.ds(ki * SUBLANES + t, 1), :]
                wr_row = cur[Q_WR, pl.ds(ki * SUBLANES + t, 1), :]
                p = ki % SCAN_PARTIALS
                for g in groups:
                    s = st_ref[ki, g * SUBLANES:(g + 1) * SUBLANES, :]
                    t_sa = s * kk_row
                    t_y = s * wr_row
                    acc_sa[g][p] = t_sa if acc_sa[g][p] is None else acc_sa[g][p] + t_sa
                    acc_y[g][p] = t_y if acc_y[g][p] is None else acc_y[g][p] + t_y
            sa = {}
            vv = {}
            for g in groups:
                v_rows = pl.ds(g * SUBLANES * SUBLANES + t, SUBLANES, stride=SUBLANES)
                sa[g] = -functools.reduce(lambda x, y: x + y, acc_sa[g]) * inv2_row
                vv[g] = cur[Q_V, v_rows, :]
                y_s[v_rows, :] = (functools.reduce(lambda x, y: x + y, acc_y[g])
                                  + sa[g] * br_row + vv[g] * kr_row)
            for ki in range(n):
                w_row = cur[Q_W, pl.ds(ki * SUBLANES + t, 1), :]
                b_row = cur[Q_B, pl.ds(ki * SUBLANES + t, 1), :]
                k_row = cur[Q_K2, pl.ds(ki * SUBLANES + t, 1), :]
                for g in groups:
                    s = st_ref[ki, g * SUBLANES:(g + 1) * SUBLANES, :]
                    st_ref[ki, g * SUBLANES:(g + 1) * SUBLANES, :] = s * w_row + sa[g] * b_row + vv[g] * k_row

    def normalise(cur, cur_rows):
        tot = None
        for vi in range(n):
            yv = y_s[vi * SUBLANES:(vi + 1) * SUBLANES, :]
            tot = yv if tot is None else tot + yv
        mean = tot * (1.0 / n)
        sq = None
        for vi in range(n):
            yc = y_s[vi * SUBLANES:(vi + 1) * SUBLANES, :] - mean
            sq = yc * yc if sq is None else sq + yc * yc
        rstd = lax.rsqrt(sq * (1.0 / n) + GN_EPS)
        bonus_v = cur_rows[R_BONUS]
        for vi in range(n):
            rows = slice(vi * SUBLANES, (vi + 1) * SUBLANES)
            yn_s[rows, :] = ((y_s[rows, :] - mean) * rstd * gng_ref[rows, :] + gnb_ref[rows, :]
                             + bonus_v * cur[Q_V, rows, :])

    def run_group(g, cur, cur_rows, nxt, nxt_rows):
        s0 = pl.multiple_of(g * SUBLANES, SUBLANES)
        s_next = pl.multiple_of(jnp.minimum(s0 + SUBLANES, steps - SUBLANES), SUBLANES)
        s_prev = pl.multiple_of(jnp.maximum(s0 - SUBLANES, 0), SUBLANES)

        def body(i, sums):
            for u in range(SCAN_STEP_UNROLL):
                t = SCAN_STEP_UNROLL * i + u
                qs = [SCAN_UNIT_Q * t + v for v in range(SCAN_UNIT_Q)]
                part = regroup(s_next, qs, nxt, True)
                unregroup(s_prev, qs, True)
                step(t, cur, cur_rows)
                sums = tuple(x + y for x, y in zip(sums, part))
            return sums

        zero = jnp.zeros((SUBLANES, LANES), F32)
        sums = lax.fori_loop(0, SUBLANES // SCAN_STEP_UNROLL, body, (zero, zero, zero, zero))
        finish_rows(sums, nxt_rows)
        normalise(cur, cur_rows)

    half = SCAN_Q // 2
    first = regroup(0, list(range(half)), buf_a, False)
    second = regroup(0, list(range(half, SCAN_Q)), buf_a, False)
    finish_rows(tuple(x + y for x, y in zip(first, second)), rows_a)

    def two_groups(i, carry):
        run_group(2 * i, buf_a, rows_a, buf_b, rows_b)
        run_group(2 * i + 1, buf_b, rows_b, buf_a, rows_a)
        return carry

    lax.fori_loop(0, n_groups // 2, two_groups, 0)

    unregroup(steps - SUBLANES, list(range(half)), False)
    unregroup(steps - SUBLANES, list(range(half, SCAN_Q)), False)


def _rwkv_scan_pipelined(rkv, w, a, kkp, kap, rkp, gng, gnb, steps=64):
    bsz, nq, seq, lanes = w.shape
    n = RWKV_HEAD_SIZE
    assert bsz == SCAN_BATCH and nq == SCAN_Q and lanes == LANES and (steps // SUBLANES) % 2 == 0

    def rkv_spec(c):
        return pl.BlockSpec((None, bsz, nq, steps, lanes), lambda i: (c, 0, 0, i, 0))

    seq_spec = pl.BlockSpec((bsz, nq, steps, lanes), lambda i: (0, 0, i, 0))
    par_spec = pl.BlockSpec((n * SUBLANES, LANES), lambda i: (0, 0))
    rows = pltpu.VMEM((n * SUBLANES, LANES), F32)
    group_buf = pltpu.VMEM((6, n * SUBLANES, LANES), F32)
    group_rows = pltpu.VMEM((4, SUBLANES, LANES), F32)
    return pl.pallas_call(
        functools.partial(_scan_pipelined_kernel, steps=steps),
        grid=(seq // steps,),
        in_specs=[rkv_spec(0), rkv_spec(1), rkv_spec(2), seq_spec, seq_spec] + [par_spec] * 5,
        out_specs=seq_spec,
        out_shape=jax.ShapeDtypeStruct((bsz, nq, seq, lanes), F32),
        scratch_shapes=[pltpu.VMEM((n, n, LANES), F32), group_buf, group_buf, group_rows, group_rows, rows, rows],
        compiler_params=_params("arbitrary"),
        name="rwkv7_scan",
    )(rkv, rkv, rkv, w, a, kkp, kap, rkp, gng, gnb)


def _mm_ln_kernel(*refs, gated, emit_bf16):
    refs = list(refs)
    a_ref = refs.pop(0)
    gate_ref = refs.pop(0) if gated else None
    w_ref, x_ref, g_ref, b_ref, o_ref = refs[:5]
    ob_ref = refs[5] if emit_bf16 else None
    acc_ref = refs[-1]

    if gated:
        a = jnp.concatenate([a_ref[q] for q in range(a_ref.shape[0])], axis=1)
        a = (a * gate_ref[...]).astype(BF16)
    else:
        a = a_ref[...]

    panel_rows = min(MM_LN_PANEL_ROWS, acc_ref.shape[0])
    for p in range(0, acc_ref.shape[0], panel_rows):
        panel = slice(p, p + panel_rows)
        acc_ref[panel, :] = jnp.dot(a[panel, :], w_ref[...], preferred_element_type=F32)
        for c in range(p // LN_CHUNK_ROWS, (p + panel_rows) // LN_CHUNK_ROWS):
            rows = slice(c * LN_CHUNK_ROWS, (c + 1) * LN_CHUNK_ROWS)
            z = DEEPNORM_ALPHA * x_ref[rows, :] + acc_ref[rows, :]
            mu = jnp.mean(z, axis=-1, keepdims=True)
            zc = z - mu
            var = jnp.mean(zc * zc, axis=-1, keepdims=True)
            out = zc * lax.rsqrt(var + LN_EPS) * g_ref[...] + b_ref[...]
            o_ref[rows, :] = out
            if emit_bf16:
                ob_ref[rows, :] = out.astype(BF16)


def _matmul_residual_ln(a, w, x, g, b, *, gate=None, tm, emit_bf16=True, layer=0):
    if w.ndim == 2:
        w, layer = w[None], 0
    kdim, n = w.shape[1:]
    m = x.shape[0]
    row = lambda i: (i, 0)
    fixed = lambda i: (0, 0)
    args = [a]
    if gate is None:
        in_specs = [pl.BlockSpec((tm, kdim), row)]
    else:
        per = a.shape[2] // tm
        in_specs = [pl.BlockSpec((None, kdim // LANES, tm, LANES), lambda i: (i // per, 0, i % per, 0)),
                    pl.BlockSpec((tm, kdim), row)]
        args.append(gate)
    in_specs += [pl.BlockSpec((None, kdim, n), lambda i: (layer, 0, 0), pipeline_mode=pl.Buffered(1)),
                 pl.BlockSpec((tm, n), row),
                 pl.BlockSpec((1, n), fixed),
                 pl.BlockSpec((1, n), fixed)]
    args += [w, x, g.reshape(1, n), b.reshape(1, n)]
    out_specs = [pl.BlockSpec((tm, n), row)]
    out_shape = [jax.ShapeDtypeStruct((m, n), F32)]
    if emit_bf16:
        out_specs.append(pl.BlockSpec((tm, n), row))
        out_shape.append(jax.ShapeDtypeStruct((m, n), BF16))
    outs = pl.pallas_call(
        functools.partial(_mm_ln_kernel, gated=gate is not None, emit_bf16=emit_bf16),
        grid=(m // tm,),
        in_specs=in_specs,
        out_specs=out_specs,
        out_shape=out_shape,
        scratch_shapes=[pltpu.VMEM((tm, n), F32)],
        compiler_params=_params("parallel"),
        name="matmul_residual_ln",
    )(*args)
    return outs if emit_bf16 else (outs[0], None)


FFN_COL_CHUNK = 256
FFN_PANEL_ROWS = 1024


def _ffn_up_kernel(x_ref, wg32_ref, wu32_ref, cw_ref, cb_ref, o_ref, tail_ref, wg_ref, wu_ref, *,
                   tiles_per_seq):
    i = pl.program_id(1)

    @pl.when(i == 0)
    def _():
        wg_ref[...] = wg32_ref[...].astype(BF16)
        wu_ref[...] = wu32_ref[...].astype(BF16)

    seq_start = i % tiles_per_seq == 0
    row = lax.broadcasted_iota(jnp.int32, (SUBLANES, FFN_COL_CHUNK), 0)
    n_chunks = o_ref.shape[1] // FFN_COL_CHUNK
    tails = [None] * n_chunks
    for p in range(0, x_ref.shape[0], FFN_PANEL_ROWS):
        panel = slice(p, p + FFN_PANEL_ROWS)
        x = x_ref[panel, :]
        for c in range(n_chunks):
            cols = slice(c * FFN_COL_CHUNK, (c + 1) * FFN_COL_CHUNK)
            gate = jnp.dot(x, wg_ref[:, cols], preferred_element_type=F32)
            up = jnp.dot(x, wu_ref[:, cols], preferred_element_type=F32)
            tail = jnp.where(seq_start, 0.0, tail_ref[:, cols]) if p == 0 else tails[c]
            tails[c] = gate[FFN_PANEL_ROWS - SUBLANES:, :]
            r1 = pltpu.roll(gate, 1, 0)
            r2 = pltpu.roll(gate, 2, 0)
            h1 = jnp.where(row == 0, tail[SUBLANES - 1:, :], r1[:SUBLANES, :])
            h2 = jnp.where(row == 0, tail[SUBLANES - 2:SUBLANES - 1, :],
                           jnp.where(row == 1, tail[SUBLANES - 1:, :], r2[:SUBLANES, :]))
            g1 = jnp.concatenate([h1, r1[SUBLANES:, :]], axis=0)
            g2 = jnp.concatenate([h2, r2[SUBLANES:, :]], axis=0)
            acc = (cb_ref[:, cols] + g2 * cw_ref[0:1, cols] + g1 * cw_ref[1:2, cols]
                   + gate * cw_ref[2:3, cols])
            o_ref[panel, cols] = (acc * jax.nn.sigmoid(acc) * up).astype(o_ref.dtype)
    for c in range(n_chunks):
        tail_ref[:, c * FFN_COL_CHUNK:(c + 1) * FFN_COL_CHUNK] = tails[c]


def _ffn_up(xb, w_up_all, layer, conv_w, conv_b, seq, tm=2 * FFN_PANEL_ROWS, tn=512):
    t, dm = xb.shape
    dff = w_up_all.shape[2] // 2
    n_col = dff // tn
    w_up = w_up_all
    return pl.pallas_call(
        functools.partial(_ffn_up_kernel, tiles_per_seq=seq // tm),
        grid=(n_col, t // tm),
        in_specs=[
            pl.BlockSpec((tm, dm), lambda j, i: (i, 0)),
            pl.BlockSpec((None, dm, tn), lambda j, i: (layer, 0, j)),
            pl.BlockSpec((None, dm, tn), lambda j, i: (layer, 0, j + n_col)),
            pl.BlockSpec((CONV_WIDTH, tn), lambda j, i: (0, j)),
            pl.BlockSpec((1, tn), lambda j, i: (0, j)),
        ],
        out_specs=pl.BlockSpec((tm, tn), lambda j, i: (i, j)),
        out_shape=jax.ShapeDtypeStruct((t, dff), BF16),
        scratch_shapes=[pltpu.VMEM((SUBLANES, tn), F32), pltpu.VMEM((dm, tn), BF16), pltpu.VMEM((dm, tn), BF16)],
        compiler_params=_params("parallel", "arbitrary"),
        name="ffn_up_convglu",
    )(xb, w_up, w_up, conv_w, conv_b.reshape(1, dff))


ATTN_SPAN = 128
ATTN_TILE = 2048
ATTN_MERGE_ROWS = 256


def _block_attention(q, k, v, valid):
    s = lax.dot_general(q.astype(BF16), k.astype(BF16), (((1,), (1,)), ((), ())),
                        preferred_element_type=F32) * (ATTN_HEAD_DIM ** -0.5)
    s = jnp.where(valid, s, MASK_VALUE)
    m = jnp.max(s, axis=-1, keepdims=True)
    p = jnp.exp(s - m)
    l = jnp.sum(p, axis=-1, keepdims=True)
    o = jnp.dot(p.astype(BF16), v.astype(BF16), preferred_element_type=F32) * (1.0 / l)
    return o, m + jnp.log(l)


def _attn_kernel(q1, k1, v1, kp1, vp1, q4, k4, v4, kp4, vp4, q16, k16, v16, kp16, vp16,
                 o_ref, og, lg, x4, *, tiles_per_seq):
    span, hd = ATTN_SPAN, ATTN_HEAD_DIM
    first_tile = pl.program_id(1) % tiles_per_seq == 0
    qi = lax.broadcasted_iota(jnp.int32, (span, 2 * span), 0)
    kj = lax.broadcasted_iota(jnp.int32, (span, 2 * span), 1)
    valid = (kj <= qi + span) & (kj >= qi)
    valid_edge = valid & (kj >= jnp.where(first_tile, span, 0))

    def emit(g, rows, q, k, v, mask):
        o, lse = _block_attention(q, k, v, mask)
        og[g, rows, :] = o
        lg[g, rows, :] = jnp.broadcast_to(lse, (span, hd))

    def cat(a, b):
        return jnp.concatenate([a, b], axis=0)

    n1 = ATTN_TILE // span
    rows = pl.ds(0, span)
    emit(0, rows, q1[rows, :], cat(kp1[...], k1[rows, :]), cat(vp1[...], v1[rows, :]), valid_edge)

    for j in range(1, n1):
        rows = pl.ds(j * span, span)
        keys = pl.ds((j - 1) * span, 2 * span)
        emit(0, rows, q1[rows, :], k1[keys, :], v1[keys, :], valid)

    d4 = 4
    nb4 = ATTN_TILE // (span * d4)
    for r in range(d4):
        rows = pl.ds(r, span, stride=d4)
        emit(1, rows, q4[rows, :], cat(kp4[rows, :], k4[rows, :]), cat(vp4[rows, :], v4[rows, :]), valid_edge)
        for j in range(1, nb4):
            rows = pl.ds(j * span * d4 + r, span, stride=d4)
            keys = pl.ds((j - 1) * span * d4 + r, 2 * span, stride=d4)
            emit(1, rows, q4[rows, :], k4[keys, :], v4[keys, :], valid)

    d16 = 16
    quarter = ATTN_TILE // d4
    for a, src in enumerate((q16, k16, v16, kp16, vp16)):
        for r4 in range(d4):
            x4[a, r4 * quarter:(r4 + 1) * quarter, :] = src[pl.ds(r4, quarter, stride=d4), :]
    for r in range(d16):
        sub = pl.ds((r % d4) * quarter + r // d4, span, stride=d4)
        o, lse = _block_attention(x4[0, sub, :], cat(x4[3, sub, :], x4[1, sub, :]),
                                  cat(x4[4, sub, :], x4[2, sub, :]), valid_edge)
        x4[0, sub, :] = o
        x4[3, sub, :] = jnp.broadcast_to(lse, (span, hd))
    for r4 in range(d4):
        og[2, pl.ds(r4, quarter, stride=d4), :] = x4[0, r4 * quarter:(r4 + 1) * quarter, :]
        lg[2, pl.ds(r4, quarter, stride=d4), :] = x4[3, r4 * quarter:(r4 + 1) * quarter, :]

    def merge(c, carry):
        rows = pl.ds(pl.multiple_of(c * ATTN_MERGE_ROWS, ATTN_MERGE_ROWS), ATTN_MERGE_ROWS)
        l0, l1, l2 = lg[0, rows, :], lg[1, rows, :], lg[2, rows, :]
        m = jnp.maximum(jnp.maximum(l0, l1), l2)
        e0, e1, e2 = jnp.exp(l0 - m), jnp.exp(l1 - m), jnp.exp(l2 - m)
        num = e0 * og[0, rows, :] + e1 * og[1, rows, :] + e2 * og[2, rows, :]
        o_ref[rows, :] = (num / (e0 + e1 + e2)).astype(o_ref.dtype)
        return carry

    lax.fori_loop(0, ATTN_TILE // ATTN_MERGE_ROWS, merge, 0)


def _dilated_attention(qkv, seq):
    t, width = qkv.shape
    hd, span, tile = ATTN_HEAD_DIM, ATTN_SPAN, ATTN_TILE
    heads = width // (3 * len(DILATED_GROUPS) * hd)
    tiles_per_seq = seq // tile

    def col(group, which):
        return lambda h: (group * 3 + which) * heads + h

    def cur(group, which):
        return pl.BlockSpec((tile, hd), lambda h, i: (i, col(group, which)(h)))

    def prev(group, which, rows):
        per = tile // rows
        return pl.BlockSpec((rows, hd), lambda h, i: (jnp.maximum(i * per - 1, 0), col(group, which)(h)))

    in_specs = []
    for group, (window, dilation) in enumerate(DILATED_GROUPS):
        assert window // dilation == span and tile % window == 0
        in_specs += [cur(group, 0), cur(group, 1), cur(group, 2), prev(group, 1, window), prev(group, 2, window)]
    return pl.pallas_call(
        functools.partial(_attn_kernel, tiles_per_seq=tiles_per_seq),
        grid=(heads, t // tile),
        in_specs=in_specs,
        out_specs=pl.BlockSpec((tile, hd), lambda h, i: (i, h)),
        out_shape=jax.ShapeDtypeStruct((t, heads * hd), BF16),
        scratch_shapes=[pltpu.VMEM((len(DILATED_GROUPS), tile, hd), F32)] * 2 + [pltpu.VMEM((5, tile, hd), F32)],
        compiler_params=_params("parallel", "parallel"),
        name="dilated_attention",
    )(*([qkv] * len(in_specs)))


def _pad_rank(p, q):
    r = p.shape[1]
    if r % LORA_PAD:
        extra = LORA_PAD - r % LORA_PAD
        p = jnp.pad(p, ((0, 0), (0, extra)))
        q = jnp.pad(q, ((0, extra), (0, 0)))
    return p.astype(BF16), q.astype(BF16)


def _rwkv_layer(x, mu, w_rkv, w0, w1, w2, a0, a1, a2, g1, g2, k_k, k_a, r_k, gn_g, gn_b, w_out, ln_g, ln_b):
    bsz, seq, dm = x.shape
    n = RWKV_HEAD_SIZE
    heads = dm // n
    t = bsz * seq
    def cols(wt):
        lead = wt.shape[:-1]
        return wt.reshape(*lead, heads, n).swapaxes(-1, -2).reshape(*lead, dm)

    mixed = _token_shift_mix(x, mu[jnp.array([0, 2, 3, 1, 4, 5])])
    rkv = _batched_matmul(mixed, cols(w_rkv.astype(BF16)), 3, F32, rows_per_seq=seq)
    w1b, w2b = _pad_rank(w1, cols(w2))
    a1b, a2b = _pad_rank(a1, cols(a2))
    g1b, g2b = _pad_rank(g1, cols(g2))
    decay, a, g = _rwkv_lora(mixed, w1b, w2b, cols(w0).reshape(1, dm), a1b, a2b, cols(a0).reshape(1, dm),
                             g1b, g2b, seq)

    def lanes_param(p):
        return jnp.repeat(jnp.tile(p.reshape(heads, n).T, (1, bsz)), SUBLANES, axis=0)

    y = _rwkv_scan_pipelined(rkv, decay, a, lanes_param(k_k), lanes_param(k_a), lanes_param(r_k),
                             lanes_param(gn_g), lanes_param(gn_b), steps=min(64, seq))
    w_out_rows = w_out.reshape(heads, n, dm).swapaxes(0, 1).reshape(dm, dm)
    return _matmul_residual_ln(y, w_out_rows.astype(BF16), x.reshape(t, dm), ln_g, ln_b, gate=g,
                               tm=min(512, seq))


def _attn_layer(x32, xb, w_in, w_out, ln_g, ln_b, bsz, seq):
    t, dm = x32.shape
    qkv = _batched_matmul(xb[None], w_in[None], 1, F32)[0]
    merged = _dilated_attention(qkv, seq)
    return _matmul_residual_ln(merged, w_out.astype(BF16), x32, ln_g, ln_b, tm=512)


def _ffn_layer(x32, xb, w_up_all, layer, conv_w, conv_b, w_down_all, ln_g, ln_b, seq, emit_bf16):
    act = _ffn_up(xb, w_up_all, layer, conv_w, conv_b, seq)
    return _matmul_residual_ln(act, w_down_all.astype(BF16), x32, ln_g, ln_b, tm=256, emit_bf16=emit_bf16,
                               layer=layer)


def kernel(x, rwkv_mu, rwkv_w_rkv, rwkv_w0, rwkv_w1, rwkv_w2, rwkv_a0, rwkv_a1, rwkv_a2, rwkv_g1, rwkv_g2, rwkv_k_k, rwkv_k_a, rwkv_r_k, rwkv_gn_g, rwkv_gn_b, rwkv_w_out, attn_w_in, attn_w_out, ffn_w_up, ffn_conv_w, ffn_conv_b, ffn_w_down, ln_mix_g, ln_mix_b, ln_ffn_g, ln_ffn_b):
    bsz, seq, dm = x.shape
    x32, xb = _rwkv_layer(x, rwkv_mu[0], rwkv_w_rkv[0], rwkv_w0[0], rwkv_w1[0], rwkv_w2[0], rwkv_a0[0],
                          rwkv_a1[0], rwkv_a2[0], rwkv_g1[0], rwkv_g2[0], rwkv_k_k[0], rwkv_k_a[0],
                          rwkv_r_k[0], rwkv_gn_g[0], rwkv_gn_b[0], rwkv_w_out[0], ln_mix_g[0], ln_mix_b[0])
    x32, xb = _ffn_layer(x32, xb, ffn_w_up, 0, ffn_conv_w[0], ffn_conv_b[0], ffn_w_down,
                         ln_ffn_g[0], ln_ffn_b[0], seq, True)
    x32, xb = _attn_layer(x32, xb, attn_w_in[0], attn_w_out[0], ln_mix_g[1], ln_mix_b[1], bsz, seq)
    x32, _ = _ffn_layer(x32, xb, ffn_w_up, 1, ffn_conv_w[1], ffn_conv_b[1], ffn_w_down,
                        ln_ffn_g[1], ln_ffn_b[1], seq, False)
    return x32.reshape(bsz, seq, dm)
```

```python
import functools

import jax
import jax.numpy as jnp
from jax import lax
from jax.experimental import pallas as pl
from jax.experimental.pallas import tpu as pltpu

F32 = jnp.float32
BF16 = jnp.bfloat16

V7X_VMEM_BYTES = 64 * 1024 * 1024
VMEM_LIMIT_BYTES = V7X_VMEM_BYTES - 8 * 1024 * 1024
SUBLANES = 8
LANES = 128

RWKV_HEAD_SIZE = 64
N_SHIFT_MIX = 6
GN_EPS = 64e-5
ATTN_HEAD_DIM = 128
DILATED_GROUPS = ((128, 1), (512, 4), (2048, 16))
MASK_VALUE = -1e30
CONV_WIDTH = 3
LN_EPS = 1e-5
DEPTH = 2
DEEPNORM_ALPHA = (2.0 * DEPTH) ** 0.25
LORA_PAD = 128
LN_CHUNK_ROWS = 32
MM_LN_PANEL_ROWS = 256


def _params(*semantics):
    return pltpu.CompilerParams(dimension_semantics=semantics, vmem_limit_bytes=VMEM_LIMIT_BYTES)


def _mix_kernel(x_ref, xp_ref, mu_ref, o_ref):
    s = pl.program_id(1)
    x = x_ref[...]
    before = jnp.where(s > 0, xp_ref[SUBLANES - 1:SUBLANES, :], 0.0)
    prev = pltpu.roll(x, 1, 0)
    row = lax.broadcasted_iota(jnp.int32, x.shape, 0)
    prev = jnp.where(row == 0, before, prev)
    xx = prev - x
    for c in range(N_SHIFT_MIX):
        o_ref[c] = (x + xx * mu_ref[c:c + 1, :]).astype(o_ref.dtype)


def _token_shift_mix(x, mu, ts=512):
    bsz, seq, dm = x.shape
    ts = min(ts, seq)
    nst = seq // ts
    return pl.pallas_call(
        _mix_kernel,
        grid=(bsz, nst),
        in_specs=[
            pl.BlockSpec((None, ts, dm), lambda b, s: (b, s, 0)),
            pl.BlockSpec((None, SUBLANES, dm), lambda b, s: (b, jnp.maximum(s * (ts // SUBLANES) - 1, 0), 0)),
            pl.BlockSpec((N_SHIFT_MIX, dm), lambda b, s: (0, 0)),
        ],
        out_specs=pl.BlockSpec((N_SHIFT_MIX, ts, dm), lambda b, s: (0, b * nst + s, 0)),
        out_shape=jax.ShapeDtypeStruct((N_SHIFT_MIX, bsz * seq, dm), BF16),
        compiler_params=_params("parallel", "parallel"),
        name="token_shift_mix",
    )(x, x, mu)


def _mm_kernel(a_ref, w_ref, o_ref, wb_ref):
    @pl.when(pl.program_id(2) == 0)
    def _():
        wb_ref[...] = w_ref[...].astype(BF16)

    res = jnp.dot(a_ref[...], wb_ref[...], preferred_element_type=F32).astype(o_ref.dtype)
    if len(o_ref.shape) == 2:
        o_ref[...] = res
    else:
        for q in range(o_ref.shape[0]):
            o_ref[q] = res[:, q * LANES:(q + 1) * LANES]


def _batched_matmul(a, w, n_batch, out_dtype, tm=1024, tn=1024, rows_per_seq=None):
    _, m, kdim = a.shape
    tm = min(tm, m)
    n = w.shape[2]
    if rows_per_seq is None:
        out_spec = pl.BlockSpec((None, tm, tn), lambda c, j, i: (c, i, j))
        out_shape = jax.ShapeDtypeStruct((n_batch, m, n), out_dtype)
    else:
        tm = min(tm, rows_per_seq)
        per = rows_per_seq // tm
        out_spec = pl.BlockSpec((None, None, tn // LANES, tm, LANES), lambda c, j, i: (c, i // per, j, i % per, 0))
        out_shape = jax.ShapeDtypeStruct((n_batch, m // rows_per_seq, n // LANES, rows_per_seq, LANES), out_dtype)
    return pl.pallas_call(
        _mm_kernel,
        grid=(n_batch, n // tn, m // tm),
        in_specs=[
            pl.BlockSpec((None, tm, kdim), lambda c, j, i: (c, i, 0)),
            pl.BlockSpec((None, kdim, tn), lambda c, j, i: (c, 0, j)),
        ],
        out_specs=out_spec,
        out_shape=out_shape,
        scratch_shapes=[pltpu.VMEM((kdim, tn), BF16)],
        compiler_params=_params("parallel", "parallel", "arbitrary"),
        name="batched_matmul",
    )(a, w)


def _lora_kernel(xw_ref, xa_ref, xg_ref, w1_ref, w2_ref, w0_ref, a1_ref, a2_ref, a0_ref,
                 g1_ref, g2_ref, decay_ref, a_ref, g_ref):
    def two_stage(x_ref, p_ref, q_ref, act):
        h = act(jnp.dot(x_ref[...], p_ref[...], preferred_element_type=F32))
        return jnp.dot(h.astype(BF16), q_ref[...], preferred_element_type=F32)

    def split_lanes(o_ref, val):
        for q in range(o_ref.shape[0]):
            o_ref[q] = val[:, q * LANES:(q + 1) * LANES]

    z = w0_ref[...] + two_stage(xw_ref, w1_ref, w2_ref, jnp.tanh)
    split_lanes(decay_ref, jnp.exp(-jnp.exp(-0.5) * jax.nn.sigmoid(z)))
    split_lanes(a_ref, jax.nn.sigmoid(a0_ref[...] + two_stage(xa_ref, a1_ref, a2_ref, lambda h: h)))
    g_ref[...] = two_stage(xg_ref, g1_ref, g2_ref, jax.nn.sigmoid)


def _rwkv_lora(mixed, w1, w2, w0, a1, a2, a0, g1, g2, seq, tm=512):
    _, t, dm = mixed.shape
    tm = min(tm, seq)
    per = seq // tm

    def full(arr):
        return pl.BlockSpec(arr.shape, lambda i: (0,) * arr.ndim)

    def mix_spec(c):
        return pl.BlockSpec((None, tm, dm), lambda i: (c, i, 0))

    out_spec = pl.BlockSpec((tm, dm), lambda i: (i, 0))
    out = jax.ShapeDtypeStruct((t, dm), F32)
    split_spec = pl.BlockSpec((None, dm // LANES, tm, LANES), lambda i: (i // per, 0, i % per, 0))
    split = jax.ShapeDtypeStruct((t // seq, dm // LANES, seq, LANES), F32)
    return pl.pallas_call(
        _lora_kernel,
        grid=(t // tm,),
        in_specs=[mix_spec(3), mix_spec(4), mix_spec(5), full(w1), full(w2), full(w0),
                  full(a1), full(a2), full(a0), full(g1), full(g2)],
        out_specs=[split_spec, split_spec, out_spec],
        out_shape=[split, split, out],
        compiler_params=_params("parallel"),
        name="rwkv_lora",
    )(mixed, mixed, mixed, w1, w2, w0, a1, a2, a0, g1, g2)


SCAN_V_GROUPS = 4
SCAN_PARTIALS = 2


SCAN_BATCH = 4
SCAN_CHUNK = LANES // SCAN_BATCH


def _swap_halves(a0, a1, a2, a3, low_half):
    r0, r1, r2, r3 = (pltpu.roll(a, 2 * SCAN_CHUNK, 1) for a in (a0, a1, a2, a3))
    return (jnp.where(low_half, a0, r2), jnp.where(low_half, a1, r3),
            jnp.where(low_half, r0, a2), jnp.where(low_half, r1, a3))


def _swap_chunks(p0, p1, p2, p3, even_chunk):
    return (jnp.where(even_chunk, p0, pltpu.roll(p1, SCAN_CHUNK, 1)),
            jnp.where(even_chunk, pltpu.roll(p0, 3 * SCAN_CHUNK, 1), p1),
            jnp.where(even_chunk, p2, pltpu.roll(p3, SCAN_CHUNK, 1)),
            jnp.where(even_chunk, pltpu.roll(p2, 3 * SCAN_CHUNK, 1), p3))


def _chunk_transpose_direct(ins, at_chunk):
    outs = []
    for j in range(SCAN_BATCH):
        picked = [ins[b] if b == j else pltpu.roll(ins[b], ((b - j) % SCAN_BATCH) * SCAN_CHUNK, 1)
                  for b in range(SCAN_BATCH)]
        out = picked[SCAN_BATCH - 1]
        for b in range(SCAN_BATCH - 2, -1, -1):
            out = jnp.where(at_chunk[b], picked[b], out)
        outs.append(out)
    return tuple(outs)


Q_KK, Q_WR, Q_W, Q_B, Q_K2, Q_V = range(6)
R_INV2, R_BR, R_KR, R_BONUS = range(4)
SCAN_UNIT_Q = 2
SCAN_Q = RWKV_HEAD_SIZE // SCAN_BATCH
SCAN_STEP_UNROLL = 1


def _scan_pipelined_kernel(r_ref, k_ref, v_ref, w_ref, a_ref, kkp_ref, kap_ref, rkp_ref, gng_ref, gnb_ref,
                           y_ref, st_ref, buf_a, buf_b, rows_a, rows_b, y_s, yn_s, *, steps):
    n = RWKV_HEAD_SIZE
    n_vg = n // SUBLANES
    n_groups = steps // SUBLANES
    lane = lax.broadcasted_iota(jnp.int32, (SUBLANES, LANES), 1)
    low_half = lane < 2 * SCAN_CHUNK
    even_chunk = (lane & SCAN_CHUNK) == 0
    at_chunk = [(lane >= b * SCAN_CHUNK) & (lane < (b + 1) * SCAN_CHUNK) for b in range(SCAN_BATCH)]

    @pl.when(pl.program_id(0) == 0)
    def _():
        st_ref[...] = jnp.zeros_like(st_ref)
        yn_s[...] = jnp.zeros_like(yn_s)

    def tile_rows(i):
        if isinstance(i, int):
            return pl.ds(i * SUBLANES, SUBLANES)
        return pl.ds(pl.multiple_of(i * SUBLANES, SUBLANES), SUBLANES)

    def transposed(tiles_per_q, direct):
        if direct:
            return [_chunk_transpose_direct(t, at_chunk) for t in tiles_per_q]
        halves = [_swap_halves(*t, low_half) for t in tiles_per_q]
        return [_swap_chunks(*h, even_chunk) for h in halves]

    def regroup(s0, qs, dst, direct):
        srcs = (r_ref, k_ref, v_ref, w_ref, a_ref)
        tiles = [tuple(src[bi, q, pl.ds(s0, SUBLANES), :] for bi in range(SCAN_BATCH))
                 for q in qs for src in srcs]
        regrouped = transposed(tiles, direct)
        sums = None
        for qi, q in enumerate(qs):
            rq, kq, vq, wq, aq = regrouped[qi * len(srcs):(qi + 1) * len(srcs)]
            for j in range(SCAN_BATCH):
                rows = tile_rows(q * SCAN_BATCH + j)
                r, k, v, w, a = rq[j], kq[j], vq[j], wq[j], aq[j]
                kk0 = k * kkp_ref[rows, :]
                k2 = k * (1.0 + (a - 1.0) * kap_ref[rows, :])
                b0 = kk0 * a
                rk2 = r * k2
                dst[Q_KK, rows, :] = kk0
                dst[Q_WR, rows, :] = w * r
                dst[Q_W, rows, :] = w
                dst[Q_B, rows, :] = b0
                dst[Q_K2, rows, :] = k2
                dst[Q_V, rows, :] = v
                terms = (kk0 * kk0, b0 * r, rk2, rk2 * rkp_ref[rows, :])
                sums = terms if sums is None else tuple(x + y for x, y in zip(sums, terms))
        return sums

    def finish_rows(sums, rows_ref):
        ss, br, kr, bonus = sums
        rows_ref[R_INV2] = 1.0 / jnp.maximum(ss, 1e-24)
        rows_ref[R_BR] = br
        rows_ref[R_KR] = kr
        rows_ref[R_BONUS] = bonus

    def unregroup(s0, qs, direct):
        tiles = [tuple(yn_s[tile_rows(q * SCAN_BATCH + j), :] for j in range(SCAN_BATCH)) for q in qs]
        for q, outs in zip(qs, transposed(tiles, direct)):
            for bi, val in enumerate(outs):
                y_ref[bi, q, pl.ds(s0, SUBLANES), :] = val

    def step(t, cur, cur_rows):
        inv2_row = cur_rows[R_INV2, pl.ds(t, 1), :]
        br_row = cur_rows[R_BR, pl.ds(t, 1), :]
        kr_row = cur_rows[R_KR, pl.ds(t, 1), :]
        for g0 in range(0, n_vg, SCAN_V_GROUPS):
            groups = range(g0, g0 + SCAN_V_GROUPS)
            acc_sa = {g: [None] * SCAN_PARTIALS for g in groups}
            acc_y = {g: [None] * SCAN_PARTIALS for g in groups}
            for ki in range(n):
                kk_row = cur[Q_KK, pl.ds(ki * SUBLANES + t, 1), :]
                wr_row = cur[Q_WR, pl.ds(ki * SUBLANES + t, 1), :]
                p = ki % SCAN_PARTIALS
                for g in groups:
                    s = st_ref[ki, g * SUBLANES:(g + 1) * SUBLANES, :]
                    t_sa = s * kk_row
                    t_y = s * wr_row
                    acc_sa[g][p] = t_sa if acc_sa[g][p] is None else acc_sa[g][p] + t_sa
                    acc_y[g][p] = t_y if acc_y[g][p] is None else acc_y[g][p] + t_y
            sa = {}
            vv = {}
            for g in groups:
                v_rows = pl.ds(g * SUBLANES * SUBLANES + t, SUBLANES, stride=SUBLANES)
                sa[g] = -functools.reduce(lambda x, y: x + y, acc_sa[g]) * inv2_row
                vv[g] = cur[Q_V, v_rows, :]
                y_s[v_rows, :] = (functools.reduce(lambda x, y: x + y, acc_y[g])
                                  + sa[g] * br_row + vv[g] * kr_row)
            for ki in range(n):
                w_row = cur[Q_W, pl.ds(ki * SUBLANES + t, 1), :]
                b_row = cur[Q_B, pl.ds(ki * SUBLANES + t, 1), :]
                k_row = cur[Q_K2, pl.ds(ki * SUBLANES + t, 1), :]
                for g in groups:
                    s = st_ref[ki, g * SUBLANES:(g + 1) * SUBLANES, :]
                    st_ref[ki, g * SUBLANES:(g + 1) * SUBLANES, :] = s * w_row + sa[g] * b_row + vv[g] * k_row

    def normalise(cur, cur_rows):
        tot = None
        for vi in range(n):
            yv = y_s[vi * SUBLANES:(vi + 1) * SUBLANES, :]
            tot = yv if tot is None else tot + yv
        mean = tot * (1.0 / n)
        sq = None
        for vi in range(n):
            yc = y_s[vi * SUBLANES:(vi + 1) * SUBLANES, :] - mean
            sq = yc * yc if sq is None else sq + yc * yc
        rstd = lax.rsqrt(sq * (1.0 / n) + GN_EPS)
        bonus_v = cur_rows[R_BONUS]
        for vi in range(n):
            rows = slice(vi * SUBLANES, (vi + 1) * SUBLANES)
            yn_s[rows, :] = ((y_s[rows, :] - mean) * rstd * gng_ref[rows, :] + gnb_ref[rows, :]
                             + bonus_v * cur[Q_V, rows, :])

    def run_group(g, cur, cur_rows, nxt, nxt_rows):
        s0 = pl.multiple_of(g * SUBLANES, SUBLANES)
        s_next = pl.multiple_of(jnp.minimum(s0 + SUBLANES, steps - SUBLANES), SUBLANES)
        s_prev = pl.multiple_of(jnp.maximum(s0 - SUBLANES, 0), SUBLANES)

        def body(i, sums):
            for u in range(SCAN_STEP_UNROLL):
                t = SCAN_STEP_UNROLL * i + u
                qs = [SCAN_UNIT_Q * t + v for v in range(SCAN_UNIT_Q)]
                part = regroup(s_next, qs, nxt, True)
                unregroup(s_prev, qs, True)
                step(t, cur, cur_rows)
                sums = tuple(x + y for x, y in zip(sums, part))
            return sums

        zero = jnp.zeros((SUBLANES, LANES), F32)
        sums = lax.fori_loop(0, SUBLANES // SCAN_STEP_UNROLL, body, (zero, zero, zero, zero))
        finish_rows(sums, nxt_rows)
        normalise(cur, cur_rows)

    half = SCAN_Q // 2
    first = regroup(0, list(range(half)), buf_a, False)
    second = regroup(0, list(range(half, SCAN_Q)), buf_a, False)
    finish_rows(tuple(x + y for x, y in zip(first, second)), rows_a)

    def two_groups(i, carry):
        run_group(2 * i, buf_a, rows_a, buf_b, rows_b)
        run_group(2 * i + 1, buf_b, rows_b, buf_a, rows_a)
        return carry

    lax.fori_loop(0, n_groups // 2, two_groups, 0)

    unregroup(steps - SUBLANES, list(range(half)), False)
    unregroup(steps - SUBLANES, list(range(half, SCAN_Q)), False)


def _rwkv_scan_pipelined(rkv, w, a, kkp, kap, rkp, gng, gnb, steps=64):
    bsz, nq, seq, lanes = w.shape
    n = RWKV_HEAD_SIZE
    assert bsz == SCAN_BATCH and nq == SCAN_Q and lanes == LANES and (steps // SUBLANES) % 2 == 0

    def rkv_spec(c):
        return pl.BlockSpec((None, bsz, nq, steps, lanes), lambda i: (c, 0, 0, i, 0))

    seq_spec = pl.BlockSpec((bsz, nq, steps, lanes), lambda i: (0, 0, i, 0))
    par_spec = pl.BlockSpec((n * SUBLANES, LANES), lambda i: (0, 0))
    rows = pltpu.VMEM((n * SUBLANES, LANES), F32)
    group_buf = pltpu.VMEM((6, n * SUBLANES, LANES), F32)
    group_rows = pltpu.VMEM((4, SUBLANES, LANES), F32)
    return pl.pallas_call(
        functools.partial(_scan_pipelined_kernel, steps=steps),
        grid=(seq // steps,),
        in_specs=[rkv_spec(0), rkv_spec(1), rkv_spec(2), seq_spec, seq_spec] + [par_spec] * 5,
        out_specs=seq_spec,
        out_shape=jax.ShapeDtypeStruct((bsz, nq, seq, lanes), F32),
        scratch_shapes=[pltpu.VMEM((n, n, LANES), F32), group_buf, group_buf, group_rows, group_rows, rows, rows],
        compiler_params=_params("arbitrary"),
        name="rwkv7_scan",
    )(rkv, rkv, rkv, w, a, kkp, kap, rkp, gng, gnb)


def _mm_ln_kernel(*refs, gated, emit_bf16):
    refs = list(refs)
    a_ref = refs.pop(0)
    gate_ref = refs.pop(0) if gated else None
    w_ref, x_ref, g_ref, b_ref, o_ref = refs[:5]
    ob_ref = refs[5] if emit_bf16 else None
    acc_ref = refs[-1]

    if gated:
        a = jnp.concatenate([a_ref[q] for q in range(a_ref.shape[0])], axis=1)
        a = (a * gate_ref[...]).astype(BF16)
    else:
        a = a_ref[...]

    panel_rows = min(MM_LN_PANEL_ROWS, acc_ref.shape[0])
    for p in range(0, acc_ref.shape[0], panel_rows):
        panel = slice(p, p + panel_rows)
        acc_ref[panel, :] = jnp.dot(a[panel, :], w_ref[...], preferred_element_type=F32)
        for c in range(p // LN_CHUNK_ROWS, (p + panel_rows) // LN_CHUNK_ROWS):
            rows = slice(c * LN_CHUNK_ROWS, (c + 1) * LN_CHUNK_ROWS)
            z = DEEPNORM_ALPHA * x_ref[rows, :] + acc_ref[rows, :]
            mu = jnp.mean(z, axis=-1, keepdims=True)
            zc = z - mu
            var = jnp.mean(zc * zc, axis=-1, keepdims=True)
            out = zc * lax.rsqrt(var + LN_EPS) * g_ref[...] + b_ref[...]
            o_ref[rows, :] = out
            if emit_bf16:
                ob_ref[rows, :] = out.astype(BF16)


def _matmul_residual_ln(a, w, x, g, b, *, gate=None, tm, emit_bf16=True, layer=0):
    if w.ndim == 2:
        w, layer = w[None], 0
    kdim, n = w.shape[1:]
    m = x.shape[0]
    row = lambda i: (i, 0)
    fixed = lambda i: (0, 0)
    args = [a]
    if gate is None:
        in_specs = [pl.BlockSpec((tm, kdim), row)]
    else:
        per = a.shape[2] // tm
        in_specs = [pl.BlockSpec((None, kdim // LANES, tm, LANES), lambda i: (i // per, 0, i % per, 0)),
                    pl.BlockSpec((tm, kdim), row)]
        args.append(gate)
    in_specs += [pl.BlockSpec((None, kdim, n), lambda i: (layer, 0, 0), pipeline_mode=pl.Buffered(1)),
                 pl.BlockSpec((tm, n), row),
                 pl.BlockSpec((1, n), fixed),
                 pl.BlockSpec((1, n), fixed)]
    args += [w, x, g.reshape(1, n), b.reshape(1, n)]
    out_specs = [pl.BlockSpec((tm, n), row)]
    out_shape = [jax.ShapeDtypeStruct((m, n), F32)]
    if emit_bf16:
        out_specs.append(pl.BlockSpec((tm, n), row))
        out_shape.append(jax.ShapeDtypeStruct((m, n), BF16))
    outs = pl.pallas_call(
        functools.partial(_mm_ln_kernel, gated=gate is not None, emit_bf16=emit_bf16),
        grid=(m // tm,),
        in_specs=in_specs,
        out_specs=out_specs,
        out_shape=out_shape,
        scratch_shapes=[pltpu.VMEM((tm, n), F32)],
        compiler_params=_params("parallel"),
        name="matmul_residual_ln",
    )(*args)
    return outs if emit_bf16 else (outs[0], None)


FFN_COL_CHUNK = 256
FFN_PANEL_ROWS = 1024


def _ffn_up_kernel(x_ref, wg32_ref, wu32_ref, cw_ref, cb_ref, o_ref, tail_ref, wg_ref, wu_ref, *,
                   tiles_per_seq):
    i = pl.program_id(1)

    @pl.when(i == 0)
    def _():
        wg_ref[...] = wg32_ref[...].astype(BF16)
        wu_ref[...] = wu32_ref[...].astype(BF16)

    seq_start = i % tiles_per_seq == 0
    row = lax.broadcasted_iota(jnp.int32, (SUBLANES, FFN_COL_CHUNK), 0)
    n_chunks = o_ref.shape[1] // FFN_COL_CHUNK
    tails = [None] * n_chunks
    for p in range(0, x_ref.shape[0], FFN_PANEL_ROWS):
        panel = slice(p, p + FFN_PANEL_ROWS)
        x = x_ref[panel, :]
        for c in range(n_chunks):
            cols = slice(c * FFN_COL_CHUNK, (c + 1) * FFN_COL_CHUNK)
            gate = jnp.dot(x, wg_ref[:, cols], preferred_element_type=F32)
            up = jnp.dot(x, wu_ref[:, cols], preferred_element_type=F32)
            tail = jnp.where(seq_start, 0.0, tail_ref[:, cols]) if p == 0 else tails[c]
            tails[c] = gate[FFN_PANEL_ROWS - SUBLANES:, :]
            r1 = pltpu.roll(gate, 1, 0)
            r2 = pltpu.roll(gate, 2, 0)
            h1 = jnp.where(row == 0, tail[SUBLANES - 1:, :], r1[:SUBLANES, :])
            h2 = jnp.where(row == 0, tail[SUBLANES - 2:SUBLANES - 1, :],
                           jnp.where(row == 1, tail[SUBLANES - 1:, :], r2[:SUBLANES, :]))
            g1 = jnp.concatenate([h1, r1[SUBLANES:, :]], axis=0)
            g2 = jnp.concatenate([h2, r2[SUBLANES:, :]], axis=0)
            acc = (cb_ref[:, cols] + g2 * cw_ref[0:1, cols] + g1 * cw_ref[1:2, cols]
                   + gate * cw_ref[2:3, cols])
            o_ref[panel, cols] = (acc * jax.nn.sigmoid(acc) * up).astype(o_ref.dtype)
    for c in range(n_chunks):
        tail_ref[:, c * FFN_COL_CHUNK:(c + 1) * FFN_COL_CHUNK] = tails[c]


def _ffn_up(xb, w_up_all, layer, conv_w, conv_b, seq, tm=2 * FFN_PANEL_ROWS, tn=512):
    t, dm = xb.shape
    dff = w_up_all.shape[2] // 2
    n_col = dff // tn
    w_up = w_up_all
    return pl.pallas_call(
        functools.partial(_ffn_up_kernel, tiles_per_seq=seq // tm),
        grid=(n_col, t // tm),
        in_specs=[
            pl.BlockSpec((tm, dm), lambda j, i: (i, 0)),
            pl.BlockSpec((None, dm, tn), lambda j, i: (layer, 0, j)),
            pl.BlockSpec((None, dm, tn), lambda j, i: (layer, 0, j + n_col)),
            pl.BlockSpec((CONV_WIDTH, tn), lambda j, i: (0, j)),
            pl.BlockSpec((1, tn), lambda j, i: (0, j)),
        ],
        out_specs=pl.BlockSpec((tm, tn), lambda j, i: (i, j)),
        out_shape=jax.ShapeDtypeStruct((t, dff), BF16),
        scratch_shapes=[pltpu.VMEM((SUBLANES, tn), F32), pltpu.VMEM((dm, tn), BF16), pltpu.VMEM((dm, tn), BF16)],
        compiler_params=_params("parallel", "arbitrary"),
        name="ffn_up_convglu",
    )(xb, w_up, w_up, conv_w, conv_b.reshape(1, dff))


ATTN_SPAN = 128
ATTN_TILE = 2048
ATTN_MERGE_ROWS = 256
LOG2_E = 1.4426950408889634


def _block_attention(q, k, v, valid):
    scale = ATTN_HEAD_DIM ** -0.5
    s = lax.dot_general(q.astype(BF16), k.astype(BF16), (((1,), (1,)), ((), ())), preferred_element_type=F32)
    s = jnp.where(valid, s, MASK_VALUE)
    m = jnp.max(s, axis=-1, keepdims=True)
    p = jnp.exp2((s - m) * (scale * LOG2_E))
    l = jnp.sum(p, axis=-1, keepdims=True)
    o = jnp.dot(p.astype(BF16), v.astype(BF16), preferred_element_type=F32) * (1.0 / l)
    return o, m * scale + jnp.log(l)


def _attn_kernel(q1, k1, v1, kp1, vp1, q4, k4, v4, kp4, vp4, q16, k16, v16, kp16, vp16,
                 o_ref, og, lg, x4, *, tiles_per_seq):
    span, hd = ATTN_SPAN, ATTN_HEAD_DIM
    first_tile = pl.program_id(1) % tiles_per_seq == 0
    qi = lax.broadcasted_iota(jnp.int32, (span, 2 * span), 0)
    kj = lax.broadcasted_iota(jnp.int32, (span, 2 * span), 1)
    valid = (kj <= qi + span) & (kj >= qi)
    valid_edge = valid & (kj >= jnp.where(first_tile, span, 0))

    def emit(g, rows, q, k, v, mask):
        o, lse = _block_attention(q, k, v, mask)
        og[g, rows, :] = o
        lg[g, rows, :] = jnp.broadcast_to(lse, (span, hd))

    def cat(a, b):
        return jnp.concatenate([a, b], axis=0)

    n1 = ATTN_TILE // span
    rows = pl.ds(0, span)
    emit(0, rows, q1[rows, :], cat(kp1[...], k1[rows, :]), cat(vp1[...], v1[rows, :]), valid_edge)

    for j in range(1, n1):
        rows = pl.ds(j * span, span)
        keys = pl.ds((j - 1) * span, 2 * span)
        emit(0, rows, q1[rows, :], k1[keys, :], v1[keys, :], valid)

    d4 = 4
    nb4 = ATTN_TILE // (span * d4)
    for r in range(d4):
        rows = pl.ds(r, span, stride=d4)
        emit(1, rows, q4[rows, :], cat(kp4[rows, :], k4[rows, :]), cat(vp4[rows, :], v4[rows, :]), valid_edge)
        for j in range(1, nb4):
            rows = pl.ds(j * span * d4 + r, span, stride=d4)
            keys = pl.ds((j - 1) * span * d4 + r, 2 * span, stride=d4)
            emit(1, rows, q4[rows, :], k4[keys, :], v4[keys, :], valid)

    d16 = 16
    quarter = ATTN_TILE // d4
    for a, src in enumerate((q16, k16, v16, kp16, vp16)):
        for r4 in range(d4):
            x4[a, r4 * quarter:(r4 + 1) * quarter, :] = src[pl.ds(r4, quarter, stride=d4), :]
    for r in range(d16):
        sub = pl.ds((r % d4) * quarter + r // d4, span, stride=d4)
        o, lse = _block_attention(x4[0, sub, :], cat(x4[3, sub, :], x4[1, sub, :]),
                                  cat(x4[4, sub, :], x4[2, sub, :]), valid_edge)
        x4[0, sub, :] = o
        x4[3, sub, :] = jnp.broadcast_to(lse, (span, hd))
    for r4 in range(d4):
        og[2, pl.ds(r4, quarter, stride=d4), :] = x4[0, r4 * quarter:(r4 + 1) * quarter, :]
        lg[2, pl.ds(r4, quarter, stride=d4), :] = x4[3, r4 * quarter:(r4 + 1) * quarter, :]

    def merge(c, carry):
        rows = pl.ds(pl.multiple_of(c * ATTN_MERGE_ROWS, ATTN_MERGE_ROWS), ATTN_MERGE_ROWS)
        l0, l1, l2 = lg[0, rows, :], lg[1, rows, :], lg[2, rows, :]
        m = jnp.maximum(jnp.maximum(l0, l1), l2)
        e0, e1, e2 = jnp.exp(l0 - m), jnp.exp(l1 - m), jnp.exp(l2 - m)
        num = e0 * og[0, rows, :] + e1 * og[1, rows, :] + e2 * og[2, rows, :]
        o_ref[rows, :] = (num / (e0 + e1 + e2)).astype(o_ref.dtype)
        return carry

    lax.fori_loop(0, ATTN_TILE // ATTN_MERGE_ROWS, merge, 0)


def _dilated_attention(qkv, seq):
    t, width = qkv.shape
    hd, span, tile = ATTN_HEAD_DIM, ATTN_SPAN, ATTN_TILE
    heads = width // (3 * len(DILATED_GROUPS) * hd)
    tiles_per_seq = seq // tile

    def col(group, which):
        return lambda h: (group * 3 + which) * heads + h

    def cur(group, which):
        return pl.BlockSpec((tile, hd), lambda h, i: (i, col(group, which)(h)))

    def prev(group, which, rows):
        per = tile // rows
        return pl.BlockSpec((rows, hd), lambda h, i: (jnp.maximum(i * per - 1, 0), col(group, which)(h)))

    in_specs = []
    for group, (window, dilation) in enumerate(DILATED_GROUPS):
        assert window // dilation == span and tile % window == 0
        in_specs += [cur(group, 0), cur(group, 1), cur(group, 2), prev(group, 1, window), prev(group, 2, window)]
    return pl.pallas_call(
        functools.partial(_attn_kernel, tiles_per_seq=tiles_per_seq),
        grid=(heads, t // tile),
        in_specs=in_specs,
        out_specs=pl.BlockSpec((tile, hd), lambda h, i: (i, h)),
        out_shape=jax.ShapeDtypeStruct((t, heads * hd), BF16),
        scratch_shapes=[pltpu.VMEM((len(DILATED_GROUPS), tile, hd), F32)] * 2 + [pltpu.VMEM((5, tile, hd), F32)],
        compiler_params=_params("parallel", "parallel"),
        name="dilated_attention",
    )(*([qkv] * len(in_specs)))


def _pad_rank(p, q):
    r = p.shape[1]
    if r % LORA_PAD:
        extra = LORA_PAD - r % LORA_PAD
        p = jnp.pad(p, ((0, 0), (0, extra)))
        q = jnp.pad(q, ((0, extra), (0, 0)))
    return p.astype(BF16), q.astype(BF16)


def _rwkv_layer(x, mu, w_rkv, w0, w1, w2, a0, a1, a2, g1, g2, k_k, k_a, r_k, gn_g, gn_b, w_out, ln_g, ln_b):
    bsz, seq, dm = x.shape
    n = RWKV_HEAD_SIZE
    heads = dm // n
    t = bsz * seq
    def cols(wt):
        lead = wt.shape[:-1]
        return wt.reshape(*lead, heads, n).swapaxes(-1, -2).reshape(*lead, dm)

    mixed = _token_shift_mix(x, mu[jnp.array([0, 2, 3, 1, 4, 5])])
    rkv = _batched_matmul(mixed, cols(w_rkv.astype(BF16)), 3, F32, rows_per_seq=seq)
    w1b, w2b = _pad_rank(w1, cols(w2))
    a1b, a2b = _pad_rank(a1, cols(a2))
    g1b, g2b = _pad_rank(g1, cols(g2))
    decay, a, g = _rwkv_lora(mixed, w1b, w2b, cols(w0).reshape(1, dm), a1b, a2b, cols(a0).reshape(1, dm),
                             g1b, g2b, seq)

    def lanes_param(p):
        return jnp.repeat(jnp.tile(p.reshape(heads, n).T, (1, bsz)), SUBLANES, axis=0)

    y = _rwkv_scan_pipelined(rkv, decay, a, lanes_param(k_k), lanes_param(k_a), lanes_param(r_k),
                             lanes_param(gn_g), lanes_param(gn_b), steps=min(64, seq))
    w_out_rows = w_out.reshape(heads, n, dm).swapaxes(0, 1).reshape(dm, dm)
    return _matmul_residual_ln(y, w_out_rows.astype(BF16), x.reshape(t, dm), ln_g, ln_b, gate=g,
                               tm=min(512, seq))


def _attn_layer(x32, xb, w_in, w_out, ln_g, ln_b, bsz, seq):
    t, dm = x32.shape
    qkv = _batched_matmul(xb[None], w_in[None], 1, F32)[0]
    merged = _dilated_attention(qkv, seq)
    return _matmul_residual_ln(merged, w_out.astype(BF16), x32, ln_g, ln_b, tm=512)


def _ffn_layer(x32, xb, w_up_all, layer, conv_w, conv_b, w_down_all, ln_g, ln_b, seq, emit_bf16):
    act = _ffn_up(xb, w_up_all, layer, conv_w, conv_b, seq)
    return _matmul_residual_ln(act, w_down_all.astype(BF16), x32, ln_g, ln_b, tm=256, emit_bf16=emit_bf16,
                               layer=layer)


def kernel(x, rwkv_mu, rwkv_w_rkv, rwkv_w0, rwkv_w1, rwkv_w2, rwkv_a0, rwkv_a1, rwkv_a2, rwkv_g1, rwkv_g2, rwkv_k_k, rwkv_k_a, rwkv_r_k, rwkv_gn_g, rwkv_gn_b, rwkv_w_out, attn_w_in, attn_w_out, ffn_w_up, ffn_conv_w, ffn_conv_b, ffn_w_down, ln_mix_g, ln_mix_b, ln_ffn_g, ln_ffn_b):
    bsz, seq, dm = x.shape
    x32, xb = _rwkv_layer(x, rwkv_mu[0], rwkv_w_rkv[0], rwkv_w0[0], rwkv_w1[0], rwkv_w2[0], rwkv_a0[0],
                          rwkv_a1[0], rwkv_a2[0], rwkv_g1[0], rwkv_g2[0], rwkv_k_k[0], rwkv_k_a[0],
                          rwkv_r_k[0], rwkv_gn_g[0], rwkv_gn_b[0], rwkv_w_out[0], ln_mix_g[0], ln_mix_b[0])
    x32, xb = _ffn_layer(x32, xb, ffn_w_up, 0, ffn_conv_w[0], ffn_conv_b[0], ffn_w_down,
                         ln_ffn_g[0], ln_ffn_b[0], seq, True)
    x32, xb = _attn_layer(x32, xb, attn_w_in[0], attn_w_out[0], ln_mix_g[1], ln_mix_b[1], bsz, seq)
    x32, _ = _ffn_layer(x32, xb, ffn_w_up, 1, ffn_conv_w[1], ffn_conv_b[1], ffn_w_down,
                        ln_ffn_g[1], ln_ffn_b[1], seq, False)
    return x32.reshape(bsz, seq, dm)
```

```python
import functools

import jax
import jax.numpy as jnp
from jax import lax
from jax.experimental import pallas as pl
from jax.experimental.pallas import tpu as pltpu

F32 = jnp.float32
BF16 = jnp.bfloat16

V7X_VMEM_BYTES = 64 * 1024 * 1024
VMEM_LIMIT_BYTES = V7X_VMEM_BYTES - 8 * 1024 * 1024
SUBLANES = 8
LANES = 128

RWKV_HEAD_SIZE = 64
N_SHIFT_MIX = 6
GN_EPS = 64e-5
ATTN_HEAD_DIM = 128
DILATED_GROUPS = ((128, 1), (512, 4), (2048, 16))
MASK_VALUE = -1e30
CONV_WIDTH = 3
LN_EPS = 1e-5
DEPTH = 2
DEEPNORM_ALPHA = (2.0 * DEPTH) ** 0.25
LORA_PAD = 128
LN_CHUNK_ROWS = 32
MM_LN_PANEL_ROWS = 256


def _params(*semantics):
    return pltpu.CompilerParams(dimension_semantics=semantics, vmem_limit_bytes=VMEM_LIMIT_BYTES)


def _mix_kernel(x_ref, xp_ref, mu_ref, o_ref):
    s = pl.program_id(1)
    x = x_ref[...]
    before = jnp.where(s > 0, xp_ref[SUBLANES - 1:SUBLANES, :], 0.0)
    prev = pltpu.roll(x, 1, 0)
    row = lax.broadcasted_iota(jnp.int32, x.shape, 0)
    prev = jnp.where(row == 0, before, prev)
    xx = prev - x
    for c in range(N_SHIFT_MIX):
        o_ref[c] = (x + xx * mu_ref[c:c + 1, :]).astype(o_ref.dtype)


def _token_shift_mix(x, mu, ts=512):
    bsz, seq, dm = x.shape
    ts = min(ts, seq)
    nst = seq // ts
    return pl.pallas_call(
        _mix_kernel,
        grid=(bsz, nst),
        in_specs=[
            pl.BlockSpec((None, ts, dm), lambda b, s: (b, s, 0)),
            pl.BlockSpec((None, SUBLANES, dm), lambda b, s: (b, jnp.maximum(s * (ts // SUBLANES) - 1, 0), 0)),
            pl.BlockSpec((N_SHIFT_MIX, dm), lambda b, s: (0, 0)),
        ],
        out_specs=pl.BlockSpec((N_SHIFT_MIX, ts, dm), lambda b, s: (0, b * nst + s, 0)),
        out_shape=jax.ShapeDtypeStruct((N_SHIFT_MIX, bsz * seq, dm), BF16),
        compiler_params=_params("parallel", "parallel"),
        name="token_shift_mix",
    )(x, x, mu)


def _mm_kernel(a_ref, w_ref, o_ref, wb_ref):
    @pl.when(pl.program_id(2) == 0)
    def _():
        wb_ref[...] = w_ref[...].astype(BF16)

    res = jnp.dot(a_ref[...], wb_ref[...], preferred_element_type=F32).astype(o_ref.dtype)
    if len(o_ref.shape) == 2:
        o_ref[...] = res
    else:
        for q in range(o_ref.shape[0]):
            o_ref[q] = res[:, q * LANES:(q + 1) * LANES]


def _batched_matmul(a, w, n_batch, out_dtype, tm=1024, tn=1024, rows_per_seq=None):
    _, m, kdim = a.shape
    tm = min(tm, m)
    n = w.shape[2]
    if rows_per_seq is None:
        out_spec = pl.BlockSpec((None, tm, tn), lambda c, j, i: (c, i, j))
        out_shape = jax.ShapeDtypeStruct((n_batch, m, n), out_dtype)
    else:
        tm = min(tm, rows_per_seq)
        per = rows_per_seq // tm
        out_spec = pl.BlockSpec((None, None, tn // LANES, tm, LANES), lambda c, j, i: (c, i // per, j, i % per, 0))
        out_shape = jax.ShapeDtypeStruct((n_batch, m // rows_per_seq, n // LANES, rows_per_seq, LANES), out_dtype)
    return pl.pallas_call(
        _mm_kernel,
        grid=(n_batch, n // tn, m // tm),
        in_specs=[
            pl.BlockSpec((None, tm, kdim), lambda c, j, i: (c, i, 0)),
            pl.BlockSpec((None, kdim, tn), lambda c, j, i: (c, 0, j)),
        ],
        out_specs=out_spec,
        out_shape=out_shape,
        scratch_shapes=[pltpu.VMEM((kdim, tn), BF16)],
        compiler_params=_params("parallel", "parallel", "arbitrary"),
        name="batched_matmul",
    )(a, w)


def _lora_kernel(xw_ref, xa_ref, xg_ref, w1_ref, w2_ref, w0_ref, a1_ref, a2_ref, a0_ref,
                 g1_ref, g2_ref, decay_ref, a_ref, g_ref):
    def two_stage(x_ref, p_ref, q_ref, act):
        h = act(jnp.dot(x_ref[...], p_ref[...], preferred_element_type=F32))
        return jnp.dot(h.astype(BF16), q_ref[...], preferred_element_type=F32)

    def split_lanes(o_ref, val):
        for q in range(o_ref.shape[0]):
            o_ref[q] = val[:, q * LANES:(q + 1) * LANES]

    z = w0_ref[...] + two_stage(xw_ref, w1_ref, w2_ref, jnp.tanh)
    split_lanes(decay_ref, jnp.exp(-jnp.exp(-0.5) * jax.nn.sigmoid(z)))
    split_lanes(a_ref, jax.nn.sigmoid(a0_ref[...] + two_stage(xa_ref, a1_ref, a2_ref, lambda h: h)))
    g_ref[...] = two_stage(xg_ref, g1_ref, g2_ref, jax.nn.sigmoid)


def _rwkv_lora(mixed, w1, w2, w0, a1, a2, a0, g1, g2, seq, tm=512):
    _, t, dm = mixed.shape
    tm = min(tm, seq)
    per = seq // tm

    def full(arr):
        return pl.BlockSpec(arr.shape, lambda i: (0,) * arr.ndim)

    def mix_spec(c):
        return pl.BlockSpec((None, tm, dm), lambda i: (c, i, 0))

    out_spec = pl.BlockSpec((tm, dm), lambda i: (i, 0))
    out = jax.ShapeDtypeStruct((t, dm), F32)
    split_spec = pl.BlockSpec((None, dm // LANES, tm, LANES), lambda i: (i // per, 0, i % per, 0))
    split = jax.ShapeDtypeStruct((t // seq, dm // LANES, seq, LANES), F32)
    return pl.pallas_call(
        _lora_kernel,
        grid=(t // tm,),
        in_specs=[mix_spec(3), mix_spec(4), mix_spec(5), full(w1), full(w2), full(w0),
                  full(a1), full(a2), full(a0), full(g1), full(g2)],
        out_specs=[split_spec, split_spec, out_spec],
        out_shape=[split, split, out],
        compiler_params=_params("parallel"),
        name="rwkv_lora",
    )(mixed, mixed, mixed, w1, w2, w0, a1, a2, a0, g1, g2)


SCAN_V_GROUPS = 4
SCAN_PARTIALS = 2


SCAN_BATCH = 4
SCAN_CHUNK = LANES // SCAN_BATCH


def _swap_halves(a0, a1, a2, a3, low_half):
    r0, r1, r2, r3 = (pltpu.roll(a, 2 * SCAN_CHUNK, 1) for a in (a0, a1, a2, a3))
    return (jnp.where(low_half, a0, r2), jnp.where(low_half, a1, r3),
            jnp.where(low_half, r0, a2), jnp.where(low_half, r1, a3))


def _swap_chunks(p0, p1, p2, p3, even_chunk):
    return (jnp.where(even_chunk, p0, pltpu.roll(p1, SCAN_CHUNK, 1)),
            jnp.where(even_chunk, pltpu.roll(p0, 3 * SCAN_CHUNK, 1), p1),
            jnp.where(even_chunk, p2, pltpu.roll(p3, SCAN_CHUNK, 1)),
            jnp.where(even_chunk, pltpu.roll(p2, 3 * SCAN_CHUNK, 1), p3))


def _chunk_transpose_direct(ins, at_chunk):
    outs = []
    for j in range(SCAN_BATCH):
        picked = [ins[b] if b == j else pltpu.roll(ins[b], ((b - j) % SCAN_BATCH) * SCAN_CHUNK, 1)
                  for b in range(SCAN_BATCH)]
        out = picked[SCAN_BATCH - 1]
        for b in range(SCAN_BATCH - 2, -1, -1):
            out = jnp.where(at_chunk[b], picked[b], out)
        outs.append(out)
    return tuple(outs)


Q_KK, Q_WR, Q_W, Q_B, Q_K2, Q_V = range(6)
R_INV2, R_BR, R_KR, R_BONUS = range(4)
SCAN_UNIT_Q = 2
SCAN_Q = RWKV_HEAD_SIZE // SCAN_BATCH


def _scan_pipelined_kernel(r_ref, k_ref, v_ref, w_ref, a_ref, kkp_ref, kap_ref, rkp_ref, gng_ref, gnb_ref,
                           y_ref, st_ref, buf_a, buf_b, rows_a, rows_b, y_s, yn_s, *, steps):
    n = RWKV_HEAD_SIZE
    n_vg = n // SUBLANES
    n_groups = steps // SUBLANES
    lane = lax.broadcasted_iota(jnp.int32, (SUBLANES, LANES), 1)
    low_half = lane < 2 * SCAN_CHUNK
    even_chunk = (lane & SCAN_CHUNK) == 0
    at_chunk = [(lane >= b * SCAN_CHUNK) & (lane < (b + 1) * SCAN_CHUNK) for b in range(SCAN_BATCH)]

    @pl.when(pl.program_id(0) == 0)
    def _():
        st_ref[...] = jnp.zeros_like(st_ref)
        yn_s[...] = jnp.zeros_like(yn_s)

    def tile_rows(i):
        if isinstance(i, int):
            return pl.ds(i * SUBLANES, SUBLANES)
        return pl.ds(pl.multiple_of(i * SUBLANES, SUBLANES), SUBLANES)

    def transposed(tiles_per_q, direct):
        if direct:
            return [_chunk_transpose_direct(t, at_chunk) for t in tiles_per_q]
        halves = [_swap_halves(*t, low_half) for t in tiles_per_q]
        return [_swap_chunks(*h, even_chunk) for h in halves]

    def regroup(s0, qs, dst, direct):
        srcs = (r_ref, k_ref, v_ref, w_ref, a_ref)
        tiles = [tuple(src[bi, q, pl.ds(s0, SUBLANES), :] for bi in range(SCAN_BATCH))
                 for q in qs for src in srcs]
        regrouped = transposed(tiles, direct)
        sums = None
        for qi, q in enumerate(qs):
            rq, kq, vq, wq, aq = regrouped[qi * len(srcs):(qi + 1) * len(srcs)]
            for j in range(SCAN_BATCH):
                rows = tile_rows(q * SCAN_BATCH + j)
                r, k, v, w, a = rq[j], kq[j], vq[j], wq[j], aq[j]
                kk0 = k * kkp_ref[rows, :]
                k2 = k * (1.0 + (a - 1.0) * kap_ref[rows, :])
                b0 = kk0 * a
                rk2 = r * k2
                dst[Q_KK, rows, :] = kk0
                dst[Q_WR, rows, :] = w * r
                dst[Q_W, rows, :] = w
                dst[Q_B, rows, :] = b0
                dst[Q_K2, rows, :] = k2
                dst[Q_V, rows, :] = v
                terms = (kk0 * kk0, b0 * r, rk2, rk2 * rkp_ref[rows, :])
                sums = terms if sums is None else tuple(x + y for x, y in zip(sums, terms))
        return sums

    def finish_rows(sums, rows_ref):
        ss, br, kr, bonus = sums
        rows_ref[R_INV2] = 1.0 / jnp.maximum(ss, 1e-24)
        rows_ref[R_BR] = br
        rows_ref[R_KR] = kr
        rows_ref[R_BONUS] = bonus

    def unregroup(s0, qs, direct):
        tiles = [tuple(yn_s[tile_rows(q * SCAN_BATCH + j), :] for j in range(SCAN_BATCH)) for q in qs]
        for q, outs in zip(qs, transposed(tiles, direct)):
            for bi, val in enumerate(outs):
                y_ref[bi, q, pl.ds(s0, SUBLANES), :] = val

    def step(t, cur, cur_rows):
        inv2_row = cur_rows[R_INV2, pl.ds(t, 1), :]
        br_row = cur_rows[R_BR, pl.ds(t, 1), :]
        kr_row = cur_rows[R_KR, pl.ds(t, 1), :]
        for g0 in range(0, n_vg, SCAN_V_GROUPS):
            groups = range(g0, g0 + SCAN_V_GROUPS)
            acc_sa = {g: [None] * SCAN_PARTIALS for g in groups}
            acc_y = {g: [None] * SCAN_PARTIALS for g in groups}
            for ki in range(n):
                kk_row = cur[Q_KK, pl.ds(ki * SUBLANES + t, 1), :]
                wr_row = cur[Q_WR, pl.ds(ki * SUBLANES + t, 1), :]
                p = ki % SCAN_PARTIALS
                for g in groups:
                    s = st_ref[ki, g * SUBLANES:(g + 1) * SUBLANES, :]
                    t_sa = s * kk_row
                    t_y = s * wr_row
                    acc_sa[g][p] = t_sa if acc_sa[g][p] is None else acc_sa[g][p] + t_sa
                    acc_y[g][p] = t_y if acc_y[g][p] is None else acc_y[g][p] + t_y
            sa = {}
            vv = {}
            for g in groups:
                v_rows = pl.ds(g * SUBLANES * SUBLANES + t, SUBLANES, stride=SUBLANES)
                sa[g] = -functools.reduce(lambda x, y: x + y, acc_sa[g]) * inv2_row
                vv[g] = cur[Q_V, v_rows, :]
                y_s[v_rows, :] = (functools.reduce(lambda x, y: x + y, acc_y[g])
                                  + sa[g] * br_row + vv[g] * kr_row)
            for ki in range(n):
                w_row = cur[Q_W, pl.ds(ki * SUBLANES + t, 1), :]
                b_row = cur[Q_B, pl.ds(ki * SUBLANES + t, 1), :]
                k_row = cur[Q_K2, pl.ds(ki * SUBLANES + t, 1), :]
                for g in groups:
                    s = st_ref[ki, g * SUBLANES:(g + 1) * SUBLANES, :]
                    st_ref[ki, g * SUBLANES:(g + 1) * SUBLANES, :] = s * w_row + sa[g] * b_row + vv[g] * k_row

    def normalise(cur, cur_rows):
        tot = None
        for vi in range(n):
            yv = y_s[vi * SUBLANES:(vi + 1) * SUBLANES, :]
            tot = yv if tot is None else tot + yv
        mean = tot * (1.0 / n)
        sq = None
        for vi in range(n):
            yc = y_s[vi * SUBLANES:(vi + 1) * SUBLANES, :] - mean
            sq = yc * yc if sq is None else sq + yc * yc
        rstd = lax.rsqrt(sq * (1.0 / n) + GN_EPS)
        bonus_v = cur_rows[R_BONUS]
        for vi in range(n):
            rows = slice(vi * SUBLANES, (vi + 1) * SUBLANES)
            yn_s[rows, :] = ((y_s[rows, :] - mean) * rstd * gng_ref[rows, :] + gnb_ref[rows, :]
                             + bonus_v * cur[Q_V, rows, :])

    def run_group(g, cur, cur_rows, nxt, nxt_rows):
        s0 = pl.multiple_of(g * SUBLANES, SUBLANES)
        s_next = pl.multiple_of(jnp.minimum(s0 + SUBLANES, steps - SUBLANES), SUBLANES)
        s_prev = pl.multiple_of(jnp.maximum(s0 - SUBLANES, 0), SUBLANES)

        def body(t, sums):
            qs = [SCAN_UNIT_Q * t + v for v in range(SCAN_UNIT_Q)]
            part = regroup(s_next, qs, nxt, True)
            unregroup(s_prev, qs, True)
            step(t, cur, cur_rows)
            return tuple(x + y for x, y in zip(sums, part))

        zero = jnp.zeros((SUBLANES, LANES), F32)
        sums = lax.fori_loop(0, SUBLANES, body, (zero, zero, zero, zero))
        finish_rows(sums, nxt_rows)
        normalise(cur, cur_rows)

    half = SCAN_Q // 2
    first = regroup(0, list(range(half)), buf_a, False)
    second = regroup(0, list(range(half, SCAN_Q)), buf_a, False)
    finish_rows(tuple(x + y for x, y in zip(first, second)), rows_a)

    def two_groups(i, carry):
        run_group(2 * i, buf_a, rows_a, buf_b, rows_b)
        run_group(2 * i + 1, buf_b, rows_b, buf_a, rows_a)
        return carry

    lax.fori_loop(0, n_groups // 2, two_groups, 0)

    unregroup(steps - SUBLANES, list(range(half)), False)
    unregroup(steps - SUBLANES, list(range(half, SCAN_Q)), False)


def _rwkv_scan_pipelined(rkv, w, a, kkp, kap, rkp, gng, gnb, steps=64):
    bsz, nq, seq, lanes = w.shape
    n = RWKV_HEAD_SIZE
    assert bsz == SCAN_BATCH and nq == SCAN_Q and lanes == LANES and (steps // SUBLANES) % 2 == 0

    def rkv_spec(c):
        return pl.BlockSpec((None, bsz, nq, steps, lanes), lambda i: (c, 0, 0, i, 0))

    seq_spec = pl.BlockSpec((bsz, nq, steps, lanes), lambda i: (0, 0, i, 0))
    par_spec = pl.BlockSpec((n * SUBLANES, LANES), lambda i: (0, 0))
    rows = pltpu.VMEM((n * SUBLANES, LANES), F32)
    group_buf = pltpu.VMEM((6, n * SUBLANES, LANES), F32)
    group_rows = pltpu.VMEM((4, SUBLANES, LANES), F32)
    return pl.pallas_call(
        functools.partial(_scan_pipelined_kernel, steps=steps),
        grid=(seq // steps,),
        in_specs=[rkv_spec(0), rkv_spec(1), rkv_spec(2), seq_spec, seq_spec] + [par_spec] * 5,
        out_specs=seq_spec,
        out_shape=jax.ShapeDtypeStruct((bsz, nq, seq, lanes), F32),
        scratch_shapes=[pltpu.VMEM((n, n, LANES), F32), group_buf, group_buf, group_rows, group_rows, rows, rows],
        compiler_params=_params("arbitrary"),
        name="rwkv7_scan",
    )(rkv, rkv, rkv, w, a, kkp, kap, rkp, gng, gnb)


def _mm_ln_kernel(*refs, gated, emit_bf16):
    refs = list(refs)
    a_ref = refs.pop(0)
    gate_ref = refs.pop(0) if gated else None
    w_ref, x_ref, g_ref, b_ref, o_ref = refs[:5]
    ob_ref = refs[5] if emit_bf16 else None
    acc_ref = refs[-1]

    if gated:
        a = jnp.concatenate([a_ref[q] for q in range(a_ref.shape[0])], axis=1)
        a = (a * gate_ref[...]).astype(BF16)
    else:
        a = a_ref[...]

    panel_rows = min(MM_LN_PANEL_ROWS, acc_ref.shape[0])
    for p in range(0, acc_ref.shape[0], panel_rows):
        panel = slice(p, p + panel_rows)
        acc_ref[panel, :] = jnp.dot(a[panel, :], w_ref[...], preferred_element_type=F32)
        for c in range(p // LN_CHUNK_ROWS, (p + panel_rows) // LN_CHUNK_ROWS):
            rows = slice(c * LN_CHUNK_ROWS, (c + 1) * LN_CHUNK_ROWS)
            z = DEEPNORM_ALPHA * x_ref[rows, :] + acc_ref[rows, :]
            mu = jnp.mean(z, axis=-1, keepdims=True)
            zc = z - mu
            var = jnp.mean(zc * zc, axis=-1, keepdims=True)
            out = zc * lax.rsqrt(var + LN_EPS) * g_ref[...] + b_ref[...]
            o_ref[rows, :] = out
            if emit_bf16:
                ob_ref[rows, :] = out.astype(BF16)


def _matmul_residual_ln(a, w, x, g, b, *, gate=None, tm, emit_bf16=True, layer=0):
    if w.ndim == 2:
        w, layer = w[None], 0
    kdim, n = w.shape[1:]
    m = x.shape[0]
    row = lambda i: (i, 0)
    fixed = lambda i: (0, 0)
    args = [a]
    if gate is None:
        in_specs = [pl.BlockSpec((tm, kdim), row)]
    else:
        per = a.shape[2] // tm
        in_specs = [pl.BlockSpec((None, kdim // LANES, tm, LANES), lambda i: (i // per, 0, i % per, 0)),
                    pl.BlockSpec((tm, kdim), row)]
        args.append(gate)
    in_specs += [pl.BlockSpec((None, kdim, n), lambda i: (layer, 0, 0), pipeline_mode=pl.Buffered(1)),
                 pl.BlockSpec((tm, n), row),
                 pl.BlockSpec((1, n), fixed),
                 pl.BlockSpec((1, n), fixed)]
    args += [w, x, g.reshape(1, n), b.reshape(1, n)]
    out_specs = [pl.BlockSpec((tm, n), row)]
    out_shape = [jax.ShapeDtypeStruct((m, n), F32)]
    if emit_bf16:
        out_specs.append(pl.BlockSpec((tm, n), row))
        out_shape.append(jax.ShapeDtypeStruct((m, n), BF16))
    outs = pl.pallas_call(
        functools.partial(_mm_ln_kernel, gated=gate is not None, emit_bf16=emit_bf16),
        grid=(m // tm,),
        in_specs=in_specs,
        out_specs=out_specs,
        out_shape=out_shape,
        scratch_shapes=[pltpu.VMEM((tm, n), F32)],
        compiler_params=_params("parallel"),
        name="matmul_residual_ln",
    )(*args)
    return outs if emit_bf16 else (outs[0], None)


FFN_COL_CHUNK = 256
FFN_PANEL_ROWS = 1024


def _ffn_up_kernel(x_ref, wg32_ref, wu32_ref, cw_ref, cb_ref, o_ref, tail_ref, wg_ref, wu_ref, *,
                   tiles_per_seq):
    i = pl.program_id(1)

    @pl.when(i == 0)
    def _():
        wg_ref[...] = wg32_ref[...].astype(BF16)
        wu_ref[...] = wu32_ref[...].astype(BF16)

    seq_start = i % tiles_per_seq == 0
    row = lax.broadcasted_iota(jnp.int32, (SUBLANES, FFN_COL_CHUNK), 0)
    n_chunks = o_ref.shape[1] // FFN_COL_CHUNK
    tails = [None] * n_chunks
    for p in range(0, x_ref.shape[0], FFN_PANEL_ROWS):
        panel = slice(p, p + FFN_PANEL_ROWS)
        x = x_ref[panel, :]
        for c in range(n_chunks):
            cols = slice(c * FFN_COL_CHUNK, (c + 1) * FFN_COL_CHUNK)
            gate = jnp.dot(x, wg_ref[:, cols], preferred_element_type=F32)
            up = jnp.dot(x, wu_ref[:, cols], preferred_element_type=F32)
            tail = jnp.where(seq_start, 0.0, tail_ref[:, cols]) if p == 0 else tails[c]
            tails[c] = gate[FFN_PANEL_ROWS - SUBLANES:, :]
            r1 = pltpu.roll(gate, 1, 0)
            r2 = pltpu.roll(gate, 2, 0)
            h1 = jnp.where(row == 0, tail[SUBLANES - 1:, :], r1[:SUBLANES, :])
            h2 = jnp.where(row == 0, tail[SUBLANES - 2:SUBLANES - 1, :],
                           jnp.where(row == 1, tail[SUBLANES - 1:, :], r2[:SUBLANES, :]))
            g1 = jnp.concatenate([h1, r1[SUBLANES:, :]], axis=0)
            g2 = jnp.concatenate([h2, r2[SUBLANES:, :]], axis=0)
            acc = (cb_ref[:, cols] + g2 * cw_ref[0:1, cols] + g1 * cw_ref[1:2, cols]
                   + gate * cw_ref[2:3, cols])
            o_ref[panel, cols] = (acc * jax.nn.sigmoid(acc) * up).astype(o_ref.dtype)
    for c in range(n_chunks):
        tail_ref[:, c * FFN_COL_CHUNK:(c + 1) * FFN_COL_CHUNK] = tails[c]


def _ffn_up(xb, w_up_all, layer, conv_w, conv_b, seq, tm=2 * FFN_PANEL_ROWS, tn=512):
    t, dm = xb.shape
    dff = w_up_all.shape[2] // 2
    n_col = dff // tn
    w_up = w_up_all
    return pl.pallas_call(
        functools.partial(_ffn_up_kernel, tiles_per_seq=seq // tm),
        grid=(n_col, t // tm),
        in_specs=[
            pl.BlockSpec((tm, dm), lambda j, i: (i, 0)),
            pl.BlockSpec((None, dm, tn), lambda j, i: (layer, 0, j)),
            pl.BlockSpec((None, dm, tn), lambda j, i: (layer, 0, j + n_col)),
            pl.BlockSpec((CONV_WIDTH, tn), lambda j, i: (0, j)),
            pl.BlockSpec((1, tn), lambda j, i: (0, j)),
        ],
        out_specs=pl.BlockSpec((tm, tn), lambda j, i: (i, j)),
        out_shape=jax.ShapeDtypeStruct((t, dff), BF16),
        scratch_shapes=[pltpu.VMEM((SUBLANES, tn), F32), pltpu.VMEM((dm, tn), BF16), pltpu.VMEM((dm, tn), BF16)],
        compiler_params=_params("parallel", "arbitrary"),
        name="ffn_up_convglu",
    )(xb, w_up, w_up, conv_w, conv_b.reshape(1, dff))


ATTN_SPAN = 128
ATTN_TILE = 2048
ATTN_MERGE_ROWS = 256
LOG2_E = 1.4426950408889634


def _block_attention(q, k, v, valid):
    scale = ATTN_HEAD_DIM ** -0.5
    s = lax.dot_general(q.astype(BF16), k.astype(BF16), (((1,), (1,)), ((), ())), preferred_element_type=F32)
    s = jnp.where(valid, s, MASK_VALUE)
    m = jnp.max(s, axis=-1, keepdims=True)
    p = jnp.exp2((s - m) * (scale * LOG2_E))
    l = jnp.sum(p, axis=-1, keepdims=True)
    o = jnp.dot(p.astype(BF16), v.astype(BF16), preferred_element_type=F32) * (1.0 / l)
    return o, m * scale + jnp.log(l)


def _attn_kernel(q1, k1, v1, kp1, vp1, q4, k4, v4, kp4, vp4, q16, k16, v16, kp16, vp16,
                 o_ref, og, lg, x4, *, tiles_per_seq):
    span, hd = ATTN_SPAN, ATTN_HEAD_DIM
    first_tile = pl.program_id(1) % tiles_per_seq == 0
    qi = lax.broadcasted_iota(jnp.int32, (span, 2 * span), 0)
    kj = lax.broadcasted_iota(jnp.int32, (span, 2 * span), 1)
    valid = (kj <= qi + span) & (kj >= qi)
    valid_edge = valid & (kj >= jnp.where(first_tile, span, 0))

    def emit(g, rows, q, k, v, mask):
        o, lse = _block_attention(q, k, v, mask)
        og[g, rows, :] = o
        lg[g, rows, :] = jnp.broadcast_to(lse, (span, hd))

    def cat(a, b):
        return jnp.concatenate([a, b], axis=0)

    n1 = ATTN_TILE // span
    rows = pl.ds(0, span)
    emit(0, rows, q1[rows, :], cat(kp1[...], k1[rows, :]), cat(vp1[...], v1[rows, :]), valid_edge)

    for j in range(1, n1):
        rows = pl.ds(j * span, span)
        keys = pl.ds((j - 1) * span, 2 * span)
        emit(0, rows, q1[rows, :], k1[keys, :], v1[keys, :], valid)

    d4 = 4
    nb4 = ATTN_TILE // (span * d4)
    for r in range(d4):
        rows = pl.ds(r, span, stride=d4)
        emit(1, rows, q4[rows, :], cat(kp4[rows, :], k4[rows, :]), cat(vp4[rows, :], v4[rows, :]), valid_edge)
        for j in range(1, nb4):
            rows = pl.ds(j * span * d4 + r, span, stride=d4)
            keys = pl.ds((j - 1) * span * d4 + r, 2 * span, stride=d4)
            emit(1, rows, q4[rows, :], k4[keys, :], v4[keys, :], valid)

    d16 = 16
    quarter = ATTN_TILE // d4
    for a, src in enumerate((q16, k16, v16, kp16, vp16)):
        for r4 in range(d4):
            x4[a, r4 * quarter:(r4 + 1) * quarter, :] = src[pl.ds(r4, quarter, stride=d4), :]
    for r in range(d16):
        sub = pl.ds((r % d4) * quarter + r // d4, span, stride=d4)
        o, lse = _block_attention(x4[0, sub, :], cat(x4[3, sub, :], x4[1, sub, :]),
                                  cat(x4[4, sub, :], x4[2, sub, :]), valid_edge)
        x4[0, sub, :] = o
        x4[3, sub, :] = jnp.broadcast_to(lse, (span, hd))
    for r4 in range(d4):
        og[2, pl.ds(r4, quarter, stride=d4), :] = x4[0, r4 * quarter:(r4 + 1) * quarter, :]
        lg[2, pl.ds(r4, quarter, stride=d4), :] = x4[3, r4 * quarter:(r4 + 1) * quarter, :]

    def merge(c, carry):
        rows = pl.ds(pl.multiple_of(c * ATTN_MERGE_ROWS, ATTN_MERGE_ROWS), ATTN_MERGE_ROWS)
        l0, l1, l2 = lg[0, rows, :], lg[1, rows, :], lg[2, rows, :]
        m = jnp.maximum(jnp.maximum(l0, l1), l2)
        e0, e1, e2 = jnp.exp(l0 - m), jnp.exp(l1 - m), jnp.exp(l2 - m)
        num = e0 * og[0, rows, :] + e1 * og[1, rows, :] + e2 * og[2, rows, :]
        o_ref[rows, :] = (num / (e0 + e1 + e2)).astype(o_ref.dtype)
        return carry

    lax.fori_loop(0, ATTN_TILE // ATTN_MERGE_ROWS, merge, 0)


def _dilated_attention(qkv, seq):
    t, width = qkv.shape
    hd, span, tile = ATTN_HEAD_DIM, ATTN_SPAN, ATTN_TILE
    heads = width // (3 * len(DILATED_GROUPS) * hd)
    tiles_per_seq = seq // tile

    def col(group, which):
        return lambda h: (group * 3 + which) * heads + h

    def cur(group, which):
        return pl.BlockSpec((tile, hd), lambda h, i: (i, col(group, which)(h)))

    def prev(group, which, rows):
        per = tile // rows
        return pl.BlockSpec((rows, hd), lambda h, i: (jnp.maximum(i * per - 1, 0), col(group, which)(h)))

    in_specs = []
    for group, (window, dilation) in enumerate(DILATED_GROUPS):
        assert window // dilation == span and tile % window == 0
        in_specs += [cur(group, 0), cur(group, 1), cur(group, 2), prev(group, 1, window), prev(group, 2, window)]
    return pl.pallas_call(
        functools.partial(_attn_kernel, tiles_per_seq=tiles_per_seq),
        grid=(heads, t // tile),
        in_specs=in_specs,
        out_specs=pl.BlockSpec((tile, hd), lambda h, i: (i, h)),
        out_shape=jax.ShapeDtypeStruct((t, heads * hd), BF16),
        scratch_shapes=[pltpu.VMEM((len(DILATED_GROUPS), tile, hd), F32)] * 2 + [pltpu.VMEM((5, tile, hd), F32)],
        compiler_params=_params("parallel", "parallel"),
        name="dilated_attention",
    )(*([qkv] * len(in_specs)))


def _pad_rank(p, q):
    r = p.shape[1]
    if r % LORA_PAD:
        extra = LORA_PAD - r % LORA_PAD
        p = jnp.pad(p, ((0, 0), (0, extra)))
        q = jnp.pad(q, ((0, extra), (0, 0)))
    return p.astype(BF16), q.astype(BF16)


def _rwkv_layer(x, mu, w_rkv, w0, w1, w2, a0, a1, a2, g1, g2, k_k, k_a, r_k, gn_g, gn_b, w_out, ln_g, ln_b):
    bsz, seq, dm = x.shape
    n = RWKV_HEAD_SIZE
    heads = dm // n
    t = bsz * seq
    def cols(wt):
        lead = wt.shape[:-1]
        return wt.reshape(*lead, heads, n).swapaxes(-1, -2).reshape(*lead, dm)

    mixed = _token_shift_mix(x, mu[jnp.array([0, 2, 3, 1, 4, 5])])
    rkv = _batched_matmul(mixed, cols(w_rkv.astype(BF16)), 3, F32, rows_per_seq=seq)
    w1b, w2b = _pad_rank(w1, cols(w2))
    a1b, a2b = _pad_rank(a1, cols(a2))
    g1b, g2b = _pad_rank(g1, cols(g2))
    decay, a, g = _rwkv_lora(mixed, w1b, w2b, cols(w0).reshape(1, dm), a1b, a2b, cols(a0).reshape(1, dm),
                             g1b, g2b, seq)

    def lanes_param(p):
        return jnp.repeat(jnp.tile(p.reshape(heads, n).T, (1, bsz)), SUBLANES, axis=0)

    y = _rwkv_scan_pipelined(rkv, decay, a, lanes_param(k_k), lanes_param(k_a), lanes_param(r_k),
                             lanes_param(gn_g), lanes_param(gn_b), steps=min(64, seq))
    w_out_rows = w_out.reshape(heads, n, dm).swapaxes(0, 1).reshape(dm, dm)
    return _matmul_residual_ln(y, w_out_rows.astype(BF16), x.reshape(t, dm), ln_g, ln_b, gate=g,
                               tm=min(512, seq))


def _attn_layer(x32, xb, w_in, w_out, ln_g, ln_b, bsz, seq):
    t, dm = x32.shape
    qkv = _batched_matmul(xb[None], w_in[None], 1, F32)[0]
    merged = _dilated_attention(qkv, seq)
    return _matmul_residual_ln(merged, w_out.astype(BF16), x32, ln_g, ln_b, tm=512)


def _ffn_layer(x32, xb, w_up_all, layer, conv_w, conv_b, w_down_all, ln_g, ln_b, seq, emit_bf16):
    act = _ffn_up(xb, w_up_all, layer, conv_w, conv_b, seq)
    return _matmul_residual_ln(act, w_down_all.astype(BF16), x32, ln_g, ln_b, tm=256, emit_bf16=emit_bf16,
                               layer=layer)


def kernel(x, rwkv_mu, rwkv_w_rkv, rwkv_w0, rwkv_w1, rwkv_w2, rwkv_a0, rwkv_a1, rwkv_a2, rwkv_g1, rwkv_g2, rwkv_k_k, rwkv_k_a, rwkv_r_k, rwkv_gn_g, rwkv_gn_b, rwkv_w_out, attn_w_in, attn_w_out, ffn_w_up, ffn_conv_w, ffn_conv_b, ffn_w_down, ln_mix_g, ln_mix_b, ln_ffn_g, ln_ffn_b):
    bsz, seq, dm = x.shape
    x32, xb = _rwkv_layer(x, rwkv_mu[0], rwkv_w_rkv[0], rwkv_w0[0], rwkv_w1[0], rwkv_w2[0], rwkv_a0[0],
                          rwkv_a1[0], rwkv_a2[0], rwkv_g1[0], rwkv_g2[0], rwkv_k_k[0], rwkv_k_a[0],
                          rwkv_r_k[0], rwkv_gn_g[0], rwkv_gn_b[0], rwkv_w_out[0], ln_mix_g[0], ln_mix_b[0])
    x32, xb = _ffn_layer(x32, xb, ffn_w_up, 0, ffn_conv_w[0], ffn_conv_b[0], ffn_w_down,
                         ln_ffn_g[0], ln_ffn_b[0], seq, True)
    x32, xb = _attn_layer(x32, xb, attn_w_in[0], attn_w_out[0], ln_mix_g[1], ln_mix_b[1], bsz, seq)
    x32, _ = _ffn_layer(x32, xb, ffn_w_up, 1, ffn_conv_w[1], ffn_conv_b[1], ffn_w_down,
                        ln_ffn_g[1], ln_ffn_b[1], seq, False)
    return x32.reshape(bsz, seq, dm)
```

```python
import functools

import jax
import jax.numpy as jnp
from jax import lax
from jax.experimental import pallas as pl
from jax.experimental.pallas import tpu as pltpu

F32 = jnp.float32
BF16 = jnp.bfloat16

V7X_VMEM_BYTES = 64 * 1024 * 1024
VMEM_LIMIT_BYTES = V7X_VMEM_BYTES - 8 * 1024 * 1024
SUBLANES = 8
LANES = 128

RWKV_HEAD_SIZE = 64
N_SHIFT_MIX = 6
GN_EPS = 64e-5
ATTN_HEAD_DIM = 128
DILATED_GROUPS = ((128, 1), (512, 4), (2048, 16))
MASK_VALUE = -1e30
CONV_WIDTH = 3
LN_EPS = 1e-5
DEPTH = 2
DEEPNORM_ALPHA = (2.0 * DEPTH) ** 0.25
LORA_PAD = 128
LN_CHUNK_ROWS = 32
MM_LN_PANEL_ROWS = 256


def _params(*semantics):
    return pltpu.CompilerParams(dimension_semantics=semantics, vmem_limit_bytes=VMEM_LIMIT_BYTES)


def _mix_kernel(x_ref, xp_ref, mu_ref, o_ref):
    s = pl.program_id(1)
    x = x_ref[...]
    before = jnp.where(s > 0, xp_ref[SUBLANES - 1:SUBLANES, :], 0.0)
    prev = pltpu.roll(x, 1, 0)
    row = lax.broadcasted_iota(jnp.int32, x.shape, 0)
    prev = jnp.where(row == 0, before, prev)
    xx = prev - x
    for c in range(N_SHIFT_MIX):
        o_ref[c] = (x + xx * mu_ref[c:c + 1, :]).astype(o_ref.dtype)


def _token_shift_mix(x, mu, ts=512):
    bsz, seq, dm = x.shape
    ts = min(ts, seq)
    nst = seq // ts
    return pl.pallas_call(
        _mix_kernel,
        grid=(bsz, nst),
        in_specs=[
            pl.BlockSpec((None, ts, dm), lambda b, s: (b, s, 0)),
            pl.BlockSpec((None, SUBLANES, dm), lambda b, s: (b, jnp.maximum(s * (ts // SUBLANES) - 1, 0), 0)),
            pl.BlockSpec((N_SHIFT_MIX, dm), lambda b, s: (0, 0)),
        ],
        out_specs=pl.BlockSpec((N_SHIFT_MIX, ts, dm), lambda b, s: (0, b * nst + s, 0)),
        out_shape=jax.ShapeDtypeStruct((N_SHIFT_MIX, bsz * seq, dm), BF16),
        compiler_params=_params("parallel", "parallel"),
        name="token_shift_mix",
    )(x, x, mu)


def _mm_kernel(a_ref, w_ref, o_ref, wb_ref):
    @pl.when(pl.program_id(2) == 0)
    def _():
        wb_ref[...] = w_ref[...].astype(BF16)

    res = jnp.dot(a_ref[...], wb_ref[...], preferred_element_type=F32).astype(o_ref.dtype)
    if len(o_ref.shape) == 2:
        o_ref[...] = res
    else:
        for q in range(o_ref.shape[0]):
            o_ref[q] = res[:, q * LANES:(q + 1) * LANES]


def _batched_matmul(a, w, n_batch, out_dtype, tm=1024, tn=1024, rows_per_seq=None):
    _, m, kdim = a.shape
    tm = min(tm, m)
    n = w.shape[2]
    if rows_per_seq is None:
        out_spec = pl.BlockSpec((None, tm, tn), lambda c, j, i: (c, i, j))
        out_shape = jax.ShapeDtypeStruct((n_batch, m, n), out_dtype)
    else:
        tm = min(tm, rows_per_seq)
        per = rows_per_seq // tm
        out_spec = pl.BlockSpec((None, None, tn // LANES, tm, LANES), lambda c, j, i: (c, i // per, j, i % per, 0))
        out_shape = jax.ShapeDtypeStruct((n_batch, m // rows_per_seq, n // LANES, rows_per_seq, LANES), out_dtype)
    return pl.pallas_call(
        _mm_kernel,
        grid=(n_batch, n // tn, m // tm),
        in_specs=[
            pl.BlockSpec((None, tm, kdim), lambda c, j, i: (c, i, 0)),
            pl.BlockSpec((None, kdim, tn), lambda c, j, i: (c, 0, j)),
        ],
        out_specs=out_spec,
        out_shape=out_shape,
        scratch_shapes=[pltpu.VMEM((kdim, tn), BF16)],
        compiler_params=_params("parallel", "parallel", "arbitrary"),
        name="batched_matmul",
    )(a, w)


def _lora_kernel(xw_ref, xa_ref, xg_ref, w1_ref, w2_ref, w0_ref, a1_ref, a2_ref, a0_ref,
                 g1_ref, g2_ref, decay_ref, a_ref, g_ref):
    def two_stage(x_ref, p_ref, q_ref, act):
        h = act(jnp.dot(x_ref[...], p_ref[...], preferred_element_type=F32))
        return jnp.dot(h.astype(BF16), q_ref[...], preferred_element_type=F32)

    def split_lanes(o_ref, val):
        for q in range(o_ref.shape[0]):
            o_ref[q] = val[:, q * LANES:(q + 1) * LANES]

    z = w0_ref[...] + two_stage(xw_ref, w1_ref, w2_ref, jnp.tanh)
    split_lanes(decay_ref, jnp.exp(-jnp.exp(-0.5) * jax.nn.sigmoid(z)))
    split_lanes(a_ref, jax.nn.sigmoid(a0_ref[...] + two_stage(xa_ref, a1_ref, a2_ref, lambda h: h)))
    g_ref[...] = two_stage(xg_ref, g1_ref, g2_ref, jax.nn.sigmoid).astype(g_ref.dtype)


def _rwkv_lora(mixed, w1, w2, w0, a1, a2, a0, g1, g2, seq, tm=512):
    _, t, dm = mixed.shape
    tm = min(tm, seq)
    per = seq // tm

    def full(arr):
        return pl.BlockSpec(arr.shape, lambda i: (0,) * arr.ndim)

    def mix_spec(c):
        return pl.BlockSpec((None, tm, dm), lambda i: (c, i, 0))

    out_spec = pl.BlockSpec((tm, dm), lambda i: (i, 0))
    out = jax.ShapeDtypeStruct((t, dm), BF16)
    split_spec = pl.BlockSpec((None, dm // LANES, tm, LANES), lambda i: (i // per, 0, i % per, 0))
    split = jax.ShapeDtypeStruct((t // seq, dm // LANES, seq, LANES), F32)
    return pl.pallas_call(
        _lora_kernel,
        grid=(t // tm,),
        in_specs=[mix_spec(3), mix_spec(4), mix_spec(5), full(w1), full(w2), full(w0),
                  full(a1), full(a2), full(a0), full(g1), full(g2)],
        out_specs=[split_spec, split_spec, out_spec],
        out_shape=[split, split, out],
        compiler_params=_params("parallel"),
        name="rwkv_lora",
    )(mixed, mixed, mixed, w1, w2, w0, a1, a2, a0, g1, g2)


SCAN_V_GROUPS = 4
SCAN_PARTIALS = 2


SCAN_BATCH = 4
SCAN_CHUNK = LANES // SCAN_BATCH


def _swap_halves(a0, a1, a2, a3, low_half):
    r0, r1, r2, r3 = (pltpu.roll(a, 2 * SCAN_CHUNK, 1) for a in (a0, a1, a2, a3))
    return (jnp.where(low_half, a0, r2), jnp.where(low_half, a1, r3),
            jnp.where(low_half, r0, a2), jnp.where(low_half, r1, a3))


def _swap_chunks(p0, p1, p2, p3, even_chunk):
    return (jnp.where(even_chunk, p0, pltpu.roll(p1, SCAN_CHUNK, 1)),
            jnp.where(even_chunk, pltpu.roll(p0, 3 * SCAN_CHUNK, 1), p1),
            jnp.where(even_chunk, p2, pltpu.roll(p3, SCAN_CHUNK, 1)),
            jnp.where(even_chunk, pltpu.roll(p2, 3 * SCAN_CHUNK, 1), p3))


def _chunk_transpose_direct(ins, at_chunk):
    outs = []
    for j in range(SCAN_BATCH):
        picked = [ins[b] if b == j else pltpu.roll(ins[b], ((b - j) % SCAN_BATCH) * SCAN_CHUNK, 1)
                  for b in range(SCAN_BATCH)]
        out = picked[SCAN_BATCH - 1]
        for b in range(SCAN_BATCH - 2, -1, -1):
            out = jnp.where(at_chunk[b], picked[b], out)
        outs.append(out)
    return tuple(outs)


Q_KK, Q_WR, Q_W, Q_B, Q_K2, Q_V = range(6)
R_INV2, R_BR, R_KR, R_BONUS = range(4)
SCAN_UNIT_Q = 2
SCAN_Q = RWKV_HEAD_SIZE // SCAN_BATCH


def _scan_pipelined_kernel(r_ref, k_ref, v_ref, w_ref, a_ref, kkp_ref, kap_ref, rkp_ref, gng_ref, gnb_ref,
                           y_ref, st_ref, buf_a, buf_b, rows_a, rows_b, y_s, yn_s, *, steps):
    n = RWKV_HEAD_SIZE
    n_vg = n // SUBLANES
    n_groups = steps // SUBLANES
    lane = lax.broadcasted_iota(jnp.int32, (SUBLANES, LANES), 1)
    low_half = lane < 2 * SCAN_CHUNK
    even_chunk = (lane & SCAN_CHUNK) == 0
    at_chunk = [(lane >= b * SCAN_CHUNK) & (lane < (b + 1) * SCAN_CHUNK) for b in range(SCAN_BATCH)]

    @pl.when(pl.program_id(0) == 0)
    def _():
        st_ref[...] = jnp.zeros_like(st_ref)
        yn_s[...] = jnp.zeros_like(yn_s)

    def tile_rows(i):
        if isinstance(i, int):
            return pl.ds(i * SUBLANES, SUBLANES)
        return pl.ds(pl.multiple_of(i * SUBLANES, SUBLANES), SUBLANES)

    def transposed(tiles_per_q, direct):
        if direct:
            return [_chunk_transpose_direct(t, at_chunk) for t in tiles_per_q]
        halves = [_swap_halves(*t, low_half) for t in tiles_per_q]
        return [_swap_chunks(*h, even_chunk) for h in halves]

    def regroup(s0, qs, dst, direct):
        srcs = (r_ref, k_ref, v_ref, w_ref, a_ref)
        tiles = [tuple(src[bi, q, pl.ds(s0, SUBLANES), :] for bi in range(SCAN_BATCH))
                 for q in qs for src in srcs]
        regrouped = transposed(tiles, direct)
        sums = None
        for qi, q in enumerate(qs):
            rq, kq, vq, wq, aq = regrouped[qi * len(srcs):(qi + 1) * len(srcs)]
            for j in range(SCAN_BATCH):
                rows = tile_rows(q * SCAN_BATCH + j)
                r, k, v, w, a = rq[j], kq[j], vq[j], wq[j], aq[j]
                kk0 = k * kkp_ref[rows, :]
                k2 = k * (1.0 + (a - 1.0) * kap_ref[rows, :])
                b0 = kk0 * a
                rk2 = r * k2
                dst[Q_KK, rows, :] = kk0
                dst[Q_WR, rows, :] = w * r
                dst[Q_W, rows, :] = w
                dst[Q_B, rows, :] = b0
                dst[Q_K2, rows, :] = k2
                dst[Q_V, rows, :] = v
                terms = (kk0 * kk0, b0 * r, rk2, rk2 * rkp_ref[rows, :])
                sums = terms if sums is None else tuple(x + y for x, y in zip(sums, terms))
        return sums

    def finish_rows(sums, rows_ref):
        ss, br, kr, bonus = sums
        rows_ref[R_INV2] = 1.0 / jnp.maximum(ss, 1e-24)
        rows_ref[R_BR] = br
        rows_ref[R_KR] = kr
        rows_ref[R_BONUS] = bonus

    def unregroup(s0, qs, direct):
        tiles = [tuple(yn_s[tile_rows(q * SCAN_BATCH + j), :] for j in range(SCAN_BATCH)) for q in qs]
        for q, outs in zip(qs, transposed(tiles, direct)):
            for bi, val in enumerate(outs):
                y_ref[bi, q, pl.ds(s0, SUBLANES), :] = val

    def step(t, cur, cur_rows):
        inv2_row = cur_rows[R_INV2, pl.ds(t, 1), :]
        br_row = cur_rows[R_BR, pl.ds(t, 1), :]
        kr_row = cur_rows[R_KR, pl.ds(t, 1), :]
        for g0 in range(0, n_vg, SCAN_V_GROUPS):
            groups = range(g0, g0 + SCAN_V_GROUPS)
            acc_sa = {g: [None] * SCAN_PARTIALS for g in groups}
            acc_y = {g: [None] * SCAN_PARTIALS for g in groups}
            for ki in range(n):
                kk_row = cur[Q_KK, pl.ds(ki * SUBLANES + t, 1), :]
                wr_row = cur[Q_WR, pl.ds(ki * SUBLANES + t, 1), :]
                p = ki % SCAN_PARTIALS
                for g in groups:
                    s = st_ref[ki, g * SUBLANES:(g + 1) * SUBLANES, :]
                    t_sa = s * kk_row
                    t_y = s * wr_row
                    acc_sa[g][p] = t_sa if acc_sa[g][p] is None else acc_sa[g][p] + t_sa
                    acc_y[g][p] = t_y if acc_y[g][p] is None else acc_y[g][p] + t_y
            sa = {}
            vv = {}
            for g in groups:
                v_rows = pl.ds(g * SUBLANES * SUBLANES + t, SUBLANES, stride=SUBLANES)
                sa[g] = -functools.reduce(lambda x, y: x + y, acc_sa[g]) * inv2_row
                vv[g] = cur[Q_V, v_rows, :]
                y_s[v_rows, :] = (functools.reduce(lambda x, y: x + y, acc_y[g])
                                  + sa[g] * br_row + vv[g] * kr_row)
            for ki in range(n):
                w_row = cur[Q_W, pl.ds(ki * SUBLANES + t, 1), :]
                b_row = cur[Q_B, pl.ds(ki * SUBLANES + t, 1), :]
                k_row = cur[Q_K2, pl.ds(ki * SUBLANES + t, 1), :]
                for g in groups:
                    s = st_ref[ki, g * SUBLANES:(g + 1) * SUBLANES, :]
                    st_ref[ki, g * SUBLANES:(g + 1) * SUBLANES, :] = s * w_row + sa[g] * b_row + vv[g] * k_row

    def normalise(cur, cur_rows):
        tot = None
        for vi in range(n):
            yv = y_s[vi * SUBLANES:(vi + 1) * SUBLANES, :]
            tot = yv if tot is None else tot + yv
        mean = tot * (1.0 / n)
        sq = None
        for vi in range(n):
            yc = y_s[vi * SUBLANES:(vi + 1) * SUBLANES, :] - mean
            sq = yc * yc if sq is None else sq + yc * yc
        rstd = lax.rsqrt(sq * (1.0 / n) + GN_EPS)
        bonus_v = cur_rows[R_BONUS]
        for vi in range(n):
            rows = slice(vi * SUBLANES, (vi + 1) * SUBLANES)
            yn_s[rows, :] = ((y_s[rows, :] - mean) * rstd * gng_ref[rows, :] + gnb_ref[rows, :]
                             + bonus_v * cur[Q_V, rows, :])

    def run_group(g, cur, cur_rows, nxt, nxt_rows):
        s0 = pl.multiple_of(g * SUBLANES, SUBLANES)
        s_next = pl.multiple_of(jnp.minimum(s0 + SUBLANES, steps - SUBLANES), SUBLANES)
        s_prev = pl.multiple_of(jnp.maximum(s0 - SUBLANES, 0), SUBLANES)

        def body(t, sums):
            qs = [SCAN_UNIT_Q * t + v for v in range(SCAN_UNIT_Q)]
            part = regroup(s_next, qs, nxt, True)
            unregroup(s_prev, qs, True)
            step(t, cur, cur_rows)
            return tuple(x + y for x, y in zip(sums, part))

        zero = jnp.zeros((SUBLANES, LANES), F32)
        sums = lax.fori_loop(0, SUBLANES, body, (zero, zero, zero, zero))
        finish_rows(sums, nxt_rows)
        normalise(cur, cur_rows)

    half = SCAN_Q // 2
    first = regroup(0, list(range(half)), buf_a, False)
    second = regroup(0, list(range(half, SCAN_Q)), buf_a, False)
    finish_rows(tuple(x + y for x, y in zip(first, second)), rows_a)

    def two_groups(i, carry):
        run_group(2 * i, buf_a, rows_a, buf_b, rows_b)
        run_group(2 * i + 1, buf_b, rows_b, buf_a, rows_a)
        return carry

    lax.fori_loop(0, n_groups // 2, two_groups, 0)

    unregroup(steps - SUBLANES, list(range(half)), False)
    unregroup(steps - SUBLANES, list(range(half, SCAN_Q)), False)


def _rwkv_scan_pipelined(rkv, w, a, kkp, kap, rkp, gng, gnb, steps=64):
    bsz, nq, seq, lanes = w.shape
    n = RWKV_HEAD_SIZE
    assert bsz == SCAN_BATCH and nq == SCAN_Q and lanes == LANES and (steps // SUBLANES) % 2 == 0

    def rkv_spec(c):
        return pl.BlockSpec((None, bsz, nq, steps, lanes), lambda i: (c, 0, 0, i, 0))

    seq_spec = pl.BlockSpec((bsz, nq, steps, lanes), lambda i: (0, 0, i, 0))
    par_spec = pl.BlockSpec((n * SUBLANES, LANES), lambda i: (0, 0))
    rows = pltpu.VMEM((n * SUBLANES, LANES), F32)
    group_buf = pltpu.VMEM((6, n * SUBLANES, LANES), F32)
    group_rows = pltpu.VMEM((4, SUBLANES, LANES), F32)
    return pl.pallas_call(
        functools.partial(_scan_pipelined_kernel, steps=steps),
        grid=(seq // steps,),
        in_specs=[rkv_spec(0), rkv_spec(1), rkv_spec(2), seq_spec, seq_spec] + [par_spec] * 5,
        out_specs=seq_spec,
        out_shape=jax.ShapeDtypeStruct((bsz, nq, seq, lanes), F32),
        scratch_shapes=[pltpu.VMEM((n, n, LANES), F32), group_buf, group_buf, group_rows, group_rows, rows, rows],
        compiler_params=_params("arbitrary"),
        name="rwkv7_scan",
    )(rkv, rkv, rkv, w, a, kkp, kap, rkp, gng, gnb)


def _mm_ln_kernel(*refs, gated, emit_bf16):
    refs = list(refs)
    a_ref = refs.pop(0)
    gate_ref = refs.pop(0) if gated else None
    w_ref, x_ref, g_ref, b_ref, o_ref = refs[:5]
    ob_ref = refs[5] if emit_bf16 else None
    acc_ref = refs[-1]

    if gated:
        a = jnp.concatenate([a_ref[q] for q in range(a_ref.shape[0])], axis=1)
        a = (a * gate_ref[...]).astype(BF16)
    else:
        a = a_ref[...]

    panel_rows = min(MM_LN_PANEL_ROWS, acc_ref.shape[0])
    for p in range(0, acc_ref.shape[0], panel_rows):
        panel = slice(p, p + panel_rows)
        acc_ref[panel, :] = jnp.dot(a[panel, :], w_ref[...], preferred_element_type=F32)
        for c in range(p // LN_CHUNK_ROWS, (p + panel_rows) // LN_CHUNK_ROWS):
            rows = slice(c * LN_CHUNK_ROWS, (c + 1) * LN_CHUNK_ROWS)
            z = DEEPNORM_ALPHA * x_ref[rows, :] + acc_ref[rows, :]
            mu = jnp.mean(z, axis=-1, keepdims=True)
            zc = z - mu
            var = jnp.mean(zc * zc, axis=-1, keepdims=True)
            out = zc * lax.rsqrt(var + LN_EPS) * g_ref[...] + b_ref[...]
            o_ref[rows, :] = out
            if emit_bf16:
                ob_ref[rows, :] = out.astype(BF16)


def _matmul_residual_ln(a, w, x, g, b, *, gate=None, tm, emit_bf16=True, layer=0):
    if w.ndim == 2:
        w, layer = w[None], 0
    kdim, n = w.shape[1:]
    m = x.shape[0]
    row = lambda i: (i, 0)
    fixed = lambda i: (0, 0)
    args = [a]
    if gate is None:
        in_specs = [pl.BlockSpec((tm, kdim), row)]
    else:
        per = a.shape[2] // tm
        in_specs = [pl.BlockSpec((None, kdim // LANES, tm, LANES), lambda i: (i // per, 0, i % per, 0)),
                    pl.BlockSpec((tm, kdim), row)]
        args.append(gate)
    in_specs += [pl.BlockSpec((None, kdim, n), lambda i: (layer, 0, 0), pipeline_mode=pl.Buffered(1)),
                 pl.BlockSpec((tm, n), row),
                 pl.BlockSpec((1, n), fixed),
                 pl.BlockSpec((1, n), fixed)]
    args += [w, x, g.reshape(1, n), b.reshape(1, n)]
    out_specs = [pl.BlockSpec((tm, n), row)]
    out_shape = [jax.ShapeDtypeStruct((m, n), F32)]
    if emit_bf16:
        out_specs.append(pl.BlockSpec((tm, n), row))
        out_shape.append(jax.ShapeDtypeStruct((m, n), BF16))
    outs = pl.pallas_call(
        functools.partial(_mm_ln_kernel, gated=gate is not None, emit_bf16=emit_bf16),
        grid=(m // tm,),
        in_specs=in_specs,
        out_specs=out_specs,
        out_shape=out_shape,
        scratch_shapes=[pltpu.VMEM((tm, n), F32)],
        compiler_params=_params("parallel"),
        name="matmul_residual_ln",
    )(*args)
    return outs if emit_bf16 else (outs[0], None)


FFN_COL_CHUNK = 256
FFN_PANEL_ROWS = 1024


def _ffn_up_kernel(x_ref, wg32_ref, wu32_ref, cw_ref, cb_ref, o_ref, tail_ref, wg_ref, wu_ref, *,
                   tiles_per_seq):
    i = pl.program_id(1)

    @pl.when(i == 0)
    def _():
        wg_ref[...] = wg32_ref[...].astype(BF16)
        wu_ref[...] = wu32_ref[...].astype(BF16)

    seq_start = i % tiles_per_seq == 0
    row = lax.broadcasted_iota(jnp.int32, (SUBLANES, FFN_COL_CHUNK), 0)
    n_chunks = o_ref.shape[1] // FFN_COL_CHUNK
    tails = [None] * n_chunks
    for p in range(0, x_ref.shape[0], FFN_PANEL_ROWS):
        panel = slice(p, p + FFN_PANEL_ROWS)
        x = x_ref[panel, :]
        for c in range(n_chunks):
            cols = slice(c * FFN_COL_CHUNK, (c + 1) * FFN_COL_CHUNK)
            gate = jnp.dot(x, wg_ref[:, cols], preferred_element_type=F32)
            up = jnp.dot(x, wu_ref[:, cols], preferred_element_type=F32)
            tail = jnp.where(seq_start, 0.0, tail_ref[:, cols]) if p == 0 else tails[c]
            tails[c] = gate[FFN_PANEL_ROWS - SUBLANES:, :]
            r1 = pltpu.roll(gate, 1, 0)
            r2 = pltpu.roll(gate, 2, 0)
            h1 = jnp.where(row == 0, tail[SUBLANES - 1:, :], r1[:SUBLANES, :])
            h2 = jnp.where(row == 0, tail[SUBLANES - 2:SUBLANES - 1, :],
                           jnp.where(row == 1, tail[SUBLANES - 1:, :], r2[:SUBLANES, :]))
            g1 = jnp.concatenate([h1, r1[SUBLANES:, :]], axis=0)
            g2 = jnp.concatenate([h2, r2[SUBLANES:, :]], axis=0)
            acc = (cb_ref[:, cols] + g2 * cw_ref[0:1, cols] + g1 * cw_ref[1:2, cols]
                   + gate * cw_ref[2:3, cols])
            o_ref[panel, cols] = (acc * jax.nn.sigmoid(acc) * up).astype(o_ref.dtype)
    for c in range(n_chunks):
        tail_ref[:, c * FFN_COL_CHUNK:(c + 1) * FFN_COL_CHUNK] = tails[c]


def _ffn_up(xb, w_up_all, layer, conv_w, conv_b, seq, tm=2 * FFN_PANEL_ROWS, tn=512):
    t, dm = xb.shape
    dff = w_up_all.shape[2] // 2
    n_col = dff // tn
    w_up = w_up_all
    return pl.pallas_call(
        functools.partial(_ffn_up_kernel, tiles_per_seq=seq // tm),
        grid=(n_col, t // tm),
        in_specs=[
            pl.BlockSpec((tm, dm), lambda j, i: (i, 0)),
            pl.BlockSpec((None, dm, tn), lambda j, i: (layer, 0, j)),
            pl.BlockSpec((None, dm, tn), lambda j, i: (layer, 0, j + n_col)),
            pl.BlockSpec((CONV_WIDTH, tn), lambda j, i: (0, j)),
            pl.BlockSpec((1, tn), lambda j, i: (0, j)),
        ],
        out_specs=pl.BlockSpec((tm, tn), lambda j, i: (i, j)),
        out_shape=jax.ShapeDtypeStruct((t, dff), BF16),
        scratch_shapes=[pltpu.VMEM((SUBLANES, tn), F32), pltpu.VMEM((dm, tn), BF16), pltpu.VMEM((dm, tn), BF16)],
        compiler_params=_params("parallel", "arbitrary"),
        name="ffn_up_convglu",
    )(xb, w_up, w_up, conv_w, conv_b.reshape(1, dff))


ATTN_SPAN = 128
ATTN_TILE = 2048
ATTN_MERGE_ROWS = 256
LOG2_E = 1.4426950408889634


def _block_attention(q, k, v, valid):
    scale = ATTN_HEAD_DIM ** -0.5
    s = lax.dot_general(q.astype(BF16), k.astype(BF16), (((1,), (1,)), ((), ())), preferred_element_type=F32)
    s = jnp.where(valid, s, MASK_VALUE)
    m = jnp.max(s, axis=-1, keepdims=True)
    p = jnp.exp2((s - m) * (scale * LOG2_E))
    l = jnp.sum(p, axis=-1, keepdims=True)
    o = jnp.dot(p.astype(BF16), v.astype(BF16), preferred_element_type=F32) * (1.0 / l)
    return o, m * scale + jnp.log(l)


def _attn_kernel(q1, k1, v1, kp1, vp1, q4, k4, v4, kp4, vp4, q16, k16, v16, kp16, vp16,
                 o_ref, og, lg, x4, *, tiles_per_seq):
    span, hd = ATTN_SPAN, ATTN_HEAD_DIM
    first_tile = pl.program_id(1) % tiles_per_seq == 0
    qi = lax.broadcasted_iota(jnp.int32, (span, 2 * span), 0)
    kj = lax.broadcasted_iota(jnp.int32, (span, 2 * span), 1)
    valid = (kj <= qi + span) & (kj >= qi)
    valid_edge = valid & (kj >= jnp.where(first_tile, span, 0))

    def emit(g, rows, q, k, v, mask):
        o, lse = _block_attention(q, k, v, mask)
        og[g, rows, :] = o
        lg[g, rows, :] = jnp.broadcast_to(lse, (span, hd))

    def cat(a, b):
        return jnp.concatenate([a, b], axis=0)

    n1 = ATTN_TILE // span
    rows = pl.ds(0, span)
    emit(0, rows, q1[rows, :], cat(kp1[...], k1[rows, :]), cat(vp1[...], v1[rows, :]), valid_edge)

    for j in range(1, n1):
        rows = pl.ds(j * span, span)
        keys = pl.ds((j - 1) * span, 2 * span)
        emit(0, rows, q1[rows, :], k1[keys, :], v1[keys, :], valid)

    d4 = 4
    nb4 = ATTN_TILE // (span * d4)
    for r in range(d4):
        rows = pl.ds(r, span, stride=d4)
        emit(1, rows, q4[rows, :], cat(kp4[rows, :], k4[rows, :]), cat(vp4[rows, :], v4[rows, :]), valid_edge)
        for j in range(1, nb4):
            rows = pl.ds(j * span * d4 + r, span, stride=d4)
            keys = pl.ds((j - 1) * span * d4 + r, 2 * span, stride=d4)
            emit(1, rows, q4[rows, :], k4[keys, :], v4[keys, :], valid)

    d16 = 16
    quarter = ATTN_TILE // d4
    for a, src in enumerate((q16, k16, v16, kp16, vp16)):
        for r4 in range(d4):
            x4[a, r4 * quarter:(r4 + 1) * quarter, :] = src[pl.ds(r4, quarter, stride=d4), :]
    for r in range(d16):
        sub = pl.ds((r % d4) * quarter + r // d4, span, stride=d4)
        o, lse = _block_attention(x4[0, sub, :], cat(x4[3, sub, :], x4[1, sub, :]),
                                  cat(x4[4, sub, :], x4[2, sub, :]), valid_edge)
        x4[0, sub, :] = o
        x4[3, sub, :] = jnp.broadcast_to(lse, (span, hd))
    for r4 in range(d4):
        og[2, pl.ds(r4, quarter, stride=d4), :] = x4[0, r4 * quarter:(r4 + 1) * quarter, :]
        lg[2, pl.ds(r4, quarter, stride=d4), :] = x4[3, r4 * quarter:(r4 + 1) * quarter, :]

    def merge(c, carry):
        rows = pl.ds(pl.multiple_of(c * ATTN_MERGE_ROWS, ATTN_MERGE_ROWS), ATTN_MERGE_ROWS)
        l0, l1, l2 = lg[0, rows, :], lg[1, rows, :], lg[2, rows, :]
        m = jnp.maximum(jnp.maximum(l0, l1), l2)
        e0, e1, e2 = jnp.exp(l0 - m), jnp.exp(l1 - m), jnp.exp(l2 - m)
        num = e0 * og[0, rows, :] + e1 * og[1, rows, :] + e2 * og[2, rows, :]
        o_ref[rows, :] = (num / (e0 + e1 + e2)).astype(o_ref.dtype)
        return carry

    lax.fori_loop(0, ATTN_TILE // ATTN_MERGE_ROWS, merge, 0)


def _dilated_attention(qkv, seq):
    t, width = qkv.shape
    hd, span, tile = ATTN_HEAD_DIM, ATTN_SPAN, ATTN_TILE
    heads = width // (3 * len(DILATED_GROUPS) * hd)
    tiles_per_seq = seq // tile

    def col(group, which):
        return lambda h: (group * 3 + which) * heads + h

    def cur(group, which):
        return pl.BlockSpec((tile, hd), lambda h, i: (i, col(group, which)(h)))

    def prev(group, which, rows):
        per = tile // rows
        return pl.BlockSpec((rows, hd), lambda h, i: (jnp.maximum(i * per - 1, 0), col(group, which)(h)))

    in_specs = []
    for group, (window, dilation) in enumerate(DILATED_GROUPS):
        assert window // dilation == span and tile % window == 0
        in_specs += [cur(group, 0), cur(group, 1), cur(group, 2), prev(group, 1, window), prev(group, 2, window)]
    return pl.pallas_call(
        functools.partial(_attn_kernel, tiles_per_seq=tiles_per_seq),
        grid=(heads, t // tile),
        in_specs=in_specs,
        out_specs=pl.BlockSpec((tile, hd), lambda h, i: (i, h)),
        out_shape=jax.ShapeDtypeStruct((t, heads * hd), BF16),
        scratch_shapes=[pltpu.VMEM((len(DILATED_GROUPS), tile, hd), F32)] * 2 + [pltpu.VMEM((5, tile, hd), F32)],
        compiler_params=_params("parallel", "parallel"),
        name="dilated_attention",
    )(*([qkv] * len(in_specs)))


def _pad_rank(p, q):
    r = p.shape[1]
    if r % LORA_PAD:
        extra = LORA_PAD - r % LORA_PAD
        p = jnp.pad(p, ((0, 0), (0, extra)))
        q = jnp.pad(q, ((0, extra), (0, 0)))
    return p.astype(BF16), q.astype(BF16)


def _rwkv_layer(x, mu, w_rkv, w0, w1, w2, a0, a1, a2, g1, g2, k_k, k_a, r_k, gn_g, gn_b, w_out, ln_g, ln_b):
    bsz, seq, dm = x.shape
    n = RWKV_HEAD_SIZE
    heads = dm // n
    t = bsz * seq
    def cols(wt):
        lead = wt.shape[:-1]
        return wt.reshape(*lead, heads, n).swapaxes(-1, -2).reshape(*lead, dm)

    mixed = _token_shift_mix(x, mu[jnp.array([0, 2, 3, 1, 4, 5])])
    rkv = _batched_matmul(mixed, cols(w_rkv.astype(BF16)), 3, F32, rows_per_seq=seq)
    w1b, w2b = _pad_rank(w1, cols(w2))
    a1b, a2b = _pad_rank(a1, cols(a2))
    g1b, g2b = _pad_rank(g1, cols(g2))
    decay, a, g = _rwkv_lora(mixed, w1b, w2b, cols(w0).reshape(1, dm), a1b, a2b, cols(a0).reshape(1, dm),
                             g1b, g2b, seq)

    def lanes_param(p):
        return jnp.repeat(jnp.tile(p.reshape(heads, n).T, (1, bsz)), SUBLANES, axis=0)

    y = _rwkv_scan_pipelined(rkv, decay, a, lanes_param(k_k), lanes_param(k_a), lanes_param(r_k),
                             lanes_param(gn_g), lanes_param(gn_b), steps=min(64, seq))
    w_out_rows = w_out.reshape(heads, n, dm).swapaxes(0, 1).reshape(dm, dm)
    return _matmul_residual_ln(y, w_out_rows.astype(BF16), x.reshape(t, dm), ln_g, ln_b, gate=g,
                               tm=min(512, seq))


def _attn_layer(x32, xb, w_in, w_out, ln_g, ln_b, bsz, seq):
    t, dm = x32.shape
    qkv = _batched_matmul(xb[None], w_in[None], 1, F32)[0]
    merged = _dilated_attention(qkv, seq)
    return _matmul_residual_ln(merged, w_out.astype(BF16), x32, ln_g, ln_b, tm=512)


def _ffn_layer(x32, xb, w_up_all, layer, conv_w, conv_b, w_down_all, ln_g, ln_b, seq, emit_bf16):
    act = _ffn_up(xb, w_up_all, layer, conv_w, conv_b, seq)
    return _matmul_residual_ln(act, w_down_all.astype(BF16), x32, ln_g, ln_b, tm=256, emit_bf16=emit_bf16,
                               layer=layer)


def kernel(x, rwkv_mu, rwkv_w_rkv, rwkv_w0, rwkv_w1, rwkv_w2, rwkv_a0, rwkv_a1, rwkv_a2, rwkv_g1, rwkv_g2, rwkv_k_k, rwkv_k_a, rwkv_r_k, rwkv_gn_g, rwkv_gn_b, rwkv_w_out, attn_w_in, attn_w_out, ffn_w_up, ffn_conv_w, ffn_conv_b, ffn_w_down, ln_mix_g, ln_mix_b, ln_ffn_g, ln_ffn_b):
    bsz, seq, dm = x.shape
    x32, xb = _rwkv_layer(x, rwkv_mu[0], rwkv_w_rkv[0], rwkv_w0[0], rwkv_w1[0], rwkv_w2[0], rwkv_a0[0],
                          rwkv_a1[0], rwkv_a2[0], rwkv_g1[0], rwkv_g2[0], rwkv_k_k[0], rwkv_k_a[0],
                          rwkv_r_k[0], rwkv_gn_g[0], rwkv_gn_b[0], rwkv_w_out[0], ln_mix_g[0], ln_mix_b[0])
    x32, xb = _ffn_layer(x32, xb, ffn_w_up, 0, ffn_conv_w[0], ffn_conv_b[0], ffn_w_down,
                         ln_ffn_g[0], ln_ffn_b[0], seq, True)
    x32, xb = _attn_layer(x32, xb, attn_w_in[0], attn_w_out[0], ln_mix_g[1], ln_mix_b[1], bsz, seq)
    x32, _ = _ffn_layer(x32, xb, ffn_w_up, 1, ffn_conv_w[1], ffn_conv_b[1], ffn_w_down,
                        ln_ffn_g[1], ln_ffn_b[1], seq, False)
    return x32.reshape(bsz, seq, dm)
```
